```python
import numpy as np
import jax, jax.numpy as jnp
from jax import lax

D_MODEL = 1024
BATCH = 4
SEQ = 4096
DEPTH = 1

PLE_DIM = 256
NSA_HEADS = 8
NSA_GROUPS = 2
NSA_HPG = NSA_HEADS // NSA_GROUPS
HEAD_DIM = 64
CMP_LEN = 32
CMP_STRIDE = 16
CMP_HIDDEN = 256
SEL_LEN = 64
SEL_TOP = 16
WINDOW = 512
Q_BLOCK = 128
FORCE_SCORE = 1e4
CONV_CH = 512
CONV_WIDTH = 31
D_FF = 2816
FFN_CONV_WIDTH = 3
EPS = 1e-6
NEG = -1e30

NSA_Q = NSA_HEADS * HEAD_DIM
NSA_KV = 3 * 2 * NSA_GROUPS * HEAD_DIM
NSA_GATE = 3 * NSA_HEADS
CONV_IN = 2 * CONV_CH
MERGE = 2 * D_MODEL
N_IN = NSA_Q + NSA_KV + NSA_GATE + CONV_IN + MERGE
IN_SPLITS = (NSA_Q, NSA_Q + NSA_KV, NSA_Q + NSA_KV + NSA_GATE, NSA_Q + NSA_KV + NSA_GATE + CONV_IN)

kernel_name = "hybrid_nsa_conformer_convffn_ple"


def rmsnorm(x, g):
    xf = x.astype(jnp.float32)
    y = xf * lax.rsqrt(jnp.mean(xf * xf, axis=-1, keepdims=True) + EPS)
    return (y * g.astype(jnp.float32)).astype(x.dtype)


def layernorm(x, g, b):
    xf = x.astype(jnp.float32)
    mu = jnp.mean(xf, axis=-1, keepdims=True)
    xc = xf - mu
    y = xc * lax.rsqrt(jnp.mean(xc * xc, axis=-1, keepdims=True) + EPS)
    return (y * g.astype(jnp.float32) + b.astype(jnp.float32)).astype(x.dtype)


def causal_dwconv(x, w, b):
    k = w.shape[0]
    c = x.shape[-1]
    y = lax.conv_general_dilated(x, w[:, None, :].astype(x.dtype), window_strides=(1,),
                                 padding=((k - 1, 0),), dimension_numbers=('NWC', 'WIO', 'NWC'),
                                 feature_group_count=c)
    return y + b


def masked_softmax(logits, mask):
    logits = jnp.where(mask, logits.astype(jnp.float32), NEG)
    return jnp.where(mask, jax.nn.softmax(logits, axis=-1), 0.0)


def compress(kv, pos, w1, w2):
    s = kv.shape[2]
    n_cmp = (s - CMP_LEN) // CMP_STRIDE + 1
    idx = np.arange(n_cmp)[:, None] * CMP_STRIDE + np.arange(CMP_LEN)[None, :]
    blocks = kv[:, :, idx] + pos
    flat = blocks.reshape(blocks.shape[0], blocks.shape[1], n_cmp, CMP_LEN * HEAD_DIM)
    return jax.nn.gelu(flat @ w1) @ w2


def cmp_to_sel_matrix(n_cmp, n_sel):
    c0 = np.arange(n_cmp) * CMP_STRIDE
    j0 = np.arange(n_sel) * SEL_LEN
    lo = np.maximum(c0[:, None], j0[None, :])
    hi = np.minimum(c0[:, None] + CMP_LEN, j0[None, :] + SEL_LEN)
    return (np.maximum(hi - lo, 0) / CMP_LEN).astype(np.float32)


def nsa_attention(q, kc, vc, ks, vs, kw, vw):
    b, g, hg, s, dh = q.shape
    scale = dh ** -0.5
    n_cmp = kc.shape[2]
    n_sel = s // SEL_LEN
    top = min(SEL_TOP, n_sel)
    cmp_end = jnp.arange(n_cmp) * CMP_STRIDE + CMP_LEN - 1
    sel_map = jnp.asarray(cmp_to_sel_matrix(n_cmp, n_sel))
    ks_blk = ks.reshape(b, g, n_sel, SEL_LEN, dh)
    vs_blk = vs.reshape(b, g, n_sel, SEL_LEN, dh)
    kw_pad = jnp.pad(kw, ((0, 0), (0, 0), (WINDOW, 0), (0, 0)))
    vw_pad = jnp.pad(vw, ((0, 0), (0, 0), (WINDOW, 0), (0, 0)))
    bi = jnp.arange(b)[:, None, None, None]
    gi = jnp.arange(g)[None, :, None, None]
    jsel = jnp.arange(n_sel)

    def query_block(qb_idx):
        t0 = qb_idx * Q_BLOCK
        qb = lax.dynamic_slice_in_dim(q, t0, Q_BLOCK, axis=3) * scale
        t = t0 + jnp.arange(Q_BLOCK)
        lc = jnp.einsum('bghqd,bgcd->bghqc', qb, kc)
        pc = masked_softmax(lc, cmp_end[None, :] <= t[:, None])
        o_c = jnp.einsum('bghqc,bgcd->bghqd', pc.astype(vc.dtype), vc)
        imp = jnp.einsum('bghqc,cj->bgqj', pc, sel_map)
        cur = (t // SEL_LEN)[:, None]
        forced = (jsel[None, :] == 0) | (jsel[None, :] == cur) | (jsel[None, :] == cur - 1)
        valid = jsel[None, :] <= cur
        imp = jnp.where(valid, jnp.where(forced, FORCE_SCORE, imp), NEG)
        _, sel_idx = lax.top_k(imp, top)
        kg = ks_blk[bi, gi, sel_idx]
        vg = vs_blk[bi, gi, sel_idx]
        ls = jnp.einsum('bghqd,bgqnkd->bghqnk', qb, kg).reshape(b, g, hg, Q_BLOCK, top * SEL_LEN)
        kpos = sel_idx[..., None] * SEL_LEN + jnp.arange(SEL_LEN)
        ms = (kpos <= t[None, None, :, None, None]).reshape(b, g, 1, Q_BLOCK, top * SEL_LEN)
        ps = masked_softmax(ls, ms).reshape(b, g, hg, Q_BLOCK, top, SEL_LEN)
        o_s = jnp.einsum('bghqnk,bgqnkd->bghqd', ps.astype(vg.dtype), vg)
        kwb = lax.dynamic_slice_in_dim(kw_pad, t0, WINDOW + Q_BLOCK, axis=2)
        vwb = lax.dynamic_slice_in_dim(vw_pad, t0, WINDOW + Q_BLOCK, axis=2)
        spos = t0 - WINDOW + jnp.arange(WINDOW + Q_BLOCK)
        dist = t[:, None] - spos[None, :]
        mw = (dist >= 0) & (dist < WINDOW) & (spos[None, :] >= 0)
        lw = jnp.einsum('bghqd,bgkd->bghqk', qb, kwb)
        pw = masked_softmax(lw, mw)
        o_w = jnp.einsum('bghqk,bgkd->bghqd', pw.astype(vwb.dtype), vwb)
        return (o_c, o_s, o_w)

    o_c, o_s, o_w = lax.map(query_block, jnp.arange(s // Q_BLOCK))

    def unblock(o):
        return jnp.transpose(o, (1, 2, 3, 0, 4, 5)).reshape(b, g, hg, s, dh)

    return (unblock(o_c), unblock(o_s), unblock(o_w))


def setup_inputs(seed: int = 0) -> dict:
    key = jax.random.key(seed)
    ks = jax.random.split(key, 32)
    f32 = jnp.float32

    def nrm(k, shape, fan_in):
        return jax.random.normal(k, shape, f32) * (fan_in ** -0.5)

    def gain(k, shape):
        return 1.0 + 0.05 * jax.random.normal(k, shape, f32)

    def small(k, shape):
        return 0.02 * jax.random.normal(k, shape, f32)

    L = DEPTH
    return {
        "x": jax.random.normal(ks[0], (BATCH, SEQ, D_MODEL), f32),
        "p": jax.random.normal(ks[1], (DEPTH, BATCH, SEQ, PLE_DIM), f32),
        "g_mix": gain(ks[2], (L, D_MODEL)),
        "w_in": nrm(ks[3], (L, D_MODEL, N_IN), D_MODEL),
        "cmp_pos_k": small(ks[4], (L, CMP_LEN, HEAD_DIM)),
        "cmp_pos_v": small(ks[5], (L, CMP_LEN, HEAD_DIM)),
        "w_cmp_k1": nrm(ks[6], (L, CMP_LEN * HEAD_DIM, CMP_HIDDEN), CMP_LEN * HEAD_DIM),
        "w_cmp_k2": nrm(ks[7], (L, CMP_HIDDEN, HEAD_DIM), CMP_HIDDEN),
        "w_cmp_v1": nrm(ks[8], (L, CMP_LEN * HEAD_DIM, CMP_HIDDEN), CMP_LEN * HEAD_DIM),
        "w_cmp_v2": nrm(ks[9], (L, CMP_HIDDEN, HEAD_DIM), CMP_HIDDEN),
        "w_o_nsa": nrm(ks[10], (L, NSA_Q, D_MODEL), NSA_Q),
        "conv_w": nrm(ks[11], (L, CONV_WIDTH, CONV_CH), CONV_WIDTH),
        "conv_b": small(ks[12], (L, CONV_CH)),
        "conv_ln_g": gain(ks[13], (L, CONV_CH)),
        "conv_ln_b": small(ks[14], (L, CONV_CH)),
        "w_conv_out": nrm(ks[15], (L, CONV_CH, D_MODEL), CONV_CH),
        "b_conv_out": small(ks[16], (L, D_MODEL)),
        "w_out": nrm(ks[17], (L, D_MODEL, D_MODEL), D_MODEL),
        "g_ffn": gain(ks[18], (L, D_MODEL)),
        "w_up": nrm(ks[19], (L, D_MODEL, 2 * D_FF), D_MODEL),
        "ffn_conv_w": nrm(ks[20], (L, FFN_CONV_WIDTH, 2 * D_FF), FFN_CONV_WIDTH),
        "ffn_conv_b": small(ks[21], (L, 2 * D_FF)),
        "w_down": nrm(ks[22], (L, D_FF, D_MODEL), D_FF),
        "g_ple": gain(ks[23], (L, D_MODEL)),
        "w_ple_gate": nrm(ks[24], (L, D_MODEL, D_MODEL), D_MODEL),
        "w_ple": nrm(ks[25], (L, PLE_DIM, D_MODEL), PLE_DIM),
        "g_final": gain(ks[26], (D_MODEL,)),
    }


def reference(x, p, g_mix, w_in, cmp_pos_k, cmp_pos_v, w_cmp_k1, w_cmp_k2, w_cmp_v1, w_cmp_v2,
              w_o_nsa, conv_w, conv_b, conv_ln_g, conv_ln_b, w_conv_out, b_conv_out, w_out,
              g_ffn, w_up, ffn_conv_w, ffn_conv_b, w_down, g_ple, w_ple_gate, w_ple, g_final):
    b, s, _ = x.shape
    h = x
    for i in range(DEPTH):
        u = rmsnorm(h, g_mix[i])
        z = u @ w_in[i]
        zq, zkv, zg, zc, zm = jnp.split(z, IN_SPLITS, axis=-1)

        q = jnp.transpose(zq.reshape(b, s, NSA_GROUPS, NSA_HPG, HEAD_DIM), (0, 2, 3, 1, 4))
        kv = jnp.transpose(zkv.reshape(b, s, 6, NSA_GROUPS, HEAD_DIM), (2, 0, 3, 1, 4))
        kc = compress(kv[0], cmp_pos_k[i], w_cmp_k1[i], w_cmp_k2[i])
        vc = compress(kv[1], cmp_pos_v[i], w_cmp_v1[i], w_cmp_v2[i])
        o_c, o_s, o_w = nsa_attention(q, kc, vc, kv[2], kv[3], kv[4], kv[5])
        gates = jnp.transpose(jax.nn.sigmoid(zg.reshape(b, s, 3, NSA_GROUPS, NSA_HPG)),
                              (2, 0, 3, 4, 1))[..., None]
        o = gates[0] * o_c + gates[1] * o_s + gates[2] * o_w
        o = jnp.transpose(o, (0, 3, 1, 2, 4)).reshape(b, s, NSA_Q)
        y_a = o @ w_o_nsa[i]

        c_val, c_gate = jnp.split(zc, 2, axis=-1)
        c = c_val * jax.nn.sigmoid(c_gate)
        c = causal_dwconv(c, conv_w[i], conv_b[i])
        c = jax.nn.silu(layernorm(c, conv_ln_g[i], conv_ln_b[i]))
        y_b = c @ w_conv_out[i] + b_conv_out[i]

        ga, gb = jnp.split(jax.nn.sigmoid(zm), 2, axis=-1)
        h = h + (ga * y_a + gb * y_b) @ w_out[i]

        v = rmsnorm(h, g_ffn[i]) @ w_up[i]
        v = causal_dwconv(v, ffn_conv_w[i], ffn_conv_b[i])
        v_gate, v_val = jnp.split(v, 2, axis=-1)
        h = h + (jax.nn.gelu(v_gate) * v_val) @ w_down[i]

        gate = jax.nn.sigmoid(rmsnorm(h, g_ple[i]) @ w_ple_gate[i])
        h = h + gate * (p[i] @ w_ple[i])
    return rmsnorm(h, g_final)
```

```python
import functools

import numpy as np
import jax
import jax.numpy as jnp
from jax import lax
from jax.experimental import pallas as pl
from jax.experimental.pallas import tpu as pltpu

F32 = jnp.float32
BF16 = jnp.bfloat16

NSA_HEADS = 8
NSA_GROUPS = 2
NSA_HPG = NSA_HEADS // NSA_GROUPS
HEAD_DIM = 64
CMP_LEN = 32
CMP_STRIDE = 16
SEL_LEN = 64
SEL_SHIFT = 6
SEL_TOP = 16
WINDOW = 512
FORCE_SCORE = 1e4
CONV_WIDTH = 31
FFN_CONV_WIDTH = 3
EPS = 1e-6
NEG = -1e30
MASK_BIAS = -1e9

LANES = 128
SUBLANES = 8
BF16_ROWS = 16
VMEM_LIMIT = 56 * 1024 * 1024

TM = 512
TQ = 128
TK = 512
WIN_SPAN = 768
WIN_ALIGN = 256
CONV_HALO = 32
FFN_HALO = BF16_ROWS
CONV_ROWS = 64
FF_CHUNK = 256


def _sigmoid(x):
    return 0.5 * (jnp.tanh(0.5 * x) + 1.0)


def _gelu_tanh(x):
    return 0.5 * x * (1.0 + jnp.tanh(np.sqrt(2.0 / np.pi).astype(np.float32) * (x + 0.044715 * (x * x * x))))


def _rms(x, g):
    return x * lax.rsqrt(jnp.mean(x * x, axis=-1, keepdims=True) + EPS) * g


def _dot(a, b):
    return jnp.dot(a, b, preferred_element_type=F32)


def _dot_nt(a, b):
    return lax.dot_general(a, b, (((1,), (1,)), ((), ())), preferred_element_type=F32)


def _resident(shape):
    nd = len(shape)
    return pl.BlockSpec(shape, lambda *_: (0,) * nd, pipeline_mode=pl.Buffered(1))


_Q0, _QN = 0, 512
_CV0, _CVN = 512, 256
_KA0, _KAN = 768, 256
_KB0, _KBN = 1024, 256
_KC0, _KCN = 1280, 256
_ZC0, _ZCN = 1536, 1024
_ZM0, _ZMN = 2560, 2048
_ZG0, _ZGN = 4608, 128
_N_ALL = 4736


def _inproj_kernel(x_ref, g_ref, w_ref, q_ref, cv_ref, ka_ref, kb_ref, kc_ref, c_ref, sg_ref, gt_ref,
                   *, tiles_per_seq):
    u = _rms(x_ref[...], g_ref[...]).astype(BF16)

    def seg(lo, n):
        return _dot(u, w_ref[:, lo:lo + n])

    q_ref[...] = seg(_Q0, _QN).astype(BF16)
    cv_ref[...] = seg(_CV0, _CVN)
    ka = seg(_KA0, _KAN)
    s0 = (pl.program_id(0) % tiles_per_seq) * TM
    row = lax.broadcasted_iota(jnp.int32, ka.shape, 0) + s0
    col = lax.broadcasted_iota(jnp.int32, ka.shape, 1)
    onehot = ((col & HEAD_DIM) != 0) & ((col & (HEAD_DIM - 1)) == (row >> SEL_SHIFT))
    ka_ref[...] = jnp.where(onehot, 1.0, ka).astype(BF16)
    kb_ref[...] = seg(_KB0, _KBN).astype(BF16)
    kc_ref[...] = seg(_KC0, _KCN).astype(BF16)
    zc = seg(_ZC0, _ZCN)
    half = _ZCN // 2
    c_ref[...] = zc[:, :half] * _sigmoid(zc[:, half:])
    sg_ref[...] = _sigmoid(seg(_ZM0, _ZMN))
    gt_ref[...] = _sigmoid(seg(_ZG0, _ZGN))


def _inproj(x2, g_mix, w_all, seq):
    m, d = x2.shape
    row = lambda n: pl.BlockSpec((TM, n), lambda i: (i, 0))
    outs = [(_QN, BF16), (_CVN, F32), (_KAN, BF16), (_KBN, BF16), (_KCN, BF16),
            (_ZCN // 2, F32), (_ZMN, F32), (_ZGN, F32)]
    return pl.pallas_call(
        functools.partial(_inproj_kernel, tiles_per_seq=seq // TM),
        grid=(m // TM,),
        in_specs=[row(d), _resident((1, d)), _resident((d, _N_ALL))],
        out_specs=[row(n) for n, _ in outs],
        out_shape=[jax.ShapeDtypeStruct((m, n), dt) for n, dt in outs],
        compiler_params=pltpu.CompilerParams(dimension_semantics=("parallel",), vmem_limit_bytes=VMEM_LIMIT),
        name="inproj",
    )(x2, g_mix, w_all)


N_CMP_PAD = 256
_CMP_HALF = CMP_LEN // 2


def _compress_kernel(ck_ref, cv_ref, pos_ref, w1_ref, w2_ref, out_ref, *, n_cmp):
    hid = w1_ref.shape[-1]
    first = [jnp.zeros((N_CMP_PAD, hid), F32) for _ in range(4)]
    second = [jnp.zeros((N_CMP_PAD, hid), F32) for _ in range(4)]
    for l in range(_CMP_HALF):
        for kind, src_ref in enumerate((ck_ref, cv_ref)):
            xl = src_ref[pl.ds(l, N_CMP_PAD, stride=CMP_STRIDE), :]
            pcol = slice(kind * LANES, (kind + 1) * LANES)
            xa = (xl + pos_ref[l:l + 1, pcol]).astype(BF16)
            xb = (xl + pos_ref[_CMP_HALF + l:_CMP_HALF + l + 1, pcol]).astype(BF16)
            wa = w1_ref[kind, l * HEAD_DIM:(l + 1) * HEAD_DIM, :]
            wb = w1_ref[kind, (_CMP_HALF + l) * HEAD_DIM:(_CMP_HALF + l + 1) * HEAD_DIM, :]
            for g in range(NSA_GROUPS):
                kg = kind * NSA_GROUPS + g
                cols = slice(g * HEAD_DIM, (g + 1) * HEAD_DIM)
                first[kg] = first[kg] + _dot(xa[:, cols], wa)
                second[kg] = second[kg] + _dot(xb[:, cols], wb)
    rows = lax.broadcasted_iota(jnp.int32, (N_CMP_PAD, LANES), 0)
    for g in range(NSA_GROUPS):
        acts = []
        for kind in range(2):
            kg = kind * NSA_GROUPS + g
            h1 = first[kg] + pltpu.roll(second[kg], N_CMP_PAD - 1, 0)
            acts.append(_gelu_tanh(h1).astype(BF16))
        kcv = _dot(acts[0], w2_ref[0]) + _dot(acts[1], w2_ref[1])
        out_ref[0, g] = jnp.where(rows < n_cmp, kcv, 0.0).astype(BF16)


def _compress(cv, pos, w1, w2, batch, seq):
    n_cmp = (seq - CMP_LEN) // CMP_STRIDE + 1
    assert n_cmp < N_CMP_PAD and seq == N_CMP_PAD * CMP_STRIDE
    return pl.pallas_call(
        functools.partial(_compress_kernel, n_cmp=n_cmp),
        grid=(batch,),
        in_specs=[pl.BlockSpec((seq, LANES), lambda b: (b, 0)), pl.BlockSpec((seq, LANES), lambda b: (b, 1)),
                  _resident(pos.shape), _resident(w1.shape), _resident(w2.shape)],
        out_specs=pl.BlockSpec((1, NSA_GROUPS, N_CMP_PAD, LANES), lambda b: (b, 0, 0, 0)),
        out_shape=jax.ShapeDtypeStruct((batch, NSA_GROUPS, N_CMP_PAD, LANES), BF16),
        compiler_params=pltpu.CompilerParams(dimension_semantics=("parallel",), vmem_limit_bytes=VMEM_LIMIT),
        name="compress",
    )(cv, cv, pos, w1, w2)


N_SEL_PAD = 64


def _attn_kernel(q_ref, kcv_ref, selt_ref, ka_ref, kb_ref, kc_ref, gt_ref, o_ref,
                 qhi_ref, qs_ref, m_ref, l_ref, acc_ref):
    g = pl.program_id(1)
    qb = pl.program_id(2)
    t0 = qb * TQ
    rows = NSA_HPG * TQ
    lane = lax.broadcasted_iota(jnp.int32, (TQ, LANES), 1)
    low = lane < HEAD_DIM

    q = q_ref[...].astype(F32)
    qlo = []
    for h in range(NSA_HPG):
        blk = q[:, (h // 2) * LANES:(h // 2 + 1) * LANES]
        swapped = pltpu.roll(blk, HEAD_DIM, 1)
        lo_h, hi_h = (blk, swapped) if h % 2 == 0 else (swapped, blk)
        qlo.append(lo_h)
        qhi_ref[h * TQ:(h + 1) * TQ, :] = jnp.where(low, 0.0, hi_h).astype(BF16)
    qhi = qhi_ref[...]

    rq = lax.broadcasted_iota(jnp.int32, (rows, 1), 0)
    tq = t0 + (rq & (TQ - 1))

    kcv = kcv_ref[0, 0]
    sc = _dot_nt(qhi, kcv)
    cidx = lax.broadcasted_iota(jnp.int32, (1, N_CMP_PAD), 1)
    mask_c = (cidx * CMP_STRIDE + (CMP_LEN - 1)) <= tq
    sc = jnp.where(mask_c, sc, NEG)
    pc = jnp.where(mask_c, jnp.exp(sc - jnp.max(sc, axis=1, keepdims=True)), 0.0)
    lc = jnp.sum(pc, axis=1, keepdims=True)
    pc = pc / jnp.where(lc > 0.0, lc, 1.0)
    o_c = _dot(pc.astype(BF16), kcv)

    psum = pc[0:TQ]
    for h in range(1, NSA_HPG):
        psum = psum + pc[h * TQ:(h + 1) * TQ]
    selt = selt_ref[...]
    p_hi = psum.astype(BF16)
    rem = psum - p_hi.astype(F32)
    p_mid = rem.astype(BF16)
    p_lo = (rem - p_mid.astype(F32)).astype(BF16)
    imp_t = (_dot_nt(selt, p_hi) + _dot_nt(selt, p_mid)) + _dot_nt(selt, p_lo)
    imp = imp_t[N_SEL_PAD:, :]
    j = lax.broadcasted_iota(jnp.int32, (N_SEL_PAD, TQ), 0)
    cur = (t0 + lax.broadcasted_iota(jnp.int32, (N_SEL_PAD, TQ), 1)) >> SEL_SHIFT
    forced = (j == 0) | (j == cur) | (j == cur - 1)
    imp = jnp.where(j <= cur, jnp.where(forced, FORCE_SCORE, imp), NEG)
    rank = jnp.zeros((N_SEL_PAD, TQ), jnp.int32)
    for i in range(N_SEL_PAD):
        vi = imp[i:i + 1, :]
        before = (vi > imp) | ((vi == imp) & (j > i))
        rank = rank + before.astype(jnp.int32)
    bias_t = jnp.where(rank < SEL_TOP, 0.0, MASK_BIAS)
    bias = jnp.concatenate([jnp.zeros((N_SEL_PAD, TQ), F32), bias_t], axis=0).T
    for h in range(NSA_HPG):
        qs_ref[h * TQ:(h + 1) * TQ, :] = jnp.where(low, qlo[h], bias).astype(BF16)

    m_ref[...] = jnp.full(m_ref.shape, NEG, F32)
    l_ref[...] = jnp.zeros(l_ref.shape, F32)
    acc_ref[...] = jnp.zeros(acc_ref.shape, F32)
    kidx = lax.broadcasted_iota(jnp.int32, (1, TK), 1)

    def sel_tile(kt, causal):
        k0 = pl.multiple_of(kt * TK, TK)
        s = _dot_nt(qs_ref[...], ka_ref[pl.ds(k0, TK), :])
        if causal:
            s = jnp.where(k0 + kidx <= tq, s, NEG)
        m_prev = m_ref[...]
        m_new = jnp.maximum(m_prev, jnp.max(s, axis=1, keepdims=True))
        alpha = jnp.exp(m_prev - m_new)
        p = jnp.exp(s - m_new)
        l_ref[...] = alpha * l_ref[...] + jnp.sum(p, axis=1, keepdims=True)
        acc_ref[...] = alpha * acc_ref[...] + _dot(p.astype(BF16), kb_ref[pl.ds(k0, TK), :])
        m_ref[...] = m_new

    last = qb // (TK // TQ)

    def body(kt, carry):
        sel_tile(kt, causal=False)
        return carry

    lax.fori_loop(0, last, body, 0)
    sel_tile(last, causal=True)
    o_s = acc_ref[...] / l_ref[...]

    kstart = pl.multiple_of(jnp.maximum((qb // (WIN_ALIGN // TQ) - 2) * WIN_ALIGN, 0), WIN_ALIGN)
    sw = _dot_nt(qhi, kb_ref[pl.ds(kstart, WIN_SPAN), :])
    dist = tq - (kstart + lax.broadcasted_iota(jnp.int32, (1, WIN_SPAN), 1))
    sw = jnp.where((dist >= 0) & (dist < WINDOW), sw, NEG)
    pw = jnp.exp(sw - jnp.max(sw, axis=1, keepdims=True))
    lw = jnp.sum(pw, axis=1, keepdims=True)
    o_w = _dot(pw.astype(BF16), kc_ref[pl.ds(kstart, WIN_SPAN), :]) / lw

    gt = gt_ref[...]
    heads = []
    for h in range(NSA_HPG):
        def gate(br):
            c0 = br * NSA_HEADS + h
            c1 = c0 + NSA_HPG
            return jnp.where(g == 0, gt[:, c0:c0 + 1], gt[:, c1:c1 + 1])
        sl = slice(h * TQ, (h + 1) * TQ)
        heads.append(gate(0) * o_c[sl] + gate(1) * o_s[sl] + gate(2) * o_w[sl])
    for pair in range(NSA_HPG // 2):
        packed = jnp.where(low, heads[2 * pair], pltpu.roll(heads[2 * pair + 1], HEAD_DIM, 1))
        o_ref[:, pair * LANES:(pair + 1) * LANES] = packed.astype(BF16)


def _attention(q, kcv, selt, ka, kb, kc, gt, batch, seq):
    nq = seq // TQ
    rows = NSA_HPG * TQ
    assert seq // SEL_LEN == N_SEL_PAD and WIN_SPAN <= seq and TK % TQ == 0
    kv_spec = pl.BlockSpec((seq, LANES), lambda b, g, i: (b, g))
    return pl.pallas_call(
        _attn_kernel,
        grid=(batch, NSA_GROUPS, nq),
        in_specs=[
            pl.BlockSpec((TQ, NSA_HPG * HEAD_DIM), lambda b, g, i: (b * nq + i, g)),
            pl.BlockSpec((1, 1, N_CMP_PAD, LANES), lambda b, g, i: (b, g, 0, 0)),
            pl.BlockSpec(selt.shape, lambda b, g, i: (0, 0)),
            kv_spec, kv_spec, kv_spec,
            pl.BlockSpec((TQ, LANES), lambda b, g, i: (b * nq + i, 0)),
        ],
        out_specs=pl.BlockSpec((TQ, NSA_HPG * HEAD_DIM), lambda b, g, i: (b * nq + i, g)),
        out_shape=jax.ShapeDtypeStruct((batch * seq, NSA_HEADS * HEAD_DIM), BF16),
        scratch_shapes=[
            pltpu.VMEM((rows, LANES), BF16),
            pltpu.VMEM((rows, LANES), BF16),
            pltpu.VMEM((rows, 1), F32),
            pltpu.VMEM((rows, 1), F32),
            pltpu.VMEM((rows, LANES), F32),
        ],
        compiler_params=pltpu.CompilerParams(dimension_semantics=("parallel", "parallel", "arbitrary"),
                                             vmem_limit_bytes=VMEM_LIMIT),
        name="nsa_attention",
    )(q, kcv, selt, ka, kb, kc, gt)


def _merge_kernel(o_ref, c_ref, halo_ref, sg_ref, x_ref, wo_ref, cw_ref, cb_ref, lg_ref, lb_ref, wco_ref,
                  bco_ref, wout_ref, h_ref, cext_ref, cn_ref, *, tiles_per_seq):
    first = (pl.program_id(0) % tiles_per_seq) == 0
    cext_ref[0:CONV_HALO, :] = jnp.where(first, 0.0, halo_ref[...])
    cext_ref[CONV_HALO:, :] = c_ref[...]
    shift = CONV_HALO - (CONV_WIDTH - 1)
    for r in range(TM // CONV_ROWS):
        r0 = r * CONV_ROWS
        acc = jnp.broadcast_to(cb_ref[...], (CONV_ROWS, cb_ref.shape[-1]))
        for k in range(CONV_WIDTH):
            acc = acc + cw_ref[k:k + 1, :] * cext_ref[r0 + shift + k:r0 + shift + k + CONV_ROWS, :]
        xc = acc - jnp.mean(acc, axis=-1, keepdims=True)
        y = xc * lax.rsqrt(jnp.mean(xc * xc, axis=-1, keepdims=True) + EPS) * lg_ref[...] + lb_ref[...]
        cn_ref[r0:r0 + CONV_ROWS, :] = (y * _sigmoid(y)).astype(BF16)
    y_b = _dot(cn_ref[...], wco_ref[...]) + bco_ref[...]
    y_a = _dot(o_ref[...], wo_ref[...])
    d = y_a.shape[-1]
    mix = sg_ref[:, :d] * y_a + sg_ref[:, d:] * y_b
    h_ref[...] = x_ref[...] + _dot(mix.astype(BF16), wout_ref[...])


def _merge(o, c, sg, x2, wo, cw, cb, lg, lb, wco, bco, wout, seq):
    m, d = x2.shape
    ch = c.shape[-1]
    row = lambda n: pl.BlockSpec((TM, n), lambda i: (i, 0))
    halo = pl.BlockSpec((CONV_HALO, ch), lambda i: (jnp.maximum(i * (TM // CONV_HALO) - 1, 0), 0))
    weights = [wo, cw, cb, lg, lb, wco, bco, wout]
    return pl.pallas_call(
        functools.partial(_merge_kernel, tiles_per_seq=seq // TM),
        grid=(m // TM,),
        in_specs=[row(o.shape[-1]), row(ch), halo, row(sg.shape[-1]), row(d)] + [_resident(w.shape) for w in weights],
        out_specs=row(d),
        out_shape=jax.ShapeDtypeStruct((m, d), F32),
        scratch_shapes=[pltpu.VMEM((CONV_HALO + TM, ch), F32), pltpu.VMEM((TM, ch), BF16)],
        compiler_params=pltpu.CompilerParams(dimension_semantics=("parallel",), vmem_limit_bytes=VMEM_LIMIT),
        name="merge",
    )(o, c, c, sg, x2, *weights)


def _ffn_kernel(h_ref, halo_ref, p_ref, gf_ref, wup_ref, fw_ref, fb_ref, wdn_ref, gp_ref, wpg_ref, wple_ref,
                gfin_ref, out_ref, u_ref, vg_ref, vv_ref, *, tiles_per_seq, d_ff, final_norm):
    first = (pl.program_id(0) % tiles_per_seq) == 0
    h = h_ref[...]
    u_ref[0:FFN_HALO, :] = _rms(jnp.where(first, 0.0, halo_ref[...]), gf_ref[...]).astype(BF16)
    u_ref[FFN_HALO:, :] = _rms(h, gf_ref[...]).astype(BF16)
    u = u_ref[...]
    shift = FFN_HALO - (FFN_CONV_WIDTH - 1)

    def conv(v_ref, c0):
        acc = fb_ref[:, c0:c0 + FF_CHUNK]
        for k in range(FFN_CONV_WIDTH):
            acc = acc + fw_ref[k:k + 1, c0:c0 + FF_CHUNK] * v_ref[shift + k:shift + k + TM, :]
        return acc

    acc = jnp.zeros(h.shape, F32)
    for jc in range(d_ff // FF_CHUNK):
        c0 = jc * FF_CHUNK
        vg_ref[...] = _dot(u, wup_ref[:, c0:c0 + FF_CHUNK])
        vv_ref[...] = _dot(u, wup_ref[:, d_ff + c0:d_ff + c0 + FF_CHUNK])
        a = _gelu_tanh(conv(vg_ref, c0)) * conv(vv_ref, d_ff + c0)
        acc = acc + _dot(a.astype(BF16), wdn_ref[c0:c0 + FF_CHUNK, :])
    h = h + acc
    gate = _sigmoid(_dot(_rms(h, gp_ref[...]).astype(BF16), wpg_ref[...]))
    h = h + gate * _dot(p_ref[...].astype(BF16), wple_ref[...])
    out_ref[...] = _rms(h, gfin_ref[...]) if final_norm else h


def _ffn(h1, p2, gf, wup, fw, fb, wdn, gp, wpg, wple, gfin, seq, final_norm):
    m, d = h1.shape
    d_ff = wdn.shape[0]
    assert d_ff % FF_CHUNK == 0
    row = lambda n: pl.BlockSpec((TM, n), lambda i: (i, 0))
    halo = pl.BlockSpec((FFN_HALO, d), lambda i: (jnp.maximum(i * (TM // FFN_HALO) - 1, 0), 0))
    weights = [gf, wup, fw, fb, wdn, gp, wpg, wple, gfin]
    return pl.pallas_call(
        functools.partial(_ffn_kernel, tiles_per_seq=seq // TM, d_ff=d_ff, final_norm=final_norm),
        grid=(m // TM,),
        in_specs=[row(d), halo, row(p2.shape[-1])] + [_resident(w.shape) for w in weights],
        out_specs=row(d),
        out_shape=jax.ShapeDtypeStruct((m, d), F32),
        scratch_shapes=[pltpu.VMEM((FFN_HALO + TM, d), BF16), pltpu.VMEM((FFN_HALO + TM, FF_CHUNK), F32),
                        pltpu.VMEM((FFN_HALO + TM, FF_CHUNK), F32)],
        compiler_params=pltpu.CompilerParams(dimension_semantics=("parallel",), vmem_limit_bytes=VMEM_LIMIT),
        name="ffn_ple",
    )(h1, h1, p2, *weights)


def _sel_map_t(seq):
    n_cmp = (seq - CMP_LEN) // CMP_STRIDE + 1
    n_sel = seq // SEL_LEN
    c0 = np.arange(n_cmp) * CMP_STRIDE
    j0 = np.arange(n_sel) * SEL_LEN
    lo = np.maximum(c0[None, :], j0[:, None])
    hi = np.minimum(c0[None, :] + CMP_LEN, j0[:, None] + SEL_LEN)
    out = np.zeros((2 * N_SEL_PAD, N_CMP_PAD), np.float32)
    out[N_SEL_PAD:N_SEL_PAD + n_sel, :n_cmp] = np.maximum(hi - lo, 0) / CMP_LEN
    return jnp.asarray(out, BF16)


def _inproj_weight(w):
    d = w.shape[0]
    nq = NSA_HEADS * HEAD_DIM
    kv0 = nq
    gate0 = kv0 + 3 * 2 * NSA_GROUPS * HEAD_DIM
    conv0 = gate0 + 3 * NSA_HEADS
    merge0 = conv0 + _ZCN

    def kv(br, g):
        lo = kv0 + (br * NSA_GROUPS + g) * HEAD_DIM
        return w[:, lo:lo + HEAD_DIM]

    z = jnp.zeros((d, HEAD_DIM), w.dtype)
    cols = [w[:, :nq] * (HEAD_DIM ** -0.5), w[:, kv0:kv0 + _CVN]]
    cols += [kv(2, 0), z, kv(2, 1), z]
    cols += [kv(3, 0), kv(4, 0), kv(3, 1), kv(4, 1)]
    cols += [kv(5, 0), z, kv(5, 1), z]
    cols += [w[:, conv0:merge0], w[:, merge0:merge0 + _ZMN], w[:, gate0:conv0],
             jnp.zeros((d, _ZGN - 3 * NSA_HEADS), w.dtype)]
    out = jnp.concatenate(cols, axis=1).astype(BF16)
    assert out.shape[1] == _N_ALL
    return out


def kernel(x, p, g_mix, w_in, cmp_pos_k, cmp_pos_v, w_cmp_k1, w_cmp_k2, w_cmp_v1, w_cmp_v2, w_o_nsa, conv_w,
           conv_b, conv_ln_g, conv_ln_b, w_conv_out, b_conv_out, w_out, g_ffn, w_up, ffn_conv_w, ffn_conv_b,
           w_down, g_ple, w_ple_gate, w_ple, g_final):
    batch, seq, d = x.shape
    depth = w_in.shape[0]
    m = batch * seq
    assert seq % TM == 0 and seq % TK == 0
    selt = _sel_map_t(seq)
    row = lambda v: v.reshape(1, -1)
    h = x.reshape(m, d)
    for i in range(depth):
        w_all = _inproj_weight(w_in[i])
        pos = jnp.concatenate([cmp_pos_k[i]] * NSA_GROUPS + [cmp_pos_v[i]] * NSA_GROUPS, axis=1)
        w1 = jnp.stack([w_cmp_k1[i], w_cmp_v1[i]]).astype(BF16)
        zpad = jnp.zeros_like(w_cmp_k2[i])
        w2 = jnp.stack([jnp.concatenate([zpad, w_cmp_k2[i]], axis=1),
                        jnp.concatenate([w_cmp_v2[i], zpad], axis=1)]).astype(BF16)

        q, cv, ka, kb, kc, c, sg, gt = _inproj(h, row(g_mix[i]), w_all, seq)
        kcv = _compress(cv, pos, w1, w2, batch, seq)
        o = _attention(q, kcv, selt, ka, kb, kc, gt, batch, seq)
        h = _merge(o, c, sg, h, w_o_nsa[i].astype(BF16), conv_w[i], row(conv_b[i]), row(conv_ln_g[i]),
                   row(conv_ln_b[i]), w_conv_out[i].astype(BF16), row(b_conv_out[i]), w_out[i].astype(BF16), seq)
        h = _ffn(h, p[i].reshape(m, -1), row(g_ffn[i]), w_up[i].astype(BF16), ffn_conv_w[i], row(ffn_conv_b[i]),
                 w_down[i].astype(BF16), row(g_ple[i]), w_ple_gate[i].astype(BF16), w_ple[i].astype(BF16), row(g_final), seq,
                 final_norm=(i == depth - 1))
    return h.reshape(batch, seq, d)
```

```python
import functools

import numpy as np
import jax
import jax.numpy as jnp
from jax import lax
from jax.experimental import pallas as pl
from jax.experimental.pallas import tpu as pltpu

F32 = jnp.float32
BF16 = jnp.bfloat16

NSA_HEADS = 8
NSA_GROUPS = 2
NSA_HPG = NSA_HEADS // NSA_GROUPS
HEAD_DIM = 64
CMP_LEN = 32
CMP_STRIDE = 16
SEL_LEN = 64
SEL_SHIFT = 6
SEL_TOP = 16
WINDOW = 512
FORCE_SCORE = 1e4
CONV_WIDTH = 31
FFN_CONV_WIDTH = 3
EPS = 1e-6
NEG = -1e30
MASK_BIAS = -1e9

LANES = 128
BF16_ROWS = 16
VMEM_LIMIT = 56 * 1024 * 1024

TM = 512
TQ = 128
TK = 512
VT = 256
WIN_TILES = 3
CONV_HALO = 32
FFN_HALO = BF16_ROWS
CONV_ROWS = 64
FF_CHUNK = 256


def _sigmoid(x):
    return 0.5 * (jnp.tanh(0.5 * x) + 1.0)


def _gelu_tanh(x):
    return 0.5 * x * (1.0 + jnp.tanh(np.sqrt(2.0 / np.pi).astype(np.float32) * (x + 0.044715 * (x * x * x))))


def _rms(x, g):
    return x * lax.rsqrt(jnp.mean(x * x, axis=-1, keepdims=True) + EPS) * g


def _dot(a, b):
    return jnp.dot(a, b, preferred_element_type=F32)


def _dot_nt(a, b):
    return lax.dot_general(a, b, (((1,), (1,)), ((), ())), preferred_element_type=F32)


def _resident(shape):
    nd = len(shape)
    return pl.BlockSpec(shape, lambda *_: (0,) * nd, pipeline_mode=pl.Buffered(1))


_KA0, _KAN = 0, 256
_KW0, _KWN = 256, 128
_CV0, _CVN = 384, 256
_ZC0, _ZCN = 640, 1024
_ZM0, _ZMN = 1664, 2048
_N_TOK = 3712
_Q0, _QN = 0, 512
_VS0, _VSN = 512, 128
_VW0, _VWN = 640, 128
_ZG0, _ZGN = 768, 32
_N_FEAT = 800


def _inproj_kernel(x_ref, g_ref, wt_ref, wf_ref, qt_ref, ka_ref, kw_ref, cv_ref, c_ref, sg_ref, vst_ref, vwt_ref,
                   gtt_ref, *, tiles_per_seq):
    u = _rms(x_ref[...], g_ref[...]).astype(BF16)

    def tok(lo, n):
        return _dot(u, wt_ref[:, lo:lo + n])

    def feat(lo, n):
        return _dot_nt(wf_ref[lo:lo + n, :], u)

    ka = tok(_KA0, _KAN)
    s0 = (pl.program_id(0) % tiles_per_seq) * TM
    row = lax.broadcasted_iota(jnp.int32, ka.shape, 0) + s0
    col = lax.broadcasted_iota(jnp.int32, ka.shape, 1)
    onehot = ((col & HEAD_DIM) != 0) & ((col & (HEAD_DIM - 1)) == (row >> SEL_SHIFT))
    ka_ref[...] = jnp.where(onehot, 1.0, ka).astype(BF16)
    kw_ref[...] = tok(_KW0, _KWN).astype(BF16)
    cv_ref[...] = tok(_CV0, _CVN)
    zc = tok(_ZC0, _ZCN)
    half = _ZCN // 2
    c_ref[...] = zc[:, :half] * _sigmoid(zc[:, half:])
    sg_ref[...] = _sigmoid(tok(_ZM0, _ZMN))

    qt_ref[...] = feat(_Q0, _QN).astype(BF16)
    gtt_ref[...] = _sigmoid(feat(_ZG0, _ZGN))
    ones = jnp.ones((HEAD_DIM, VT), BF16)
    for src0, dst_ref in ((_VS0, vst_ref), (_VW0, vwt_ref)):
        v = feat(src0, NSA_GROUPS * HEAD_DIM).astype(BF16)
        for t in range(TM // VT):
            for g in range(NSA_GROUPS):
                r0 = g * 2 * HEAD_DIM
                dst_ref[t, r0:r0 + HEAD_DIM, :] = v[g * HEAD_DIM:(g + 1) * HEAD_DIM, t * VT:(t + 1) * VT]
                dst_ref[t, r0 + HEAD_DIM:r0 + 2 * HEAD_DIM, :] = ones


def _inproj(x2, g_mix, w_tok, w_feat, seq):
    m, d = x2.shape
    row = lambda n: pl.BlockSpec((TM, n), lambda i: (i, 0))
    colb = lambda n: pl.BlockSpec((n, TM), lambda i: (0, i))
    vt_spec = pl.BlockSpec((TM // VT, NSA_GROUPS * LANES, VT), lambda i: (i, 0, 0))
    vt_shape = jax.ShapeDtypeStruct((m // VT, NSA_GROUPS * LANES, VT), BF16)
    tok_outs = [(_KAN, BF16), (_KWN, BF16), (_CVN, F32), (_ZCN // 2, F32), (_ZMN, F32)]
    return pl.pallas_call(
        functools.partial(_inproj_kernel, tiles_per_seq=seq // TM),
        grid=(m // TM,),
        in_specs=[row(d), _resident((1, d)), _resident(w_tok.shape), _resident(w_feat.shape)],
        out_specs=[colb(_QN)] + [row(n) for n, _ in tok_outs] + [vt_spec, vt_spec, colb(_ZGN)],
        out_shape=([jax.ShapeDtypeStruct((_QN, m), BF16)]
                   + [jax.ShapeDtypeStruct((m, n), dt) for n, dt in tok_outs]
                   + [vt_shape, vt_shape, jax.ShapeDtypeStruct((_ZGN, m), F32)]),
        compiler_params=pltpu.CompilerParams(dimension_semantics=("parallel",), vmem_limit_bytes=VMEM_LIMIT),
        name="inproj",
    )(x2, g_mix, w_tok, w_feat)


N_CMP_PAD = 256
_CMP_HALF = CMP_LEN // 2


def _compress_kernel(ck_ref, cv_ref, pos_ref, w1_ref, w2_ref, w2t_ref, out_ref, outt_ref, *, n_cmp):
    hid = w1_ref.shape[-1]
    first = [jnp.zeros((N_CMP_PAD, hid), F32) for _ in range(4)]
    second = [jnp.zeros((N_CMP_PAD, hid), F32) for _ in range(4)]
    for l in range(_CMP_HALF):
        for kind, src_ref in enumerate((ck_ref, cv_ref)):
            xl = src_ref[pl.ds(l, N_CMP_PAD, stride=CMP_STRIDE), :]
            pcol = slice(kind * LANES, (kind + 1) * LANES)
            xa = (xl + pos_ref[l:l + 1, pcol]).astype(BF16)
            xb = (xl + pos_ref[_CMP_HALF + l:_CMP_HALF + l + 1, pcol]).astype(BF16)
            wa = w1_ref[kind, l * HEAD_DIM:(l + 1) * HEAD_DIM, :]
            wb = w1_ref[kind, (_CMP_HALF + l) * HEAD_DIM:(_CMP_HALF + l + 1) * HEAD_DIM, :]
            for g in range(NSA_GROUPS):
                kg = kind * NSA_GROUPS + g
                cols = slice(g * HEAD_DIM, (g + 1) * HEAD_DIM)
                first[kg] = first[kg] + _dot(xa[:, cols], wa)
                second[kg] = second[kg] + _dot(xb[:, cols], wb)
    rows = lax.broadcasted_iota(jnp.int32, (N_CMP_PAD, LANES), 0)
    lanes = lax.broadcasted_iota(jnp.int32, (LANES, N_CMP_PAD), 1)
    for g in range(NSA_GROUPS):
        acts = []
        for kind in range(2):
            kg = kind * NSA_GROUPS + g
            h1 = first[kg] + pltpu.roll(second[kg], N_CMP_PAD - 1, 0)
            acts.append(_gelu_tanh(h1).astype(BF16))
        kcv = _dot(acts[0], w2_ref[0]) + _dot(acts[1], w2_ref[1])
        kcvt = _dot_nt(w2t_ref[0], acts[0]) + _dot_nt(w2t_ref[1], acts[1])
        out_ref[0, g] = jnp.where(rows < n_cmp, kcv, 0.0).astype(BF16)
        outt_ref[0, g] = jnp.where(lanes < n_cmp, kcvt, 0.0).astype(BF16)


def _compress(cv, pos, w1, w2, w2t, batch, seq):
    n_cmp = (seq - CMP_LEN) // CMP_STRIDE + 1
    assert n_cmp < N_CMP_PAD and seq == N_CMP_PAD * CMP_STRIDE
    return pl.pallas_call(
        functools.partial(_compress_kernel, n_cmp=n_cmp),
        grid=(batch,),
        in_specs=[pl.BlockSpec((seq, LANES), lambda b: (b, 0)), pl.BlockSpec((seq, LANES), lambda b: (b, 1)),
                  _resident(pos.shape), _resident(w1.shape), _resident(w2.shape), _resident(w2t.shape)],
        out_specs=[pl.BlockSpec((1, NSA_GROUPS, N_CMP_PAD, LANES), lambda b: (b, 0, 0, 0)),
                   pl.BlockSpec((1, NSA_GROUPS, LANES, N_CMP_PAD), lambda b: (b, 0, 0, 0))],
        out_shape=[jax.ShapeDtypeStruct((batch, NSA_GROUPS, N_CMP_PAD, LANES), BF16),
                   jax.ShapeDtypeStruct((batch, NSA_GROUPS, LANES, N_CMP_PAD), BF16)],
        compiler_params=pltpu.CompilerParams(dimension_semantics=("parallel",), vmem_limit_bytes=VMEM_LIMIT),
        name="compress",
    )(cv, cv, pos, w1, w2, w2t)


N_SEL = 64


def _attn_kernel(qt_ref, kcv_ref, kcvt_ref, sel_ref, ka_ref, kw_ref, vst_ref, vwt_ref, gtt_ref, o_ref,
                 q0_ref, q1_ref, qs_ref, m_ref, acc_ref):
    g = pl.program_id(1)
    qb = pl.program_id(2)
    t0 = qb * TQ
    cols = NSA_HPG * TQ

    zeros = jnp.zeros((HEAD_DIM, TQ), BF16)
    for h in range(NSA_HPG):
        qh = qt_ref[h * HEAD_DIM:(h + 1) * HEAD_DIM, :]
        cs = slice(h * TQ, (h + 1) * TQ)
        q0_ref[0:HEAD_DIM, cs] = qh
        q0_ref[HEAD_DIM:, cs] = zeros
        q1_ref[0:HEAD_DIM, cs] = zeros
        q1_ref[HEAD_DIM:, cs] = qh
        qs_ref[0:HEAD_DIM, cs] = qh
    q0 = q0_ref[...]
    tq = t0 + (lax.broadcasted_iota(jnp.int32, (1, cols), 1) & (TQ - 1))

    sc = _dot(kcv_ref[0, 0], q0)
    cidx = lax.broadcasted_iota(jnp.int32, (N_CMP_PAD, 1), 0)
    mask_c = (cidx * CMP_STRIDE + (CMP_LEN - 1)) <= tq
    sc = jnp.where(mask_c, sc, NEG)
    pc = jnp.where(mask_c, jnp.exp(sc - jnp.max(sc, axis=0, keepdims=True)), 0.0)
    lc = jnp.sum(pc, axis=0, keepdims=True)
    pc = pc / jnp.where(lc > 0.0, lc, 1.0)
    o_c = _dot(kcvt_ref[0, 0], pc.astype(BF16))

    psum = pc[:, 0:TQ]
    for h in range(1, NSA_HPG):
        psum = psum + pc[:, h * TQ:(h + 1) * TQ]
    sel = sel_ref[...]
    p_hi = psum.astype(BF16)
    rem = psum - p_hi.astype(F32)
    p_mid = rem.astype(BF16)
    p_lo = (rem - p_mid.astype(F32)).astype(BF16)
    imp = (_dot(sel, p_hi) + _dot(sel, p_mid)) + _dot(sel, p_lo)
    j = lax.broadcasted_iota(jnp.int32, (N_SEL, TQ), 0)
    cur = (t0 + lax.broadcasted_iota(jnp.int32, (N_SEL, TQ), 1)) >> SEL_SHIFT
    forced = (j == 0) | (j == cur) | (j == cur - 1)
    imp = jnp.where(j <= cur, jnp.where(forced, FORCE_SCORE, imp), NEG)
    rank = jnp.zeros((N_SEL, TQ), jnp.int32)
    for i in range(N_SEL):
        vi = imp[i:i + 1, :]
        before = (vi > imp) | ((vi == imp) & (j > i))
        rank = rank + before.astype(jnp.int32)
    bias = jnp.where(rank < SEL_TOP, 0.0, MASK_BIAS).astype(BF16)
    for h in range(NSA_HPG):
        qs_ref[HEAD_DIM:, h * TQ:(h + 1) * TQ] = bias

    m_ref[...] = jnp.full(m_ref.shape, NEG, F32)
    acc_ref[...] = jnp.zeros(acc_ref.shape, F32)
    ksub = lax.broadcasted_iota(jnp.int32, (TK, 1), 0)

    def sel_tile(kt, causal):
        k0 = pl.multiple_of(kt * TK, TK)
        s = _dot(ka_ref[pl.ds(k0, TK), :], qs_ref[...])
        if causal:
            s = jnp.where(k0 + ksub <= tq, s, NEG)
        m_prev = m_ref[...]
        m_new = jnp.maximum(m_prev, jnp.max(s, axis=0, keepdims=True))
        p = jnp.exp(s - m_new).astype(BF16)
        pv = _dot(vst_ref[kt * (TK // VT)], p[0:VT])
        for c in range(1, TK // VT):
            pv = pv + _dot(vst_ref[kt * (TK // VT) + c], p[c * VT:(c + 1) * VT])
        acc_ref[...] = jnp.exp(m_prev - m_new) * acc_ref[...] + pv
        m_ref[...] = m_new

    last = qb // (TK // TQ)

    def body(kt, carry):
        sel_tile(kt, causal=False)
        return carry

    lax.fori_loop(0, last, body, 0)
    sel_tile(last, causal=True)
    acc = acc_ref[...]
    o_s = acc[0:HEAD_DIM] / acc[HEAD_DIM:HEAD_DIM + 1]

    wt0 = jnp.maximum(qb // (VT // TQ) - (WIN_TILES - 1), 0)
    kstart = pl.multiple_of(wt0 * VT, VT)
    span = WIN_TILES * VT
    qwin = jnp.where(g == 0, q0, q1_ref[...])
    sw = _dot(kw_ref[pl.ds(kstart, span), :], qwin)
    dist = tq - (kstart + lax.broadcasted_iota(jnp.int32, (span, 1), 0))
    sw = jnp.where((dist >= 0) & (dist < WINDOW), sw, NEG)
    pw = jnp.exp(sw - jnp.max(sw, axis=0, keepdims=True)).astype(BF16)
    ow = _dot(vwt_ref[wt0], pw[0:VT])
    for c in range(1, WIN_TILES):
        ow = ow + _dot(vwt_ref[wt0 + c], pw[c * VT:(c + 1) * VT])
    o_w = ow[0:HEAD_DIM] / ow[HEAD_DIM:HEAD_DIM + 1]

    heads = []
    for h in range(NSA_HPG):
        cs = slice(h * TQ, (h + 1) * TQ)

        def gate(br):
            return gtt_ref[pl.ds(br * NSA_HEADS + g * NSA_HPG + h, 1), :]

        heads.append(gate(0) * o_c[0:HEAD_DIM, cs] + gate(1) * o_s[:, cs] + gate(2) * o_w[:, cs])
    o_ref[...] = jnp.concatenate(heads, axis=0).T.astype(BF16)


def _attention(qt, kcv, kcvt, sel, ka, kw, vst, vwt, gtt, batch, seq):
    nq = seq // TQ
    cols = NSA_HPG * TQ
    assert seq // SEL_LEN == N_SEL and WIN_TILES * VT <= seq and TK % VT == 0 and VT % TQ == 0
    assert (WIN_TILES - 1) * VT >= WINDOW
    vt_spec = pl.BlockSpec((seq // VT, LANES, VT), lambda b, g, i: (b, g, 0))
    return pl.pallas_call(
        _attn_kernel,
        grid=(batch, NSA_GROUPS, nq),
        in_specs=[
            pl.BlockSpec((NSA_HPG * HEAD_DIM, TQ), lambda b, g, i: (g, b * nq + i)),
            pl.BlockSpec((1, 1, N_CMP_PAD, LANES), lambda b, g, i: (b, g, 0, 0)),
            pl.BlockSpec((1, 1, LANES, N_CMP_PAD), lambda b, g, i: (b, g, 0, 0)),
            pl.BlockSpec(sel.shape, lambda b, g, i: (0, 0)),
            pl.BlockSpec((seq, LANES), lambda b, g, i: (b, g)),
            pl.BlockSpec((seq, LANES), lambda b, g, i: (b, 0)),
            vt_spec, vt_spec,
            pl.BlockSpec((_ZGN, TQ), lambda b, g, i: (0, b * nq + i)),
        ],
        out_specs=pl.BlockSpec((TQ, NSA_HPG * HEAD_DIM), lambda b, g, i: (b * nq + i, g)),
        out_shape=jax.ShapeDtypeStruct((batch * seq, NSA_HEADS * HEAD_DIM), BF16),
        scratch_shapes=[
            pltpu.VMEM((LANES, cols), BF16),
            pltpu.VMEM((LANES, cols), BF16),
            pltpu.VMEM((LANES, cols), BF16),
            pltpu.VMEM((1, cols), F32),
            pltpu.VMEM((LANES, cols), F32),
        ],
        compiler_params=pltpu.CompilerParams(dimension_semantics=("parallel", "parallel", "arbitrary"),
                                             vmem_limit_bytes=VMEM_LIMIT),
        name="nsa_attention",
    )(qt, kcv, kcvt, sel, ka, kw, vst, vwt, gtt)


def _merge_kernel(o_ref, c_ref, halo_ref, sg_ref, x_ref, wo_ref, cw_ref, cb_ref, lg_ref, lb_ref, wco_ref,
                  bco_ref, wout_ref, h_ref, cext_ref, cn_ref, *, tiles_per_seq):
    first = (pl.program_id(0) % tiles_per_seq) == 0
    cext_ref[0:CONV_HALO, :] = jnp.where(first, 0.0, halo_ref[...])
    cext_ref[CONV_HALO:, :] = c_ref[...]
    shift = CONV_HALO - (CONV_WIDTH - 1)
    for r in range(TM // CONV_ROWS):
        r0 = r * CONV_ROWS
        acc = jnp.broadcast_to(cb_ref[...], (CONV_ROWS, cb_ref.shape[-1]))
        for k in range(CONV_WIDTH):
            acc = acc + cw_ref[k:k + 1, :] * cext_ref[r0 + shift + k:r0 + shift + k + CONV_ROWS, :]
        xc = acc - jnp.mean(acc, axis=-1, keepdims=True)
        y = xc * lax.rsqrt(jnp.mean(xc * xc, axis=-1, keepdims=True) + EPS) * lg_ref[...] + lb_ref[...]
        cn_ref[r0:r0 + CONV_ROWS, :] = (y * _sigmoid(y)).astype(BF16)
    y_b = _dot(cn_ref[...], wco_ref[...]) + bco_ref[...]
    y_a = _dot(o_ref[...], wo_ref[...])
    d = y_a.shape[-1]
    mix = sg_ref[:, :d] * y_a + sg_ref[:, d:] * y_b
    h_ref[...] = x_ref[...] + _dot(mix.astype(BF16), wout_ref[...])


def _merge(o, c, sg, x2, wo, cw, cb, lg, lb, wco, bco, wout, seq):
    m, d = x2.shape
    ch = c.shape[-1]
    row = lambda n: pl.BlockSpec((TM, n), lambda i: (i, 0))
    halo = pl.BlockSpec((CONV_HALO, ch), lambda i: (jnp.maximum(i * (TM // CONV_HALO) - 1, 0), 0))
    weights = [wo, cw, cb, lg, lb, wco, bco, wout]
    return pl.pallas_call(
        functools.partial(_merge_kernel, tiles_per_seq=seq // TM),
        grid=(m // TM,),
        in_specs=[row(o.shape[-1]), row(ch), halo, row(sg.shape[-1]), row(d)] + [_resident(w.shape) for w in weights],
        out_specs=row(d),
        out_shape=jax.ShapeDtypeStruct((m, d), F32),
        scratch_shapes=[pltpu.VMEM((CONV_HALO + TM, ch), F32), pltpu.VMEM((TM, ch), BF16)],
        compiler_params=pltpu.CompilerParams(dimension_semantics=("parallel",), vmem_limit_bytes=VMEM_LIMIT),
        name="merge",
    )(o, c, c, sg, x2, *weights)


def _ffn_kernel(h_ref, halo_ref, p_ref, gf_ref, wup_ref, fw_ref, fb_ref, wdn_ref, gp_ref, wpg_ref, wple_ref,
                gfin_ref, out_ref, u_ref, vg_ref, vv_ref, *, tiles_per_seq, d_ff, final_norm):
    first = (pl.program_id(0) % tiles_per_seq) == 0
    h = h_ref[...]
    u_ref[0:FFN_HALO, :] = _rms(jnp.where(first, 0.0, halo_ref[...]), gf_ref[...]).astype(BF16)
    u_ref[FFN_HALO:, :] = _rms(h, gf_ref[...]).astype(BF16)
    u = u_ref[...]
    shift = FFN_HALO - (FFN_CONV_WIDTH - 1)

    def conv(v_ref, c0):
        acc = fb_ref[:, c0:c0 + FF_CHUNK]
        for k in range(FFN_CONV_WIDTH):
            acc = acc + fw_ref[k:k + 1, c0:c0 + FF_CHUNK] * v_ref[shift + k:shift + k + TM, :]
        return acc

    acc = jnp.zeros(h.shape, F32)
    for jc in range(d_ff // FF_CHUNK):
        c0 = jc * FF_CHUNK
        vg_ref[...] = _dot(u, wup_ref[:, c0:c0 + FF_CHUNK])
        vv_ref[...] = _dot(u, wup_ref[:, d_ff + c0:d_ff + c0 + FF_CHUNK])
        a = _gelu_tanh(conv(vg_ref, c0)) * conv(vv_ref, d_ff + c0)
        acc = acc + _dot(a.astype(BF16), wdn_ref[c0:c0 + FF_CHUNK, :])
    h = h + acc
    gate = _sigmoid(_dot(_rms(h, gp_ref[...]).astype(BF16), wpg_ref[...]))
    h = h + gate * _dot(p_ref[...].astype(BF16), wple_ref[...])
    out_ref[...] = _rms(h, gfin_ref[...]) if final_norm else h


def _ffn(h1, p2, gf, wup, fw, fb, wdn, gp, wpg, wple, gfin, seq, final_norm):
    m, d = h1.shape
    d_ff = wdn.shape[0]
    assert d_ff % FF_CHUNK == 0
    row = lambda n: pl.BlockSpec((TM, n), lambda i: (i, 0))
    halo = pl.BlockSpec((FFN_HALO, d), lambda i: (jnp.maximum(i * (TM // FFN_HALO) - 1, 0), 0))
    weights = [gf, wup, fw, fb, wdn, gp, wpg, wple, gfin]
    return pl.pallas_call(
        functools.partial(_ffn_kernel, tiles_per_seq=seq // TM, d_ff=d_ff, final_norm=final_norm),
        grid=(m // TM,),
        in_specs=[row(d), halo, row(p2.shape[-1])] + [_resident(w.shape) for w in weights],
        out_specs=row(d),
        out_shape=jax.ShapeDtypeStruct((m, d), F32),
        scratch_shapes=[pltpu.VMEM((FFN_HALO + TM, d), BF16), pltpu.VMEM((FFN_HALO + TM, FF_CHUNK), F32),
                        pltpu.VMEM((FFN_HALO + TM, FF_CHUNK), F32)],
        compiler_params=pltpu.CompilerParams(dimension_semantics=("parallel",), vmem_limit_bytes=VMEM_LIMIT),
        name="ffn_ple",
    )(h1, h1, p2, *weights)


def _sel_map_t(seq):
    n_cmp = (seq - CMP_LEN) // CMP_STRIDE + 1
    n_sel = seq // SEL_LEN
    c0 = np.arange(n_cmp) * CMP_STRIDE
    j0 = np.arange(n_sel) * SEL_LEN
    lo = np.maximum(c0[None, :], j0[:, None])
    hi = np.minimum(c0[None, :] + CMP_LEN, j0[:, None] + SEL_LEN)
    out = np.zeros((N_SEL, N_CMP_PAD), np.float32)
    out[:n_sel, :n_cmp] = np.maximum(hi - lo, 0) / CMP_LEN
    return jnp.asarray(out, BF16)


def _inproj_weights(w):
    d = w.shape[0]
    nq = NSA_HEADS * HEAD_DIM
    kv0 = nq
    gate0 = kv0 + 3 * 2 * NSA_GROUPS * HEAD_DIM
    conv0 = gate0 + 3 * NSA_HEADS
    merge0 = conv0 + _ZCN

    def kv(br, g):
        lo = kv0 + (br * NSA_GROUPS + g) * HEAD_DIM
        return w[:, lo:lo + HEAD_DIM]

    z = jnp.zeros((d, HEAD_DIM), w.dtype)
    tok = [kv(2, 0), z, kv(2, 1), z, kv(4, 0), kv(4, 1), w[:, kv0:kv0 + _CVN], w[:, conv0:merge0],
           w[:, merge0:merge0 + _ZMN]]
    feat = [w[:, :nq] * (HEAD_DIM ** -0.5), kv(3, 0), kv(3, 1), kv(5, 0), kv(5, 1), w[:, gate0:conv0],
            jnp.zeros((d, _ZGN - 3 * NSA_HEADS), w.dtype)]
    w_tok = jnp.concatenate(tok, axis=1).astype(BF16)
    w_feat = jnp.concatenate(feat, axis=1).T.astype(BF16)
    assert w_tok.shape[1] == _N_TOK and w_feat.shape[0] == _N_FEAT
    return w_tok, w_feat


def kernel(x, p, g_mix, w_in, cmp_pos_k, cmp_pos_v, w_cmp_k1, w_cmp_k2, w_cmp_v1, w_cmp_v2, w_o_nsa, conv_w,
           conv_b, conv_ln_g, conv_ln_b, w_conv_out, b_conv_out, w_out, g_ffn, w_up, ffn_conv_w, ffn_conv_b,
           w_down, g_ple, w_ple_gate, w_ple, g_final):
    batch, seq, d = x.shape
    depth = w_in.shape[0]
    m = batch * seq
    assert seq % TM == 0 and seq % TK == 0
    sel = _sel_map_t(seq)
    row = lambda v: v.reshape(1, -1)
    h = x.reshape(m, d)
    for i in range(depth):
        w_tok, w_feat = _inproj_weights(w_in[i])
        pos = jnp.concatenate([cmp_pos_k[i]] * NSA_GROUPS + [cmp_pos_v[i]] * NSA_GROUPS, axis=1)
        w1 = jnp.stack([w_cmp_k1[i], w_cmp_v1[i]]).astype(BF16)
        zpad = jnp.zeros_like(w_cmp_k2[i])
        w2 = jnp.stack([jnp.concatenate([w_cmp_k2[i], zpad], axis=1),
                        jnp.concatenate([zpad, w_cmp_v2[i]], axis=1)]).astype(BF16)
        w2t = jnp.stack([jnp.concatenate([zpad, w_cmp_k2[i]], axis=1).T,
                         jnp.concatenate([w_cmp_v2[i], zpad], axis=1).T]).astype(BF16)

        qt, ka, kw, cv, c, sg, vst, vwt, gtt = _inproj(h, row(g_mix[i]), w_tok, w_feat, seq)
        kcv, kcvt = _compress(cv, pos, w1, w2, w2t, batch, seq)
        o = _attention(qt, kcv, kcvt, sel, ka, kw, vst, vwt, gtt, batch, seq)
        h = _merge(o, c, sg, h, w_o_nsa[i].astype(BF16), conv_w[i], row(conv_b[i]), row(conv_ln_g[i]),
                   row(conv_ln_b[i]), w_conv_out[i].astype(BF16), row(b_conv_out[i]), w_out[i].astype(BF16), seq)
        h = _ffn(h, p[i].reshape(m, -1), row(g_ffn[i]), w_up[i].astype(BF16), ffn_conv_w[i], row(ffn_conv_b[i]),
                 w_down[i].astype(BF16), row(g_ple[i]), w_ple_gate[i].astype(BF16), w_ple[i].astype(BF16),
                 row(g_final), seq, final_norm=(i == depth - 1))
    return h.reshape(batch, seq, d)
```

```python
import functools

import numpy as np
import jax
import jax.numpy as jnp
from jax import lax
from jax.experimental import pallas as pl
from jax.experimental.pallas import tpu as pltpu

F32 = jnp.float32
BF16 = jnp.bfloat16

NSA_HEADS = 8
NSA_GROUPS = 2
NSA_HPG = NSA_HEADS // NSA_GROUPS
HEAD_DIM = 64
CMP_LEN = 32
CMP_STRIDE = 16
SEL_LEN = 64
SEL_SHIFT = 6
SEL_TOP = 16
WINDOW = 512
FORCE_SCORE = 1e4
CONV_WIDTH = 31
FFN_CONV_WIDTH = 3
EPS = 1e-6
NEG = -1e30
MASK_BIAS = -1e9

LANES = 128
BF16_ROWS = 16
VMEM_LIMIT = 56 * 1024 * 1024

TM = 512
TQ = 128
TK = 512
VT = 256
WIN_TILES = 3
CONV_HALO = 32
FFN_HALO = BF16_ROWS
CONV_ROWS = 64
FF_CHUNK = 256


def _sigmoid(x):
    return 0.5 * (jnp.tanh(0.5 * x) + 1.0)


def _gelu_tanh(x):
    return 0.5 * x * (1.0 + jnp.tanh(np.sqrt(2.0 / np.pi).astype(np.float32) * (x + 0.044715 * (x * x * x))))


def _rms(x, g):
    return x * lax.rsqrt(jnp.mean(x * x, axis=-1, keepdims=True) + EPS) * g


def _dot(a, b):
    return jnp.dot(a, b, preferred_element_type=F32)


def _dot_nt(a, b):
    return lax.dot_general(a, b, (((1,), (1,)), ((), ())), preferred_element_type=F32)


def _resident(shape):
    nd = len(shape)
    return pl.BlockSpec(shape, lambda *_: (0,) * nd, pipeline_mode=pl.Buffered(1))


_KA0, _KAN = 0, 256
_KW0, _KWN = 256, 128
_CV0, _CVN = 384, 256
_ZC0, _ZCN = 640, 1024
_ZM0, _ZMN = 1664, 2048
_N_TOK = 3712
_Q0, _QN = 0, 512
_VS0, _VSN = 512, 128
_VW0, _VWN = 640, 128
_ZG0, _ZGN = 768, 32
_N_FEAT = 800


def _inproj_kernel(x_ref, g_ref, wt_ref, wf_ref, qt_ref, ka_ref, kw_ref, cv_ref, c_ref, sg_ref, vst_ref, vwt_ref,
                   gtt_ref, *, tiles_per_seq):
    u = _rms(x_ref[...], g_ref[...]).astype(BF16)

    def tok(lo, n):
        return _dot(u, wt_ref[:, lo:lo + n])

    def feat(lo, n):
        return _dot_nt(wf_ref[lo:lo + n, :], u)

    ka = tok(_KA0, _KAN)
    s0 = (pl.program_id(0) % tiles_per_seq) * TM
    row = lax.broadcasted_iota(jnp.int32, ka.shape, 0) + s0
    col = lax.broadcasted_iota(jnp.int32, ka.shape, 1)
    onehot = ((col & HEAD_DIM) != 0) & ((col & (HEAD_DIM - 1)) == (row >> SEL_SHIFT))
    ka_ref[...] = jnp.where(onehot, 1.0, ka).astype(BF16)
    kw_ref[...] = tok(_KW0, _KWN).astype(BF16)
    cv_ref[...] = tok(_CV0, _CVN)
    zc = tok(_ZC0, _ZCN)
    half = _ZCN // 2
    c_ref[...] = zc[:, :half] * _sigmoid(zc[:, half:])
    sg_ref[...] = _sigmoid(tok(_ZM0, _ZMN))

    qt_ref[...] = feat(_Q0, _QN).astype(BF16)
    gtt_ref[...] = _sigmoid(feat(_ZG0, _ZGN))
    ones = jnp.ones((HEAD_DIM, VT), BF16)
    for src0, dst_ref in ((_VS0, vst_ref), (_VW0, vwt_ref)):
        v = feat(src0, NSA_GROUPS * HEAD_DIM).astype(BF16)
        for t in range(TM // VT):
            for g in range(NSA_GROUPS):
                r0 = g * 2 * HEAD_DIM
                dst_ref[t, r0:r0 + HEAD_DIM, :] = v[g * HEAD_DIM:(g + 1) * HEAD_DIM, t * VT:(t + 1) * VT]
                dst_ref[t, r0 + HEAD_DIM:r0 + 2 * HEAD_DIM, :] = ones


def _inproj(x2, g_mix, w_tok, w_feat, seq):
    m, d = x2.shape
    row = lambda n: pl.BlockSpec((TM, n), lambda i: (i, 0))
    colb = lambda n: pl.BlockSpec((n, TM), lambda i: (0, i))
    vt_spec = pl.BlockSpec((TM // VT, NSA_GROUPS * LANES, VT), lambda i: (i, 0, 0))
    vt_shape = jax.ShapeDtypeStruct((m // VT, NSA_GROUPS * LANES, VT), BF16)
    tok_outs = [(_KAN, BF16), (_KWN, BF16), (_CVN, F32), (_ZCN // 2, F32), (_ZMN, F32)]
    return pl.pallas_call(
        functools.partial(_inproj_kernel, tiles_per_seq=seq // TM),
        grid=(m // TM,),
        in_specs=[row(d), _resident((1, d)), _resident(w_tok.shape), _resident(w_feat.shape)],
        out_specs=[colb(_QN)] + [row(n) for n, _ in tok_outs] + [vt_spec, vt_spec, colb(_ZGN)],
        out_shape=([jax.ShapeDtypeStruct((_QN, m), BF16)]
                   + [jax.ShapeDtypeStruct((m, n), dt) for n, dt in tok_outs]
                   + [vt_shape, vt_shape, jax.ShapeDtypeStruct((_ZGN, m), F32)]),
        compiler_params=pltpu.CompilerParams(dimension_semantics=("parallel",), vmem_limit_bytes=VMEM_LIMIT),
        name="inproj",
    )(x2, g_mix, w_tok, w_feat)


N_CMP_PAD = 256
_CMP_HALF = CMP_LEN // 2


def _compress_kernel(ck_ref, cv_ref, pos_ref, w1_ref, w2_ref, w2t_ref, out_ref, outt_ref, *, n_cmp):
    hid = w1_ref.shape[-1]
    first = [jnp.zeros((N_CMP_PAD, hid), F32) for _ in range(4)]
    second = [jnp.zeros((N_CMP_PAD, hid), F32) for _ in range(4)]
    for l in range(_CMP_HALF):
        for kind, src_ref in enumerate((ck_ref, cv_ref)):
            xl = src_ref[pl.ds(l, N_CMP_PAD, stride=CMP_STRIDE), :]
            pcol = slice(kind * LANES, (kind + 1) * LANES)
            xa = (xl + pos_ref[l:l + 1, pcol]).astype(BF16)
            xb = (xl + pos_ref[_CMP_HALF + l:_CMP_HALF + l + 1, pcol]).astype(BF16)
            wa = w1_ref[kind, l * HEAD_DIM:(l + 1) * HEAD_DIM, :]
            wb = w1_ref[kind, (_CMP_HALF + l) * HEAD_DIM:(_CMP_HALF + l + 1) * HEAD_DIM, :]
            for g in range(NSA_GROUPS):
                kg = kind * NSA_GROUPS + g
                cols = slice(g * HEAD_DIM, (g + 1) * HEAD_DIM)
                first[kg] = first[kg] + _dot(xa[:, cols], wa)
                second[kg] = second[kg] + _dot(xb[:, cols], wb)
    rows = lax.broadcasted_iota(jnp.int32, (N_CMP_PAD, LANES), 0)
    lanes = lax.broadcasted_iota(jnp.int32, (LANES, N_CMP_PAD), 1)
    for g in range(NSA_GROUPS):
        acts = []
        for kind in range(2):
            kg = kind * NSA_GROUPS + g
            h1 = first[kg] + pltpu.roll(second[kg], N_CMP_PAD - 1, 0)
            acts.append(_gelu_tanh(h1).astype(BF16))
        kcv = _dot(acts[0], w2_ref[0]) + _dot(acts[1], w2_ref[1])
        kcvt = _dot_nt(w2t_ref[0], acts[0]) + _dot_nt(w2t_ref[1], acts[1])
        out_ref[0, g] = jnp.where(rows < n_cmp, kcv, 0.0).astype(BF16)
        outt_ref[0, g] = jnp.where(lanes < n_cmp, kcvt, 0.0).astype(BF16)


def _compress(cv, pos, w1, w2, w2t, batch, seq):
    n_cmp = (seq - CMP_LEN) // CMP_STRIDE + 1
    assert n_cmp < N_CMP_PAD and seq == N_CMP_PAD * CMP_STRIDE
    return pl.pallas_call(
        functools.partial(_compress_kernel, n_cmp=n_cmp),
        grid=(batch,),
        in_specs=[pl.BlockSpec((seq, LANES), lambda b: (b, 0)), pl.BlockSpec((seq, LANES), lambda b: (b, 1)),
                  _resident(pos.shape), _resident(w1.shape), _resident(w2.shape), _resident(w2t.shape)],
        out_specs=[pl.BlockSpec((1, NSA_GROUPS, N_CMP_PAD, LANES), lambda b: (b, 0, 0, 0)),
                   pl.BlockSpec((1, NSA_GROUPS, LANES, N_CMP_PAD), lambda b: (b, 0, 0, 0))],
        out_shape=[jax.ShapeDtypeStruct((batch, NSA_GROUPS, N_CMP_PAD, LANES), BF16),
                   jax.ShapeDtypeStruct((batch, NSA_GROUPS, LANES, N_CMP_PAD), BF16)],
        compiler_params=pltpu.CompilerParams(dimension_semantics=("parallel",), vmem_limit_bytes=VMEM_LIMIT),
        name="compress",
    )(cv, cv, pos, w1, w2, w2t)


N_SEL = 64


def _attn_kernel(qt_ref, kcv_ref, kcvt_ref, sel_ref, ka_ref, kw_ref, vst_ref, vwt_ref, gtt_ref, o_ref,
                 q0_ref, q1_ref, qs_ref, m_ref, acc_ref, s_ref, smax_ref, p_ref, alpha_ref):
    g = pl.program_id(1)
    qb = pl.program_id(2)
    t0 = qb * TQ
    cols = NSA_HPG * TQ

    zeros = jnp.zeros((HEAD_DIM, TQ), BF16)
    for h in range(NSA_HPG):
        qh = qt_ref[h * HEAD_DIM:(h + 1) * HEAD_DIM, :]
        cs = slice(h * TQ, (h + 1) * TQ)
        q0_ref[0:HEAD_DIM, cs] = qh
        q0_ref[HEAD_DIM:, cs] = zeros
        q1_ref[0:HEAD_DIM, cs] = zeros
        q1_ref[HEAD_DIM:, cs] = qh
        qs_ref[0:HEAD_DIM, cs] = qh
    q0 = q0_ref[...]
    tq = t0 + (lax.broadcasted_iota(jnp.int32, (1, cols), 1) & (TQ - 1))

    sc = _dot(kcv_ref[0, 0], q0)
    cidx = lax.broadcasted_iota(jnp.int32, (N_CMP_PAD, 1), 0)
    mask_c = (cidx * CMP_STRIDE + (CMP_LEN - 1)) <= tq
    sc = jnp.where(mask_c, sc, NEG)
    pc = jnp.where(mask_c, jnp.exp(sc - jnp.max(sc, axis=0, keepdims=True)), 0.0)
    lc = jnp.sum(pc, axis=0, keepdims=True)
    pc = pc / jnp.where(lc > 0.0, lc, 1.0)
    o_c = _dot(kcvt_ref[0, 0], pc.astype(BF16))

    psum = pc[:, 0:TQ]
    for h in range(1, NSA_HPG):
        psum = psum + pc[:, h * TQ:(h + 1) * TQ]
    sel = sel_ref[...]
    p_hi = psum.astype(BF16)
    rem = psum - p_hi.astype(F32)
    p_mid = rem.astype(BF16)
    p_lo = (rem - p_mid.astype(F32)).astype(BF16)
    imp = (_dot(sel, p_hi) + _dot(sel, p_mid)) + _dot(sel, p_lo)
    j = lax.broadcasted_iota(jnp.int32, (N_SEL, TQ), 0)
    cur = (t0 + lax.broadcasted_iota(jnp.int32, (N_SEL, TQ), 1)) >> SEL_SHIFT
    forced = (j == 0) | (j == cur) | (j == cur - 1)
    imp = jnp.where(j <= cur, jnp.where(forced, FORCE_SCORE, imp), NEG)
    rank = jnp.zeros((N_SEL, TQ), jnp.int32)
    for i in range(N_SEL):
        vi = imp[i:i + 1, :]
        before = (vi > imp) | ((vi == imp) & (j > i))
        rank = rank + before.astype(jnp.int32)
    bias = jnp.where(rank < SEL_TOP, 0.0, MASK_BIAS).astype(BF16)
    for h in range(NSA_HPG):
        qs_ref[HEAD_DIM:, h * TQ:(h + 1) * TQ] = bias

    m_ref[...] = jnp.full(m_ref.shape, NEG, F32)
    acc_ref[...] = jnp.zeros(acc_ref.shape, F32)
    ksub = lax.broadcasted_iota(jnp.int32, (TK, 1), 0)
    even, odd = 0, 1

    def logits(kt, slot):
        k0 = pl.multiple_of(kt * TK, TK)
        s = _dot(ka_ref[pl.ds(k0, TK), :], qs_ref[...])
        s_ref[slot] = s
        smax_ref[slot] = jnp.max(s, axis=0, keepdims=True)

    def softmax(kt, slot, causal):
        s = s_ref[slot]
        if causal:
            s = jnp.where(kt * TK + ksub <= tq, s, NEG)
            smax = jnp.max(s, axis=0, keepdims=True)
        else:
            smax = smax_ref[slot]
        m_prev = m_ref[...]
        m_new = jnp.maximum(m_prev, smax)
        p_ref[slot] = jnp.exp(s - m_new).astype(BF16)
        alpha_ref[slot] = jnp.exp(m_prev - m_new)
        m_ref[...] = m_new

    def accumulate(kt, slot):
        p = p_ref[slot]
        v0 = jnp.maximum(kt, 0) * (TK // VT)
        pv = _dot(vst_ref[v0], p[0:VT])
        for c in range(1, TK // VT):
            pv = pv + _dot(vst_ref[v0 + c], p[c * VT:(c + 1) * VT])
        acc_ref[...] = alpha_ref[slot] * acc_ref[...] + pv

    p_ref[odd] = jnp.zeros(p_ref.shape[1:], BF16)
    alpha_ref[odd] = jnp.ones(alpha_ref.shape[1:], F32)
    last = qb // (TK // TQ)
    pairs = last // 2
    logits(0, even)

    def body(jp, carry):
        a = 2 * jp
        accumulate(a - 1, odd)
        softmax(a, even, False)
        logits(a + 1, odd)
        accumulate(a, even)
        softmax(a + 1, odd, False)
        logits(a + 2, even)
        return carry

    lax.fori_loop(0, pairs, body, 0)
    a = 2 * pairs

    @pl.when(last == a)
    def _():
        accumulate(a - 1, odd)
        softmax(a, even, True)
        accumulate(a, even)

    @pl.when(last != a)
    def _():
        accumulate(a - 1, odd)
        softmax(a, even, False)
        logits(a + 1, odd)
        accumulate(a, even)
        softmax(a + 1, odd, True)
        accumulate(a + 1, odd)

    acc = acc_ref[...]
    o_s = acc[0:HEAD_DIM] / acc[HEAD_DIM:HEAD_DIM + 1]

    wt0 = jnp.maximum(qb // (VT // TQ) - (WIN_TILES - 1), 0)
    kstart = pl.multiple_of(wt0 * VT, VT)
    span = WIN_TILES * VT
    qwin = jnp.where(g == 0, q0, q1_ref[...])
    sw = _dot(kw_ref[pl.ds(kstart, span), :], qwin)
    dist = tq - (kstart + lax.broadcasted_iota(jnp.int32, (span, 1), 0))
    sw = jnp.where((dist >= 0) & (dist < WINDOW), sw, NEG)
    pw = jnp.exp(sw - jnp.max(sw, axis=0, keepdims=True)).astype(BF16)
    ow = _dot(vwt_ref[wt0], pw[0:VT])
    for c in range(1, WIN_TILES):
        ow = ow + _dot(vwt_ref[wt0 + c], pw[c * VT:(c + 1) * VT])
    o_w = ow[0:HEAD_DIM] / ow[HEAD_DIM:HEAD_DIM + 1]

    heads = []
    for h in range(NSA_HPG):
        cs = slice(h * TQ, (h + 1) * TQ)

        def gate(br):
            return gtt_ref[pl.ds(br * NSA_HEADS + g * NSA_HPG + h, 1), :]

        heads.append(gate(0) * o_c[0:HEAD_DIM, cs] + gate(1) * o_s[:, cs] + gate(2) * o_w[:, cs])
    o_ref[...] = jnp.concatenate(heads, axis=0).T.astype(BF16)


def _attention(qt, kcv, kcvt, sel, ka, kw, vst, vwt, gtt, batch, seq):
    nq = seq // TQ
    cols = NSA_HPG * TQ
    assert seq // SEL_LEN == N_SEL and WIN_TILES * VT <= seq and TK % VT == 0 and VT % TQ == 0
    assert (WIN_TILES - 1) * VT >= WINDOW
    vt_spec = pl.BlockSpec((seq // VT, LANES, VT), lambda b, g, i: (b, g, 0))
    return pl.pallas_call(
        _attn_kernel,
        grid=(batch, NSA_GROUPS, nq),
        in_specs=[
            pl.BlockSpec((NSA_HPG * HEAD_DIM, TQ), lambda b, g, i: (g, b * nq + i)),
            pl.BlockSpec((1, 1, N_CMP_PAD, LANES), lambda b, g, i: (b, g, 0, 0)),
            pl.BlockSpec((1, 1, LANES, N_CMP_PAD), lambda b, g, i: (b, g, 0, 0)),
            pl.BlockSpec(sel.shape, lambda b, g, i: (0, 0)),
            pl.BlockSpec((seq, LANES), lambda b, g, i: (b, g)),
            pl.BlockSpec((seq, LANES), lambda b, g, i: (b, 0)),
            vt_spec, vt_spec,
            pl.BlockSpec((_ZGN, TQ), lambda b, g, i: (0, b * nq + i)),
        ],
        out_specs=pl.BlockSpec((TQ, NSA_HPG * HEAD_DIM), lambda b, g, i: (b * nq + i, g)),
        out_shape=jax.ShapeDtypeStruct((batch * seq, NSA_HEADS * HEAD_DIM), BF16),
        scratch_shapes=[
            pltpu.VMEM((LANES, cols), BF16),
            pltpu.VMEM((LANES, cols), BF16),
            pltpu.VMEM((LANES, cols), BF16),
            pltpu.VMEM((1, cols), F32),
            pltpu.VMEM((LANES, cols), F32),
            pltpu.VMEM((2, TK, cols), F32),
            pltpu.VMEM((2, 1, cols), F32),
            pltpu.VMEM((2, TK, cols), BF16),
            pltpu.VMEM((2, 1, cols), F32),
        ],
        compiler_params=pltpu.CompilerParams(dimension_semantics=("parallel", "parallel", "arbitrary"),
                                             vmem_limit_bytes=VMEM_LIMIT),
        name="nsa_attention",
    )(qt, kcv, kcvt, sel, ka, kw, vst, vwt, gtt)


def _merge_kernel(o_ref, c_ref, halo_ref, sg_ref, x_ref, wo_ref, cw_ref, cb_ref, lg_ref, lb_ref, wco_ref,
                  bco_ref, wout_ref, h_ref, cext_ref, cn_ref, *, tiles_per_seq):
    first = (pl.program_id(0) % tiles_per_seq) == 0
    cext_ref[0:CONV_HALO, :] = jnp.where(first, 0.0, halo_ref[...])
    cext_ref[CONV_HALO:, :] = c_ref[...]
    shift = CONV_HALO - (CONV_WIDTH - 1)
    for r in range(TM // CONV_ROWS):
        r0 = r * CONV_ROWS
        acc = jnp.broadcast_to(cb_ref[...], (CONV_ROWS, cb_ref.shape[-1]))
        for k in range(CONV_WIDTH):
            acc = acc + cw_ref[k:k + 1, :] * cext_ref[r0 + shift + k:r0 + shift + k + CONV_ROWS, :]
        xc = acc - jnp.mean(acc, axis=-1, keepdims=True)
        y = xc * lax.rsqrt(jnp.mean(xc * xc, axis=-1, keepdims=True) + EPS) * lg_ref[...] + lb_ref[...]
        cn_ref[r0:r0 + CONV_ROWS, :] = (y * _sigmoid(y)).astype(BF16)
    y_b = _dot(cn_ref[...], wco_ref[...]) + bco_ref[...]
    y_a = _dot(o_ref[...], wo_ref[...])
    d = y_a.shape[-1]
    mix = sg_ref[:, :d] * y_a + sg_ref[:, d:] * y_b
    h_ref[...] = x_ref[...] + _dot(mix.astype(BF16), wout_ref[...])


def _merge(o, c, sg, x2, wo, cw, cb, lg, lb, wco, bco, wout, seq):
    m, d = x2.shape
    ch = c.shape[-1]
    row = lambda n: pl.BlockSpec((TM, n), lambda i: (i, 0))
    halo = pl.BlockSpec((CONV_HALO, ch), lambda i: (jnp.maximum(i * (TM // CONV_HALO) - 1, 0), 0))
    weights = [wo, cw, cb, lg, lb, wco, bco, wout]
    return pl.pallas_call(
        functools.partial(_merge_kernel, tiles_per_seq=seq // TM),
        grid=(m // TM,),
        in_specs=[row(o.shape[-1]), row(ch), halo, row(sg.shape[-1]), row(d)] + [_resident(w.shape) for w in weights],
        out_specs=row(d),
        out_shape=jax.ShapeDtypeStruct((m, d), F32),
        scratch_shapes=[pltpu.VMEM((CONV_HALO + TM, ch), F32), pltpu.VMEM((TM, ch), BF16)],
        compiler_params=pltpu.CompilerParams(dimension_semantics=("parallel",), vmem_limit_bytes=VMEM_LIMIT),
        name="merge",
    )(o, c, c, sg, x2, *weights)


def _ffn_kernel(h_ref, halo_ref, p_ref, gf_ref, wup_ref, fw_ref, fb_ref, wdn_ref, gp_ref, wpg_ref, wple_ref,
                gfin_ref, out_ref, u_ref, vg_ref, vv_ref, *, tiles_per_seq, d_ff, final_norm):
    first = (pl.program_id(0) % tiles_per_seq) == 0
    h = h_ref[...]
    u_ref[0:FFN_HALO, :] = _rms(jnp.where(first, 0.0, halo_ref[...]), gf_ref[...]).astype(BF16)
    u_ref[FFN_HALO:, :] = _rms(h, gf_ref[...]).astype(BF16)
    u = u_ref[...]
    shift = FFN_HALO - (FFN_CONV_WIDTH - 1)

    def conv(v_ref, c0):
        acc = fb_ref[:, c0:c0 + FF_CHUNK]
        for k in range(FFN_CONV_WIDTH):
            acc = acc + fw_ref[k:k + 1, c0:c0 + FF_CHUNK] * v_ref[shift + k:shift + k + TM, :]
        return acc

    acc = jnp.zeros(h.shape, F32)
    for jc in range(d_ff // FF_CHUNK):
        c0 = jc * FF_CHUNK
        vg_ref[...] = _dot(u, wup_ref[:, c0:c0 + FF_CHUNK])
        vv_ref[...] = _dot(u, wup_ref[:, d_ff + c0:d_ff + c0 + FF_CHUNK])
        a = _gelu_tanh(conv(vg_ref, c0)) * conv(vv_ref, d_ff + c0)
        acc = acc + _dot(a.astype(BF16), wdn_ref[c0:c0 + FF_CHUNK, :])
    h = h + acc
    gate = _sigmoid(_dot(_rms(h, gp_ref[...]).astype(BF16), wpg_ref[...]))
    h = h + gate * _dot(p_ref[...].astype(BF16), wple_ref[...])
    out_ref[...] = _rms(h, gfin_ref[...]) if final_norm else h


def _ffn(h1, p2, gf, wup, fw, fb, wdn, gp, wpg, wple, gfin, seq, final_norm):
    m, d = h1.shape
    d_ff = wdn.shape[0]
    assert d_ff % FF_CHUNK == 0
    row = lambda n: pl.BlockSpec((TM, n), lambda i: (i, 0))
    halo = pl.BlockSpec((FFN_HALO, d), lambda i: (jnp.maximum(i * (TM // FFN_HALO) - 1, 0), 0))
    weights = [gf, wup, fw, fb, wdn, gp, wpg, wple, gfin]
    return pl.pallas_call(
        functools.partial(_ffn_kernel, tiles_per_seq=seq // TM, d_ff=d_ff, final_norm=final_norm),
        grid=(m // TM,),
        in_specs=[row(d), halo, row(p2.shape[-1])] + [_resident(w.shape) for w in weights],
        out_specs=row(d),
        out_shape=jax.ShapeDtypeStruct((m, d), F32),
        scratch_shapes=[pltpu.VMEM((FFN_HALO + TM, d), BF16), pltpu.VMEM((FFN_HALO + TM, FF_CHUNK), F32),
                        pltpu.VMEM((FFN_HALO + TM, FF_CHUNK), F32)],
        compiler_params=pltpu.CompilerParams(dimension_semantics=("parallel",), vmem_limit_bytes=VMEM_LIMIT),
        name="ffn_ple",
    )(h1, h1, p2, *weights)


def _sel_map_t(seq):
    n_cmp = (seq - CMP_LEN) // CMP_STRIDE + 1
    n_sel = seq // SEL_LEN
    c0 = np.arange(n_cmp) * CMP_STRIDE
    j0 = np.arange(n_sel) * SEL_LEN
    lo = np.maximum(c0[None, :], j0[:, None])
    hi = np.minimum(c0[None, :] + CMP_LEN, j0[:, None] + SEL_LEN)
    out = np.zeros((N_SEL, N_CMP_PAD), np.float32)
    out[:n_sel, :n_cmp] = np.maximum(hi - lo, 0) / CMP_LEN
    return jnp.asarray(out, BF16)


def _inproj_weights(w):
    d = w.shape[0]
    nq = NSA_HEADS * HEAD_DIM
    kv0 = nq
    gate0 = kv0 + 3 * 2 * NSA_GROUPS * HEAD_DIM
    conv0 = gate0 + 3 * NSA_HEADS
    merge0 = conv0 + _ZCN

    def kv(br, g):
        lo = kv0 + (br * NSA_GROUPS + g) * HEAD_DIM
        return w[:, lo:lo + HEAD_DIM]

    z = jnp.zeros((d, HEAD_DIM), w.dtype)
    tok = [kv(2, 0), z, kv(2, 1), z, kv(4, 0), kv(4, 1), w[:, kv0:kv0 + _CVN], w[:, conv0:merge0],
           w[:, merge0:merge0 + _ZMN]]
    feat = [w[:, :nq] * (HEAD_DIM ** -0.5), kv(3, 0), kv(3, 1), kv(5, 0), kv(5, 1), w[:, gate0:conv0],
            jnp.zeros((d, _ZGN - 3 * NSA_HEADS), w.dtype)]
    w_tok = jnp.concatenate(tok, axis=1).astype(BF16)
    w_feat = jnp.concatenate(feat, axis=1).T.astype(BF16)
    assert w_tok.shape[1] == _N_TOK and w_feat.shape[0] == _N_FEAT
    return w_tok, w_feat


def kernel(x, p, g_mix, w_in, cmp_pos_k, cmp_pos_v, w_cmp_k1, w_cmp_k2, w_cmp_v1, w_cmp_v2, w_o_nsa, conv_w,
           conv_b, conv_ln_g, conv_ln_b, w_conv_out, b_conv_out, w_out, g_ffn, w_up, ffn_conv_w, ffn_conv_b,
           w_down, g_ple, w_ple_gate, w_ple, g_final):
    batch, seq, d = x.shape
    depth = w_in.shape[0]
    m = batch * seq
    assert seq % TM == 0 and seq % TK == 0
    sel = _sel_map_t(seq)
    row = lambda v: v.reshape(1, -1)
    h = x.reshape(m, d)
    for i in range(depth):
        w_tok, w_feat = _inproj_weights(w_in[i])
        pos = jnp.concatenate([cmp_pos_k[i]] * NSA_GROUPS + [cmp_pos_v[i]] * NSA_GROUPS, axis=1)
        w1 = jnp.stack([w_cmp_k1[i], w_cmp_v1[i]]).astype(BF16)
        zpad = jnp.zeros_like(w_cmp_k2[i])
        w2 = jnp.stack([jnp.concatenate([w_cmp_k2[i], zpad], axis=1),
                        jnp.concatenate([zpad, w_cmp_v2[i]], axis=1)]).astype(BF16)
        w2t = jnp.stack([jnp.concatenate([zpad, w_cmp_k2[i]], axis=1).T,
                         jnp.concatenate([w_cmp_v2[i], zpad], axis=1).T]).astype(BF16)

        qt, ka, kw, cv, c, sg, vst, vwt, gtt = _inproj(h, row(g_mix[i]), w_tok, w_feat, seq)
        kcv, kcvt = _compress(cv, pos, w1, w2, w2t, batch, seq)
        o = _attention(qt, kcv, kcvt, sel, ka, kw, vst, vwt, gtt, batch, seq)
        h = _merge(o, c, sg, h, w_o_nsa[i].astype(BF16), conv_w[i], row(conv_b[i]), row(conv_ln_g[i]),
                   row(conv_ln_b[i]), w_conv_out[i].astype(BF16), row(b_conv_out[i]), w_out[i].astype(BF16), seq)
        h = _ffn(h, p[i].reshape(m, -1), row(g_ffn[i]), w_up[i].astype(BF16), ffn_conv_w[i], row(ffn_conv_b[i]),
                 w_down[i].astype(BF16), row(g_ple[i]), w_ple_gate[i].astype(BF16), w_ple[i].astype(BF16),
                 row(g_final), seq, final_norm=(i == depth - 1))
    return h.reshape(batch, seq, d)
```

```python
import functools

import numpy as np
import jax
import jax.numpy as jnp
from jax import lax
from jax.experimental import pallas as pl
from jax.experimental.pallas import tpu as pltpu

F32 = jnp.float32
BF16 = jnp.bfloat16

NSA_HEADS = 8
NSA_GROUPS = 2
NSA_HPG = NSA_HEADS // NSA_GROUPS
HEAD_DIM = 64
CMP_LEN = 32
CMP_STRIDE = 16
SEL_LEN = 64
SEL_SHIFT = 6
SEL_TOP = 16
WINDOW = 512
FORCE_SCORE = 1e4
CONV_WIDTH = 31
FFN_CONV_WIDTH = 3
EPS = 1e-6
NEG = -1e30
MASK_BIAS = -1e9

LANES = 128
SUBLANES = 8
BF16_ROWS = 16
VMEM_LIMIT = 56 * 1024 * 1024

TM = 512
TQ = 128
TK = 512
VT = 256
WIN_TILES = 3
CONV_HALO = 32
FFN_HALO = BF16_ROWS
CONV_ROWS = 64
FF_CHUNK = 256


def _sigmoid(x):
    return 0.5 * (jnp.tanh(0.5 * x) + 1.0)


def _gelu_tanh(x):
    return 0.5 * x * (1.0 + jnp.tanh(np.sqrt(2.0 / np.pi).astype(np.float32) * (x + 0.044715 * (x * x * x))))


def _rms(x, g):
    return x * lax.rsqrt(jnp.mean(x * x, axis=-1, keepdims=True) + EPS) * g


def _dot(a, b):
    return jnp.dot(a, b, preferred_element_type=F32)


def _dot_nt(a, b):
    return lax.dot_general(a, b, (((1,), (1,)), ((), ())), preferred_element_type=F32)


def _resident(shape):
    nd = len(shape)
    return pl.BlockSpec(shape, lambda *_: (0,) * nd, pipeline_mode=pl.Buffered(1))


_KA0, _KAN = 0, 256
_KW0, _KWN = 256, 128
_CV0, _CVN = 384, 256
_ZC0, _ZCN = 640, 1024
_ZM0, _ZMN = 1664, 2048
_N_TOK = 3712
_Q0, _QN = 0, 512
_VS0, _VSN = 512, 128
_VW0, _VWN = 640, 128
_ZG0, _ZGN = 768, 32
_N_FEAT = 800


def _inproj_kernel(x_ref, g_ref, wt_ref, wf_ref, qt_ref, ka_ref, kw_ref, cv_ref, c_ref, sg_ref, vst_ref, vwt_ref,
                   gtt_ref, *, tiles_per_seq):
    u = _rms(x_ref[...], g_ref[...]).astype(BF16)

    def tok(lo, n):
        return _dot(u, wt_ref[:, lo:lo + n])

    def feat(lo, n):
        return _dot_nt(wf_ref[lo:lo + n, :], u)

    ka = tok(_KA0, _KAN)
    s0 = (pl.program_id(0) % tiles_per_seq) * TM
    row = lax.broadcasted_iota(jnp.int32, ka.shape, 0) + s0
    col = lax.broadcasted_iota(jnp.int32, ka.shape, 1)
    onehot = ((col & HEAD_DIM) != 0) & ((col & (HEAD_DIM - 1)) == (row >> SEL_SHIFT))
    ka_ref[...] = jnp.where(onehot, 1.0, ka).astype(BF16)
    kw_ref[...] = tok(_KW0, _KWN).astype(BF16)
    cv_ref[...] = tok(_CV0, _CVN)
    zc = tok(_ZC0, _ZCN)
    half = _ZCN // 2
    c_ref[...] = zc[:, :half] * _sigmoid(zc[:, half:])
    sg_ref[...] = _sigmoid(tok(_ZM0, _ZMN))

    qt_ref[...] = feat(_Q0, _QN).astype(BF16)
    gtt_ref[...] = _sigmoid(feat(_ZG0, _ZGN))
    ones = jnp.ones((HEAD_DIM, VT), BF16)
    for src0, dst_ref in ((_VS0, vst_ref), (_VW0, vwt_ref)):
        v = feat(src0, NSA_GROUPS * HEAD_DIM).astype(BF16)
        for t in range(TM // VT):
            for g in range(NSA_GROUPS):
                r0 = g * 2 * HEAD_DIM
                dst_ref[t, r0:r0 + HEAD_DIM, :] = v[g * HEAD_DIM:(g + 1) * HEAD_DIM, t * VT:(t + 1) * VT]
                dst_ref[t, r0 + HEAD_DIM:r0 + 2 * HEAD_DIM, :] = ones


def _inproj(x2, g_mix, w_tok, w_feat, seq):
    m, d = x2.shape
    row = lambda n: pl.BlockSpec((TM, n), lambda i: (i, 0))
    colb = lambda n: pl.BlockSpec((n, TM), lambda i: (0, i))
    vt_spec = pl.BlockSpec((TM // VT, NSA_GROUPS * LANES, VT), lambda i: (i, 0, 0))
    vt_shape = jax.ShapeDtypeStruct((m // VT, NSA_GROUPS * LANES, VT), BF16)
    tok_outs = [(_KAN, BF16), (_KWN, BF16), (_CVN, F32), (_ZCN // 2, F32), (_ZMN, F32)]
    return pl.pallas_call(
        functools.partial(_inproj_kernel, tiles_per_seq=seq // TM),
        grid=(m // TM,),
        in_specs=[row(d), _resident((1, d)), _resident(w_tok.shape), _resident(w_feat.shape)],
        out_specs=[colb(_QN)] + [row(n) for n, _ in tok_outs] + [vt_spec, vt_spec, colb(_ZGN)],
        out_shape=([jax.ShapeDtypeStruct((_QN, m), BF16)]
                   + [jax.ShapeDtypeStruct((m, n), dt) for n, dt in tok_outs]
                   + [vt_shape, vt_shape, jax.ShapeDtypeStruct((_ZGN, m), F32)]),
        compiler_params=pltpu.CompilerParams(dimension_semantics=("parallel",), vmem_limit_bytes=VMEM_LIMIT),
        name="inproj",
    )(x2, g_mix, w_tok, w_feat)


N_CMP_PAD = 256
_CMP_HALF = CMP_LEN // 2


def _compress_kernel(ck_ref, cv_ref, pos_ref, w1_ref, w2_ref, w2t_ref, out_ref, outt_ref, *, n_cmp):
    hid = w1_ref.shape[-1]
    first = [jnp.zeros((N_CMP_PAD, hid), F32) for _ in range(4)]
    second = [jnp.zeros((N_CMP_PAD, hid), F32) for _ in range(4)]
    for l in range(_CMP_HALF):
        for kind, src_ref in enumerate((ck_ref, cv_ref)):
            xl = src_ref[pl.ds(l, N_CMP_PAD, stride=CMP_STRIDE), :]
            pcol = slice(kind * LANES, (kind + 1) * LANES)
            xa = (xl + pos_ref[l:l + 1, pcol]).astype(BF16)
            xb = (xl + pos_ref[_CMP_HALF + l:_CMP_HALF + l + 1, pcol]).astype(BF16)
            wa = w1_ref[kind, l * HEAD_DIM:(l + 1) * HEAD_DIM, :]
            wb = w1_ref[kind, (_CMP_HALF + l) * HEAD_DIM:(_CMP_HALF + l + 1) * HEAD_DIM, :]
            for g in range(NSA_GROUPS):
                kg = kind * NSA_GROUPS + g
                cols = slice(g * HEAD_DIM, (g + 1) * HEAD_DIM)
                first[kg] = first[kg] + _dot(xa[:, cols], wa)
                second[kg] = second[kg] + _dot(xb[:, cols], wb)
    rows = lax.broadcasted_iota(jnp.int32, (N_CMP_PAD, LANES), 0)
    lanes = lax.broadcasted_iota(jnp.int32, (LANES, N_CMP_PAD), 1)
    for g in range(NSA_GROUPS):
        acts = []
        for kind in range(2):
            kg = kind * NSA_GROUPS + g
            h1 = first[kg] + pltpu.roll(second[kg], N_CMP_PAD - 1, 0)
            acts.append(_gelu_tanh(h1).astype(BF16))
        kcv = _dot(acts[0], w2_ref[0]) + _dot(acts[1], w2_ref[1])
        kcvt = _dot_nt(w2t_ref[0], acts[0]) + _dot_nt(w2t_ref[1], acts[1])
        out_ref[0, g] = jnp.where(rows < n_cmp, kcv, 0.0).astype(BF16)
        outt_ref[0, g] = jnp.where(lanes < n_cmp, kcvt, 0.0).astype(BF16)


def _compress(cv, pos, w1, w2, w2t, batch, seq):
    n_cmp = (seq - CMP_LEN) // CMP_STRIDE + 1
    assert n_cmp < N_CMP_PAD and seq == N_CMP_PAD * CMP_STRIDE
    return pl.pallas_call(
        functools.partial(_compress_kernel, n_cmp=n_cmp),
        grid=(batch,),
        in_specs=[pl.BlockSpec((seq, LANES), lambda b: (b, 0)), pl.BlockSpec((seq, LANES), lambda b: (b, 1)),
                  _resident(pos.shape), _resident(w1.shape), _resident(w2.shape), _resident(w2t.shape)],
        out_specs=[pl.BlockSpec((1, NSA_GROUPS, N_CMP_PAD, LANES), lambda b: (b, 0, 0, 0)),
                   pl.BlockSpec((1, NSA_GROUPS, LANES, N_CMP_PAD), lambda b: (b, 0, 0, 0))],
        out_shape=[jax.ShapeDtypeStruct((batch, NSA_GROUPS, N_CMP_PAD, LANES), BF16),
                   jax.ShapeDtypeStruct((batch, NSA_GROUPS, LANES, N_CMP_PAD), BF16)],
        compiler_params=pltpu.CompilerParams(dimension_semantics=("parallel",), vmem_limit_bytes=VMEM_LIMIT),
        name="compress",
    )(cv, cv, pos, w1, w2, w2t)


N_SEL = 64


def _attn_kernel(qt_ref, kcv_ref, kcvt_ref, sel_ref, ka_ref, kw_ref, vst_ref, vwt_ref, gtt_ref, o_ref,
                 q0_ref, q1_ref, qs_ref, m_ref, acc_ref, s_ref, smax_ref, p_ref, alpha_ref, oc_ref, ow_ref):
    g = pl.program_id(1)
    qb = pl.program_id(2)
    t0 = qb * TQ
    cols = NSA_HPG * TQ

    zeros = jnp.zeros((HEAD_DIM, TQ), BF16)
    for h in range(NSA_HPG):
        qh = qt_ref[h * HEAD_DIM:(h + 1) * HEAD_DIM, :]
        cs = slice(h * TQ, (h + 1) * TQ)
        q0_ref[0:HEAD_DIM, cs] = qh
        q0_ref[HEAD_DIM:, cs] = zeros
        q1_ref[0:HEAD_DIM, cs] = zeros
        q1_ref[HEAD_DIM:, cs] = qh
        qs_ref[0:HEAD_DIM, cs] = qh
    q0 = q0_ref[...]
    tq = t0 + (lax.broadcasted_iota(jnp.int32, (1, cols), 1) & (TQ - 1))

    sc = _dot(kcv_ref[0, 0], q0)
    wt0 = jnp.maximum(qb // (VT // TQ) - (WIN_TILES - 1), 0)
    kstart = pl.multiple_of(wt0 * VT, VT)
    span = WIN_TILES * VT
    qwin = jnp.where(g == 0, q0, q1_ref[...])
    sw = _dot(kw_ref[pl.ds(kstart, span), :], qwin)

    cidx = lax.broadcasted_iota(jnp.int32, (N_CMP_PAD, 1), 0)
    mask_c = (cidx * CMP_STRIDE + (CMP_LEN - 1)) <= tq
    sc = jnp.where(mask_c, sc, NEG)
    pc = jnp.where(mask_c, jnp.exp(sc - jnp.max(sc, axis=0, keepdims=True)), 0.0)
    lc = jnp.sum(pc, axis=0, keepdims=True)
    pc = pc / jnp.where(lc > 0.0, lc, 1.0)
    oc_ref[...] = _dot(kcvt_ref[0, 0], pc.astype(BF16))[0:HEAD_DIM]

    psum = pc[:, 0:TQ]
    for h in range(1, NSA_HPG):
        psum = psum + pc[:, h * TQ:(h + 1) * TQ]
    sel = sel_ref[...]
    p_hi = psum.astype(BF16)
    rem = psum - p_hi.astype(F32)
    p_mid = rem.astype(BF16)
    p_lo = (rem - p_mid.astype(F32)).astype(BF16)
    imp = (_dot(sel, p_hi) + _dot(sel, p_mid)) + _dot(sel, p_lo)

    dist = tq - (kstart + lax.broadcasted_iota(jnp.int32, (span, 1), 0))
    sw = jnp.where(dist.astype(jnp.uint32) < WINDOW, sw, NEG)
    pw = jnp.exp(sw - jnp.max(sw, axis=0, keepdims=True)).astype(BF16)
    ow = _dot(vwt_ref[wt0], pw[0:VT])
    for c in range(1, WIN_TILES):
        ow = ow + _dot(vwt_ref[wt0 + c], pw[c * VT:(c + 1) * VT])
    ow_ref[...] = ow[0:HEAD_DIM] / ow[HEAD_DIM:HEAD_DIM + 1]

    j = lax.broadcasted_iota(jnp.int32, (N_SEL, TQ), 0)
    cur = (t0 + lax.broadcasted_iota(jnp.int32, (N_SEL, TQ), 1)) >> SEL_SHIFT
    forced = (j == 0) | (j == cur) | (j == cur - 1)
    imp = jnp.where(j <= cur, jnp.where(forced, FORCE_SCORE, imp), NEG)
    blocks = [imp[r * SUBLANES:(r + 1) * SUBLANES] for r in range(N_SEL // SUBLANES)]
    ranks = [jnp.zeros((SUBLANES, TQ), jnp.int32) for _ in blocks]
    jloc = lax.broadcasted_iota(jnp.int32, (SUBLANES, TQ), 0)
    for i in range(N_SEL):
        vi = jnp.broadcast_to(imp[i:i + 1, :], (SUBLANES, TQ))
        for r, blk in enumerate(blocks):
            if r * SUBLANES > i:
                before = vi >= blk
            elif (r + 1) * SUBLANES - 1 <= i:
                before = vi > blk
            else:
                before = (vi > blk) | ((vi == blk) & (jloc > i - r * SUBLANES))
            ranks[r] = ranks[r] + before.astype(jnp.int32)
    rank = jnp.concatenate(ranks, axis=0)
    bias = jnp.where(rank < SEL_TOP, 0.0, MASK_BIAS).astype(BF16)
    for h in range(NSA_HPG):
        qs_ref[HEAD_DIM:, h * TQ:(h + 1) * TQ] = bias

    m_ref[...] = jnp.full(m_ref.shape, NEG, F32)
    acc_ref[...] = jnp.zeros(acc_ref.shape, F32)
    ksub = lax.broadcasted_iota(jnp.int32, (TK, 1), 0)
    even, odd = 0, 1

    def logits(kt, slot):
        k0 = pl.multiple_of(kt * TK, TK)
        s = _dot(ka_ref[pl.ds(k0, TK), :], qs_ref[...])
        s_ref[slot] = s
        smax_ref[slot] = jnp.max(s, axis=0, keepdims=True)

    def softmax(kt, slot, causal):
        s = s_ref[slot]
        if causal:
            s = jnp.where(kt * TK + ksub <= tq, s, NEG)
            smax = jnp.max(s, axis=0, keepdims=True)
        else:
            smax = smax_ref[slot]
        m_prev = m_ref[...]
        m_new = jnp.maximum(m_prev, smax)
        p_ref[slot] = jnp.exp(s - m_new).astype(BF16)
        alpha_ref[slot] = jnp.exp(m_prev - m_new)
        m_ref[...] = m_new

    def accumulate(kt, slot):
        p = p_ref[slot]
        v0 = jnp.maximum(kt, 0) * (TK // VT)
        pv = _dot(vst_ref[v0], p[0:VT])
        for c in range(1, TK // VT):
            pv = pv + _dot(vst_ref[v0 + c], p[c * VT:(c + 1) * VT])
        acc_ref[...] = alpha_ref[slot] * acc_ref[...] + pv

    p_ref[odd] = jnp.zeros(p_ref.shape[1:], BF16)
    alpha_ref[odd] = jnp.ones(alpha_ref.shape[1:], F32)
    last = qb // (TK // TQ)
    pairs = last // 2
    logits(0, even)

    def body(jp, carry):
        a = 2 * jp
        accumulate(a - 1, odd)
        softmax(a, even, False)
        logits(a + 1, odd)
        accumulate(a, even)
        softmax(a + 1, odd, False)
        logits(a + 2, even)
        return carry

    lax.fori_loop(0, pairs, body, 0)
    a = 2 * pairs

    @pl.when(last == a)
    def _():
        accumulate(a - 1, odd)
        softmax(a, even, True)
        accumulate(a, even)

    @pl.when(last != a)
    def _():
        accumulate(a - 1, odd)
        softmax(a, even, False)
        logits(a + 1, odd)
        accumulate(a, even)
        softmax(a + 1, odd, True)
        accumulate(a + 1, odd)

    acc = acc_ref[...]
    o_s = acc[0:HEAD_DIM] / acc[HEAD_DIM:HEAD_DIM + 1]

    o_c = oc_ref[...]
    o_w = ow_ref[...]
    heads = []
    for h in range(NSA_HPG):
        cs = slice(h * TQ, (h + 1) * TQ)

        def gate(br):
            return gtt_ref[pl.ds(br * NSA_HEADS + g * NSA_HPG + h, 1), :]

        heads.append(gate(0) * o_c[:, cs] + gate(1) * o_s[:, cs] + gate(2) * o_w[:, cs])
    o_ref[...] = jnp.concatenate(heads, axis=0).T.astype(BF16)


def _attention(qt, kcv, kcvt, sel, ka, kw, vst, vwt, gtt, batch, seq):
    nq = seq // TQ
    cols = NSA_HPG * TQ
    assert seq // SEL_LEN == N_SEL and WIN_TILES * VT <= seq and TK % VT == 0 and VT % TQ == 0
    assert (WIN_TILES - 1) * VT >= WINDOW
    vt_spec = pl.BlockSpec((seq // VT, LANES, VT), lambda b, g, i: (b, g, 0))
    return pl.pallas_call(
        _attn_kernel,
        grid=(batch, NSA_GROUPS, nq),
        in_specs=[
            pl.BlockSpec((NSA_HPG * HEAD_DIM, TQ), lambda b, g, i: (g, b * nq + i)),
            pl.BlockSpec((1, 1, N_CMP_PAD, LANES), lambda b, g, i: (b, g, 0, 0)),
            pl.BlockSpec((1, 1, LANES, N_CMP_PAD), lambda b, g, i: (b, g, 0, 0)),
            pl.BlockSpec(sel.shape, lambda b, g, i: (0, 0)),
            pl.BlockSpec((seq, LANES), lambda b, g, i: (b, g)),
            pl.BlockSpec((seq, LANES), lambda b, g, i: (b, 0)),
            vt_spec, vt_spec,
            pl.BlockSpec((_ZGN, TQ), lambda b, g, i: (0, b * nq + i)),
        ],
        out_specs=pl.BlockSpec((TQ, NSA_HPG * HEAD_DIM), lambda b, g, i: (b * nq + i, g)),
        out_shape=jax.ShapeDtypeStruct((batch * seq, NSA_HEADS * HEAD_DIM), BF16),
        scratch_shapes=[
            pltpu.VMEM((LANES, cols), BF16),
            pltpu.VMEM((LANES, cols), BF16),
            pltpu.VMEM((LANES, cols), BF16),
            pltpu.VMEM((1, cols), F32),
            pltpu.VMEM((LANES, cols), F32),
            pltpu.VMEM((2, TK, cols), F32),
            pltpu.VMEM((2, 1, cols), F32),
            pltpu.VMEM((2, TK, cols), BF16),
            pltpu.VMEM((2, 1, cols), F32),
            pltpu.VMEM((HEAD_DIM, cols), F32),
            pltpu.VMEM((HEAD_DIM, cols), F32),
        ],
        compiler_params=pltpu.CompilerParams(dimension_semantics=("parallel", "parallel", "arbitrary"),
                                             vmem_limit_bytes=VMEM_LIMIT),
        name="nsa_attention",
    )(qt, kcv, kcvt, sel, ka, kw, vst, vwt, gtt)


def _merge_kernel(o_ref, c_ref, halo_ref, sg_ref, x_ref, wo_ref, cw_ref, cb_ref, lg_ref, lb_ref, wco_ref,
                  bco_ref, wout_ref, h_ref, cext_ref, cn_ref, *, tiles_per_seq):
    first = (pl.program_id(0) % tiles_per_seq) == 0
    cext_ref[0:CONV_HALO, :] = jnp.where(first, 0.0, halo_ref[...])
    cext_ref[CONV_HALO:, :] = c_ref[...]
    shift = CONV_HALO - (CONV_WIDTH - 1)
    for r in range(TM // CONV_ROWS):
        r0 = r * CONV_ROWS
        acc = jnp.broadcast_to(cb_ref[...], (CONV_ROWS, cb_ref.shape[-1]))
        for res in range(SUBLANES):
            taps = [k for k in range(CONV_WIDTH) if (shift + k) % SUBLANES == res]
            n_rows = CONV_ROWS + (SUBLANES if res else 0)
            part = None
            for k in taps:
                base = r0 + shift + k - res
                term = cw_ref[k:k + 1, :] * cext_ref[base:base + n_rows, :]
                part = term if part is None else part + term
            acc = acc + part[res:res + CONV_ROWS]
        xc = acc - jnp.mean(acc, axis=-1, keepdims=True)
        y = xc * lax.rsqrt(jnp.mean(xc * xc, axis=-1, keepdims=True) + EPS) * lg_ref[...] + lb_ref[...]
        cn_ref[r0:r0 + CONV_ROWS, :] = (y * _sigmoid(y)).astype(BF16)
    y_b = _dot(cn_ref[...], wco_ref[...]) + bco_ref[...]
    y_a = _dot(o_ref[...], wo_ref[...])
    d = y_a.shape[-1]
    mix = sg_ref[:, :d] * y_a + sg_ref[:, d:] * y_b
    h_ref[...] = x_ref[...] + _dot(mix.astype(BF16), wout_ref[...])


def _merge(o, c, sg, x2, wo, cw, cb, lg, lb, wco, bco, wout, seq):
    m, d = x2.shape
    ch = c.shape[-1]
    row = lambda n: pl.BlockSpec((TM, n), lambda i: (i, 0))
    halo = pl.BlockSpec((CONV_HALO, ch), lambda i: (jnp.maximum(i * (TM // CONV_HALO) - 1, 0), 0))
    weights = [wo, cw, cb, lg, lb, wco, bco, wout]
    return pl.pallas_call(
        functools.partial(_merge_kernel, tiles_per_seq=seq // TM),
        grid=(m // TM,),
        in_specs=[row(o.shape[-1]), row(ch), halo, row(sg.shape[-1]), row(d)] + [_resident(w.shape) for w in weights],
        out_specs=row(d),
        out_shape=jax.ShapeDtypeStruct((m, d), F32),
        scratch_shapes=[pltpu.VMEM((CONV_HALO + TM, ch), F32), pltpu.VMEM((TM, ch), BF16)],
        compiler_params=pltpu.CompilerParams(dimension_semantics=("parallel",), vmem_limit_bytes=VMEM_LIMIT),
        name="merge",
    )(o, c, c, sg, x2, *weights)


def _ffn_kernel(h_ref, halo_ref, p_ref, gf_ref, wup_ref, fw_ref, fb_ref, wdn_ref, gp_ref, wpg_ref, wple_ref,
                gfin_ref, out_ref, u_ref, vg_ref, vv_ref, *, tiles_per_seq, d_ff, final_norm):
    first = (pl.program_id(0) % tiles_per_seq) == 0
    h = h_ref[...]
    u_ref[0:FFN_HALO, :] = _rms(jnp.where(first, 0.0, halo_ref[...]), gf_ref[...]).astype(BF16)
    u_ref[FFN_HALO:, :] = _rms(h, gf_ref[...]).astype(BF16)
    u = u_ref[...]
    shift = FFN_HALO - (FFN_CONV_WIDTH - 1)

    def conv(v_ref, slot, c0):
        acc = fb_ref[:, c0:c0 + FF_CHUNK]
        for k in range(FFN_CONV_WIDTH):
            acc = acc + fw_ref[k:k + 1, c0:c0 + FF_CHUNK] * v_ref[slot, shift + k:shift + k + TM, :]
        return acc

    def up(jc):
        c0 = jc * FF_CHUNK
        vg_ref[jc % 2] = _dot(u, wup_ref[:, c0:c0 + FF_CHUNK])
        vv_ref[jc % 2] = _dot(u, wup_ref[:, d_ff + c0:d_ff + c0 + FF_CHUNK])

    n_chunks = d_ff // FF_CHUNK
    acc = jnp.zeros(h.shape, F32)
    up(0)
    for jc in range(n_chunks):
        c0 = jc * FF_CHUNK
        if jc + 1 < n_chunks:
            up(jc + 1)
        a = _gelu_tanh(conv(vg_ref, jc % 2, c0)) * conv(vv_ref, jc % 2, d_ff + c0)
        acc = acc + _dot(a.astype(BF16), wdn_ref[c0:c0 + FF_CHUNK, :])
    h = h + acc
    gate = _sigmoid(_dot(_rms(h, gp_ref[...]).astype(BF16), wpg_ref[...]))
    h = h + gate * _dot(p_ref[...].astype(BF16), wple_ref[...])
    out_ref[...] = _rms(h, gfin_ref[...]) if final_norm else h


def _ffn(h1, p2, gf, wup, fw, fb, wdn, gp, wpg, wple, gfin, seq, final_norm):
    m, d = h1.shape
    d_ff = wdn.shape[0]
    assert d_ff % FF_CHUNK == 0
    row = lambda n: pl.BlockSpec((TM, n), lambda i: (i, 0))
    halo = pl.BlockSpec((FFN_HALO, d), lambda i: (jnp.maximum(i * (TM // FFN_HALO) - 1, 0), 0))
    weights = [gf, wup, fw, fb, wdn, gp, wpg, wple, gfin]
    return pl.pallas_call(
        functools.partial(_ffn_kernel, tiles_per_seq=seq // TM, d_ff=d_ff, final_norm=final_norm),
        grid=(m // TM,),
        in_specs=[row(d), halo, row(p2.shape[-1])] + [_resident(w.shape) for w in weights],
        out_specs=row(d),
        out_shape=jax.ShapeDtypeStruct((m, d), F32),
        scratch_shapes=[pltpu.VMEM((FFN_HALO + TM, d), BF16), pltpu.VMEM((2, FFN_HALO + TM, FF_CHUNK), F32),
                        pltpu.VMEM((2, FFN_HALO + TM, FF_CHUNK), F32)],
        compiler_params=pltpu.CompilerParams(dimension_semantics=("parallel",), vmem_limit_bytes=VMEM_LIMIT),
        name="ffn_ple",
    )(h1, h1, p2, *weights)


def _sel_map_t(seq):
    n_cmp = (seq - CMP_LEN) // CMP_STRIDE + 1
    n_sel = seq // SEL_LEN
    c0 = np.arange(n_cmp) * CMP_STRIDE
    j0 = np.arange(n_sel) * SEL_LEN
    lo = np.maximum(c0[None, :], j0[:, None])
    hi = np.minimum(c0[None, :] + CMP_LEN, j0[:, None] + SEL_LEN)
    out = np.zeros((N_SEL, N_CMP_PAD), np.float32)
    out[:n_sel, :n_cmp] = np.maximum(hi - lo, 0) / CMP_LEN
    return jnp.asarray(out, BF16)


def _inproj_weights(w):
    d = w.shape[0]
    nq = NSA_HEADS * HEAD_DIM
    kv0 = nq
    gate0 = kv0 + 3 * 2 * NSA_GROUPS * HEAD_DIM
    conv0 = gate0 + 3 * NSA_HEADS
    merge0 = conv0 + _ZCN

    def kv(br, g):
        lo = kv0 + (br * NSA_GROUPS + g) * HEAD_DIM
        return w[:, lo:lo + HEAD_DIM]

    z = jnp.zeros((d, HEAD_DIM), w.dtype)
    tok = [kv(2, 0), z, kv(2, 1), z, kv(4, 0), kv(4, 1), w[:, kv0:kv0 + _CVN], w[:, conv0:merge0],
           w[:, merge0:merge0 + _ZMN]]
    feat = [w[:, :nq] * (HEAD_DIM ** -0.5), kv(3, 0), kv(3, 1), kv(5, 0), kv(5, 1), w[:, gate0:conv0],
            jnp.zeros((d, _ZGN - 3 * NSA_HEADS), w.dtype)]
    w_tok = jnp.concatenate(tok, axis=1).astype(BF16)
    w_feat = jnp.concatenate(feat, axis=1).T.astype(BF16)
    assert w_tok.shape[1] == _N_TOK and w_feat.shape[0] == _N_FEAT
    return w_tok, w_feat


def kernel(x, p, g_mix, w_in, cmp_pos_k, cmp_pos_v, w_cmp_k1, w_cmp_k2, w_cmp_v1, w_cmp_v2, w_o_nsa, conv_w,
           conv_b, conv_ln_g, conv_ln_b, w_conv_out, b_conv_out, w_out, g_ffn, w_up, ffn_conv_w, ffn_conv_b,
           w_down, g_ple, w_ple_gate, w_ple, g_final):
    batch, seq, d = x.shape
    depth = w_in.shape[0]
    m = batch * seq
    assert seq % TM == 0 and seq % TK == 0
    sel = _sel_map_t(seq)
    row = lambda v: v.reshape(1, -1)
    h = x.reshape(m, d)
    for i in range(depth):
        w_tok, w_feat = _inproj_weights(w_in[i])
        pos = jnp.concatenate([cmp_pos_k[i]] * NSA_GROUPS + [cmp_pos_v[i]] * NSA_GROUPS, axis=1)
        w1 = jnp.stack([w_cmp_k1[i], w_cmp_v1[i]]).astype(BF16)
        zpad = jnp.zeros_like(w_cmp_k2[i])
        w2 = jnp.stack([jnp.concatenate([w_cmp_k2[i], zpad], axis=1),
                        jnp.concatenate([zpad, w_cmp_v2[i]], axis=1)]).astype(BF16)
        w2t = jnp.stack([jnp.concatenate([zpad, w_cmp_k2[i]], axis=1).T,
                         jnp.concatenate([w_cmp_v2[i], zpad], axis=1).T]).astype(BF16)

        qt, ka, kw, cv, c, sg, vst, vwt, gtt = _inproj(h, row(g_mix[i]), w_tok, w_feat, seq)
        kcv, kcvt = _compress(cv, pos, w1, w2, w2t, batch, seq)
        o = _attention(qt, kcv, kcvt, sel, ka, kw, vst, vwt, gtt, batch, seq)
        h = _merge(o, c, sg, h, w_o_nsa[i].astype(BF16), conv_w[i], row(conv_b[i]), row(conv_ln_g[i]),
                   row(conv_ln_b[i]), w_conv_out[i].astype(BF16), row(b_conv_out[i]), w_out[i].astype(BF16), seq)
        h = _ffn(h, p[i].reshape(m, -1), row(g_ffn[i]), w_up[i].astype(BF16), ffn_conv_w[i], row(ffn_conv_b[i]),
                 w_down[i].astype(BF16), row(g_ple[i]), w_ple_gate[i].astype(BF16), w_ple[i].astype(BF16),
                 row(g_final), seq, final_norm=(i == depth - 1))
    return h.reshape(batch, seq, d)
```

```python
import functools

import numpy as np
import jax
import jax.numpy as jnp
from jax import lax
from jax.experimental import pallas as pl
from jax.experimental.pallas import tpu as pltpu

F32 = jnp.float32
BF16 = jnp.bfloat16

NSA_HEADS = 8
NSA_GROUPS = 2
NSA_HPG = NSA_HEADS // NSA_GROUPS
HEAD_DIM = 64
CMP_LEN = 32
CMP_STRIDE = 16
SEL_LEN = 64
SEL_SHIFT = 6
SEL_TOP = 16
WINDOW = 512
FORCE_SCORE = 1e4
CONV_WIDTH = 31
FFN_CONV_WIDTH = 3
EPS = 1e-6
NEG = -1e30
MASK_BIAS = -1e9
Q_SCALE = HEAD_DIM ** -0.5 * float(np.log2(np.e))

LANES = 128
SUBLANES = 8
BF16_ROWS = 16
VMEM_LIMIT = 56 * 1024 * 1024

TM = 512
TQ = 128
TK = 512
VT = 256
WIN_TILES = 3
CONV_HALO = 32
FFN_HALO = BF16_ROWS
CONV_ROWS = 64
FF_CHUNK = 256


def _sigmoid(x):
    return 0.5 * (jnp.tanh(0.5 * x) + 1.0)


def _gelu_tanh(x):
    return 0.5 * x * (1.0 + jnp.tanh(np.sqrt(2.0 / np.pi).astype(np.float32) * (x + 0.044715 * (x * x * x))))


def _rms(x, g):
    return x * lax.rsqrt(jnp.mean(x * x, axis=-1, keepdims=True) + EPS) * g


def _dot(a, b):
    return jnp.dot(a, b, preferred_element_type=F32)


def _dot_nt(a, b):
    return lax.dot_general(a, b, (((1,), (1,)), ((), ())), preferred_element_type=F32)


def _resident(shape):
    nd = len(shape)
    return pl.BlockSpec(shape, lambda *_: (0,) * nd, pipeline_mode=pl.Buffered(1))


_KA0, _KAN = 0, 256
_KW0, _KWN = 256, 128
_CV0, _CVN = 384, 256
_ZC0, _ZCN = 640, 1024
_ZM0, _ZMN = 1664, 2048
_N_TOK = 3712
_Q0, _QN = 0, 512
_VS0, _VSN = 512, 128
_VW0, _VWN = 640, 128
_ZG0, _ZGN = 768, 32
_N_FEAT = 800


def _inproj_kernel(x_ref, g_ref, wt_ref, wf_ref, qt_ref, ka_ref, kw_ref, cv_ref, c_ref, sg_ref, vst_ref, vwt_ref,
                   gtt_ref, *, tiles_per_seq):
    u = _rms(x_ref[...], g_ref[...]).astype(BF16)

    def tok(lo, n):
        return _dot(u, wt_ref[:, lo:lo + n])

    def feat(lo, n):
        return _dot_nt(wf_ref[lo:lo + n, :], u)

    ka = tok(_KA0, _KAN)
    s0 = (pl.program_id(0) % tiles_per_seq) * TM
    row = lax.broadcasted_iota(jnp.int32, ka.shape, 0) + s0
    col = lax.broadcasted_iota(jnp.int32, ka.shape, 1)
    onehot = ((col & HEAD_DIM) != 0) & ((col & (HEAD_DIM - 1)) == (row >> SEL_SHIFT))
    ka_ref[...] = jnp.where(onehot, 1.0, ka).astype(BF16)
    kw_ref[...] = tok(_KW0, _KWN).astype(BF16)
    cv_ref[...] = tok(_CV0, _CVN)
    zc = tok(_ZC0, _ZCN)
    half = _ZCN // 2
    c_ref[...] = zc[:, :half] * _sigmoid(zc[:, half:])
    sg_ref[...] = _sigmoid(tok(_ZM0, _ZMN))

    qt_ref[...] = feat(_Q0, _QN).astype(BF16)
    gtt_ref[...] = _sigmoid(feat(_ZG0, _ZGN))
    ones = jnp.ones((HEAD_DIM, VT), BF16)
    for src0, dst_ref in ((_VS0, vst_ref), (_VW0, vwt_ref)):
        v = feat(src0, NSA_GROUPS * HEAD_DIM).astype(BF16)
        for t in range(TM // VT):
            for g in range(NSA_GROUPS):
                r0 = g * 2 * HEAD_DIM
                dst_ref[t, r0:r0 + HEAD_DIM, :] = v[g * HEAD_DIM:(g + 1) * HEAD_DIM, t * VT:(t + 1) * VT]
                dst_ref[t, r0 + HEAD_DIM:r0 + 2 * HEAD_DIM, :] = ones


def _inproj(x2, g_mix, w_tok, w_feat, seq):
    m, d = x2.shape
    row = lambda n: pl.BlockSpec((TM, n), lambda i: (i, 0))
    colb = lambda n: pl.BlockSpec((n, TM), lambda i: (0, i))
    vt_spec = pl.BlockSpec((TM // VT, NSA_GROUPS * LANES, VT), lambda i: (i, 0, 0))
    vt_shape = jax.ShapeDtypeStruct((m // VT, NSA_GROUPS * LANES, VT), BF16)
    tok_outs = [(_KAN, BF16), (_KWN, BF16), (_CVN, F32), (_ZCN // 2, F32), (_ZMN, F32)]
    return pl.pallas_call(
        functools.partial(_inproj_kernel, tiles_per_seq=seq // TM),
        grid=(m // TM,),
        in_specs=[row(d), _resident((1, d)), _resident(w_tok.shape), _resident(w_feat.shape)],
        out_specs=[colb(_QN)] + [row(n) for n, _ in tok_outs] + [vt_spec, vt_spec, colb(_ZGN)],
        out_shape=([jax.ShapeDtypeStruct((_QN, m), BF16)]
                   + [jax.ShapeDtypeStruct((m, n), dt) for n, dt in tok_outs]
                   + [vt_shape, vt_shape, jax.ShapeDtypeStruct((_ZGN, m), F32)]),
        compiler_params=pltpu.CompilerParams(dimension_semantics=("parallel",), vmem_limit_bytes=VMEM_LIMIT),
        name="inproj",
    )(x2, g_mix, w_tok, w_feat)


N_CMP_PAD = 256
_CMP_HALF = CMP_LEN // 2


def _compress_kernel(ck_ref, cv_ref, pos_ref, w1_ref, w2_ref, w2t_ref, out_ref, outt_ref, *, n_cmp):
    hid = w1_ref.shape[-1]
    first = [jnp.zeros((N_CMP_PAD, hid), F32) for _ in range(4)]
    second = [jnp.zeros((N_CMP_PAD, hid), F32) for _ in range(4)]
    for l in range(_CMP_HALF):
        for kind, src_ref in enumerate((ck_ref, cv_ref)):
            xl = src_ref[pl.ds(l, N_CMP_PAD, stride=CMP_STRIDE), :]
            pcol = slice(kind * LANES, (kind + 1) * LANES)
            xa = (xl + pos_ref[l:l + 1, pcol]).astype(BF16)
            xb = (xl + pos_ref[_CMP_HALF + l:_CMP_HALF + l + 1, pcol]).astype(BF16)
            wa = w1_ref[kind, l * HEAD_DIM:(l + 1) * HEAD_DIM, :]
            wb = w1_ref[kind, (_CMP_HALF + l) * HEAD_DIM:(_CMP_HALF + l + 1) * HEAD_DIM, :]
            for g in range(NSA_GROUPS):
                kg = kind * NSA_GROUPS + g
                cols = slice(g * HEAD_DIM, (g + 1) * HEAD_DIM)
                first[kg] = first[kg] + _dot(xa[:, cols], wa)
                second[kg] = second[kg] + _dot(xb[:, cols], wb)
    rows = lax.broadcasted_iota(jnp.int32, (N_CMP_PAD, LANES), 0)
    lanes = lax.broadcasted_iota(jnp.int32, (LANES, N_CMP_PAD), 1)
    for g in range(NSA_GROUPS):
        acts = []
        for kind in range(2):
            kg = kind * NSA_GROUPS + g
            h1 = first[kg] + pltpu.roll(second[kg], N_CMP_PAD - 1, 0)
            acts.append(_gelu_tanh(h1).astype(BF16))
        kcv = _dot(acts[0], w2_ref[0]) + _dot(acts[1], w2_ref[1])
        kcvt = _dot_nt(w2t_ref[0], acts[0]) + _dot_nt(w2t_ref[1], acts[1])
        out_ref[0, g] = jnp.where(rows < n_cmp, kcv, 0.0).astype(BF16)
        outt_ref[0, g] = jnp.where(lanes < n_cmp, kcvt, 0.0).astype(BF16)


def _compress(cv, pos, w1, w2, w2t, batch, seq):
    n_cmp = (seq - CMP_LEN) // CMP_STRIDE + 1
    assert n_cmp < N_CMP_PAD and seq == N_CMP_PAD * CMP_STRIDE
    return pl.pallas_call(
        functools.partial(_compress_kernel, n_cmp=n_cmp),
        grid=(batch,),
        in_specs=[pl.BlockSpec((seq, LANES), lambda b: (b, 0)), pl.BlockSpec((seq, LANES), lambda b: (b, 1)),
                  _resident(pos.shape), _resident(w1.shape), _resident(w2.shape), _resident(w2t.shape)],
        out_specs=[pl.BlockSpec((1, NSA_GROUPS, N_CMP_PAD, LANES), lambda b: (b, 0, 0, 0)),
                   pl.BlockSpec((1, NSA_GROUPS, LANES, N_CMP_PAD), lambda b: (b, 0, 0, 0))],
        out_shape=[jax.ShapeDtypeStruct((batch, NSA_GROUPS, N_CMP_PAD, LANES), BF16),
                   jax.ShapeDtypeStruct((batch, NSA_GROUPS, LANES, N_CMP_PAD), BF16)],
        compiler_params=pltpu.CompilerParams(dimension_semantics=("parallel",), vmem_limit_bytes=VMEM_LIMIT),
        name="compress",
    )(cv, cv, pos, w1, w2, w2t)


N_SEL = 64


def _attn_kernel(qt_ref, kcv_ref, kcvt_ref, sel_ref, ka_ref, kw_ref, vst_ref, vwt_ref, gtt_ref, o_ref,
                 q0_ref, q1_ref, qs_ref, m_ref, acc_ref, s_ref, smax_ref, p_ref, alpha_ref, oc_ref, ow_ref):
    g = pl.program_id(1)
    qb = pl.program_id(2)
    t0 = qb * TQ
    cols = NSA_HPG * TQ

    zeros = jnp.zeros((HEAD_DIM, TQ), BF16)
    for h in range(NSA_HPG):
        qh = qt_ref[h * HEAD_DIM:(h + 1) * HEAD_DIM, :]
        cs = slice(h * TQ, (h + 1) * TQ)
        q0_ref[0:HEAD_DIM, cs] = qh
        q0_ref[HEAD_DIM:, cs] = zeros
        q1_ref[0:HEAD_DIM, cs] = zeros
        q1_ref[HEAD_DIM:, cs] = qh
        qs_ref[0:HEAD_DIM, cs] = qh
    q0 = q0_ref[...]
    tq = t0 + (lax.broadcasted_iota(jnp.int32, (1, cols), 1) & (TQ - 1))

    sc = _dot(kcv_ref[0, 0], q0)
    wt0 = jnp.maximum(qb // (VT // TQ) - (WIN_TILES - 1), 0)
    kstart = pl.multiple_of(wt0 * VT, VT)
    span = WIN_TILES * VT
    qwin = jnp.where(g == 0, q0, q1_ref[...])
    sw = _dot(kw_ref[pl.ds(kstart, span), :], qwin)

    tq1 = t0 + lax.broadcasted_iota(jnp.int32, (1, TQ), 1)

    def per_head(x):
        return jnp.concatenate([x] * NSA_HPG, axis=1)

    cend = lax.broadcasted_iota(jnp.int32, (N_CMP_PAD, 1), 0) * CMP_STRIDE + (CMP_LEN - 1)
    sc = sc + per_head(jnp.where(cend <= tq1, 0.0, NEG))
    ec = jnp.exp2(sc - jnp.max(sc, axis=0, keepdims=True))
    lc = jnp.sum(ec, axis=0, keepdims=True)
    visible = tq >= CMP_LEN - 1
    pc = ec * jnp.where(visible, 1.0 / lc, 0.0)
    oc_ref[...] = _dot(kcvt_ref[0, 0], pc.astype(BF16))[0:HEAD_DIM]

    psum = pc[:, 0:TQ]
    for h in range(1, NSA_HPG):
        psum = psum + pc[:, h * TQ:(h + 1) * TQ]
    sel = sel_ref[...]
    p_hi = psum.astype(BF16)
    rem = psum - p_hi.astype(F32)
    p_mid = rem.astype(BF16)
    p_lo = (rem - p_mid.astype(F32)).astype(BF16)
    imp = (_dot(sel, p_hi) + _dot(sel, p_mid)) + _dot(sel, p_lo)

    dist = tq1 - (kstart + lax.broadcasted_iota(jnp.int32, (span, 1), 0))
    sw = sw + per_head(jnp.where(dist.astype(jnp.uint32) < WINDOW, 0.0, NEG))
    pw = jnp.exp2(sw - jnp.max(sw, axis=0, keepdims=True)).astype(BF16)
    ow = _dot(vwt_ref[wt0], pw[0:VT])
    for c in range(1, WIN_TILES):
        ow = ow + _dot(vwt_ref[wt0 + c], pw[c * VT:(c + 1) * VT])
    ow_ref[...] = ow[0:HEAD_DIM] * (1.0 / ow[HEAD_DIM:HEAD_DIM + 1])

    j = lax.broadcasted_iota(jnp.int32, (N_SEL, TQ), 0)
    cur = (t0 + lax.broadcasted_iota(jnp.int32, (N_SEL, TQ), 1)) >> SEL_SHIFT

    forced = (j == 0) | (j == cur) | (j == cur - 1)
    score = jnp.where(j <= cur, jnp.where(forced, FORCE_SCORE, imp), NEG)
    blocks = [score[r * SUBLANES:(r + 1) * SUBLANES] for r in range(N_SEL // SUBLANES)]
    ranks = [jnp.zeros((SUBLANES, TQ), jnp.int32) for _ in blocks]
    jloc = lax.broadcasted_iota(jnp.int32, (SUBLANES, TQ), 0)
    for i in range(N_SEL):
        vi = jnp.broadcast_to(score[i:i + 1, :], (SUBLANES, TQ))
        for r, blk in enumerate(blocks):
            if r * SUBLANES > i:
                before = vi >= blk
            elif (r + 1) * SUBLANES - 1 <= i:
                before = vi > blk
            else:
                before = (vi > blk) | ((vi == blk) & (jloc > i - r * SUBLANES))
            ranks[r] = ranks[r] + before.astype(jnp.int32)
    bias = jnp.where(jnp.concatenate(ranks, axis=0) < SEL_TOP, 0.0, MASK_BIAS).astype(BF16)
    for h in range(NSA_HPG):
        qs_ref[HEAD_DIM:, h * TQ:(h + 1) * TQ] = bias

    m_ref[...] = jnp.full(m_ref.shape, NEG, F32)
    acc_ref[...] = jnp.zeros(acc_ref.shape, F32)
    ksub = lax.broadcasted_iota(jnp.int32, (TK, 1), 0)
    even, odd = 0, 1

    def logits(kt, slot):
        k0 = pl.multiple_of(kt * TK, TK)
        s = _dot(ka_ref[pl.ds(k0, TK), :], qs_ref[...])
        s_ref[slot] = s
        smax_ref[slot] = jnp.max(s, axis=0, keepdims=True)

    def softmax(kt, slot, causal):
        s = s_ref[slot]
        if causal:
            s = jnp.where(kt * TK + ksub <= tq, s, NEG)
            smax = jnp.max(s, axis=0, keepdims=True)
        else:
            smax = smax_ref[slot]
        m_prev = m_ref[...]
        m_new = jnp.maximum(m_prev, smax)
        p_ref[slot] = jnp.exp2(s - m_new).astype(BF16)
        alpha_ref[slot] = jnp.exp2(m_prev - m_new)
        m_ref[...] = m_new

    def accumulate(kt, slot):
        p = p_ref[slot]
        v0 = jnp.maximum(kt, 0) * (TK // VT)
        pv = _dot(vst_ref[v0], p[0:VT])
        for c in range(1, TK // VT):
            pv = pv + _dot(vst_ref[v0 + c], p[c * VT:(c + 1) * VT])
        acc_ref[...] = alpha_ref[slot] * acc_ref[...] + pv

    p_ref[odd] = jnp.zeros(p_ref.shape[1:], BF16)
    alpha_ref[odd] = jnp.ones(alpha_ref.shape[1:], F32)
    last = qb // (TK // TQ)
    pairs = last // 2
    logits(0, even)

    def body(jp, carry):
        a = 2 * jp
        accumulate(a - 1, odd)
        softmax(a, even, False)
        logits(a + 1, odd)
        accumulate(a, even)
        softmax(a + 1, odd, False)
        logits(a + 2, even)
        return carry

    lax.fori_loop(0, pairs, body, 0)
    a = 2 * pairs

    @pl.when(last == a)
    def _():
        accumulate(a - 1, odd)
        softmax(a, even, True)
        accumulate(a, even)

    @pl.when(last != a)
    def _():
        accumulate(a - 1, odd)
        softmax(a, even, False)
        logits(a + 1, odd)
        accumulate(a, even)
        softmax(a + 1, odd, True)
        accumulate(a + 1, odd)

    acc = acc_ref[...]
    o_s = acc[0:HEAD_DIM] * (1.0 / acc[HEAD_DIM:HEAD_DIM + 1])

    o_c = oc_ref[...]
    o_w = ow_ref[...]
    heads = []
    for h in range(NSA_HPG):
        cs = slice(h * TQ, (h + 1) * TQ)

        def gate(br):
            return gtt_ref[pl.ds(br * NSA_HEADS + g * NSA_HPG + h, 1), :]

        heads.append(gate(0) * o_c[:, cs] + gate(1) * o_s[:, cs] + gate(2) * o_w[:, cs])
    o_ref[...] = jnp.concatenate(heads, axis=0).T.astype(BF16)


def _attention(qt, kcv, kcvt, sel, ka, kw, vst, vwt, gtt, batch, seq):
    nq = seq // TQ
    cols = NSA_HPG * TQ
    assert seq // SEL_LEN == N_SEL and WIN_TILES * VT <= seq and TK % VT == 0 and VT % TQ == 0
    assert (WIN_TILES - 1) * VT >= WINDOW
    vt_spec = pl.BlockSpec((seq // VT, LANES, VT), lambda b, g, i: (b, g, 0))
    return pl.pallas_call(
        _attn_kernel,
        grid=(batch, NSA_GROUPS, nq),
        in_specs=[
            pl.BlockSpec((NSA_HPG * HEAD_DIM, TQ), lambda b, g, i: (g, b * nq + i)),
            pl.BlockSpec((1, 1, N_CMP_PAD, LANES), lambda b, g, i: (b, g, 0, 0)),
            pl.BlockSpec((1, 1, LANES, N_CMP_PAD), lambda b, g, i: (b, g, 0, 0)),
            pl.BlockSpec(sel.shape, lambda b, g, i: (0, 0)),
            pl.BlockSpec((seq, LANES), lambda b, g, i: (b, g)),
            pl.BlockSpec((seq, LANES), lambda b, g, i: (b, 0)),
            vt_spec, vt_spec,
            pl.BlockSpec((_ZGN, TQ), lambda b, g, i: (0, b * nq + i)),
        ],
        out_specs=pl.BlockSpec((TQ, NSA_HPG * HEAD_DIM), lambda b, g, i: (b * nq + i, g)),
        out_shape=jax.ShapeDtypeStruct((batch * seq, NSA_HEADS * HEAD_DIM), BF16),
        scratch_shapes=[
            pltpu.VMEM((LANES, cols), BF16),
            pltpu.VMEM((LANES, cols), BF16),
            pltpu.VMEM((LANES, cols), BF16),
            pltpu.VMEM((1, cols), F32),
            pltpu.VMEM((LANES, cols), F32),
            pltpu.VMEM((2, TK, cols), F32),
            pltpu.VMEM((2, 1, cols), F32),
            pltpu.VMEM((2, TK, cols), BF16),
            pltpu.VMEM((2, 1, cols), F32),
            pltpu.VMEM((HEAD_DIM, cols), F32),
            pltpu.VMEM((HEAD_DIM, cols), F32),
        ],
        compiler_params=pltpu.CompilerParams(dimension_semantics=("parallel", "parallel", "arbitrary"),
                                             vmem_limit_bytes=VMEM_LIMIT),
        name="nsa_attention",
    )(qt, kcv, kcvt, sel, ka, kw, vst, vwt, gtt)


def _merge_kernel(o_ref, c_ref, halo_ref, sg_ref, x_ref, wo_ref, cw_ref, cb_ref, lg_ref, lb_ref, wco_ref,
                  bco_ref, wout_ref, h_ref, cext_ref, cn_ref, *, tiles_per_seq):
    first = (pl.program_id(0) % tiles_per_seq) == 0
    cext_ref[0:CONV_HALO, :] = jnp.where(first, 0.0, halo_ref[...])
    cext_ref[CONV_HALO:, :] = c_ref[...]
    shift = CONV_HALO - (CONV_WIDTH - 1)
    for r in range(TM // CONV_ROWS):
        r0 = r * CONV_ROWS
        acc = jnp.broadcast_to(cb_ref[...], (CONV_ROWS, cb_ref.shape[-1]))
        for res in range(SUBLANES):
            taps = [k for k in range(CONV_WIDTH) if (shift + k) % SUBLANES == res]
            n_rows = CONV_ROWS + (SUBLANES if res else 0)
            part = None
            for k in taps:
                base = r0 + shift + k - res
                term = cw_ref[k:k + 1, :] * cext_ref[base:base + n_rows, :]
                part = term if part is None else part + term
            acc = acc + part[res:res + CONV_ROWS]
        xc = acc - jnp.mean(acc, axis=-1, keepdims=True)
        y = xc * lax.rsqrt(jnp.mean(xc * xc, axis=-1, keepdims=True) + EPS) * lg_ref[...] + lb_ref[...]
        cn_ref[r0:r0 + CONV_ROWS, :] = (y * _sigmoid(y)).astype(BF16)
    y_b = _dot(cn_ref[...], wco_ref[...]) + bco_ref[...]
    y_a = _dot(o_ref[...], wo_ref[...])
    d = y_a.shape[-1]
    mix = sg_ref[:, :d] * y_a + sg_ref[:, d:] * y_b
    h_ref[...] = x_ref[...] + _dot(mix.astype(BF16), wout_ref[...])


def _merge(o, c, sg, x2, wo, cw, cb, lg, lb, wco, bco, wout, seq):
    m, d = x2.shape
    ch = c.shape[-1]
    row = lambda n: pl.BlockSpec((TM, n), lambda i: (i, 0))
    halo = pl.BlockSpec((CONV_HALO, ch), lambda i: (jnp.maximum(i * (TM // CONV_HALO) - 1, 0), 0))
    weights = [wo, cw, cb, lg, lb, wco, bco, wout]
    return pl.pallas_call(
        functools.partial(_merge_kernel, tiles_per_seq=seq // TM),
        grid=(m // TM,),
        in_specs=[row(o.shape[-1]), row(ch), halo, row(sg.shape[-1]), row(d)] + [_resident(w.shape) for w in weights],
        out_specs=row(d),
        out_shape=jax.ShapeDtypeStruct((m, d), F32),
        scratch_shapes=[pltpu.VMEM((CONV_HALO + TM, ch), F32), pltpu.VMEM((TM, ch), BF16)],
        compiler_params=pltpu.CompilerParams(dimension_semantics=("parallel",), vmem_limit_bytes=VMEM_LIMIT),
        name="merge",
    )(o, c, c, sg, x2, *weights)


def _ffn_kernel(h_ref, halo_ref, p_ref, gf_ref, wup_ref, fw_ref, fb_ref, wdn_ref, gp_ref, wpg_ref, wple_ref,
                gfin_ref, out_ref, u_ref, vg_ref, vv_ref, *, tiles_per_seq, d_ff, final_norm):
    first = (pl.program_id(0) % tiles_per_seq) == 0
    h = h_ref[...]
    u_ref[0:FFN_HALO, :] = _rms(jnp.where(first, 0.0, halo_ref[...]), gf_ref[...]).astype(BF16)
    u_ref[FFN_HALO:, :] = _rms(h, gf_ref[...]).astype(BF16)
    u = u_ref[...]
    shift = FFN_HALO - (FFN_CONV_WIDTH - 1)

    def conv(v_ref, slot, c0):
        acc = fb_ref[:, c0:c0 + FF_CHUNK]
        for k in range(FFN_CONV_WIDTH):
            acc = acc + fw_ref[k:k + 1, c0:c0 + FF_CHUNK] * v_ref[slot, shift + k:shift + k + TM, :]
        return acc

    def up(jc):
        c0 = jc * FF_CHUNK
        vg_ref[jc % 2] = _dot(u, wup_ref[:, c0:c0 + FF_CHUNK])
        vv_ref[jc % 2] = _dot(u, wup_ref[:, d_ff + c0:d_ff + c0 + FF_CHUNK])

    n_chunks = d_ff // FF_CHUNK
    acc = jnp.zeros(h.shape, F32)
    up(0)
    for jc in range(n_chunks):
        c0 = jc * FF_CHUNK
        if jc + 1 < n_chunks:
            up(jc + 1)
        a = _gelu_tanh(conv(vg_ref, jc % 2, c0)) * conv(vv_ref, jc % 2, d_ff + c0)
        acc = acc + _dot(a.astype(BF16), wdn_ref[c0:c0 + FF_CHUNK, :])
    h = h + acc
    gate = _sigmoid(_dot(_rms(h, gp_ref[...]).astype(BF16), wpg_ref[...]))
    h = h + gate * _dot(p_ref[...].astype(BF16), wple_ref[...])
    out_ref[...] = _rms(h, gfin_ref[...]) if final_norm else h


def _ffn(h1, p2, gf, wup, fw, fb, wdn, gp, wpg, wple, gfin, seq, final_norm):
    m, d = h1.shape
    d_ff = wdn.shape[0]
    assert d_ff % FF_CHUNK == 0
    row = lambda n: pl.BlockSpec((TM, n), lambda i: (i, 0))
    halo = pl.BlockSpec((FFN_HALO, d), lambda i: (jnp.maximum(i * (TM // FFN_HALO) - 1, 0), 0))
    weights = [gf, wup, fw, fb, wdn, gp, wpg, wple, gfin]
    return pl.pallas_call(
        functools.partial(_ffn_kernel, tiles_per_seq=seq // TM, d_ff=d_ff, final_norm=final_norm),
        grid=(m // TM,),
        in_specs=[row(d), halo, row(p2.shape[-1])] + [_resident(w.shape) for w in weights],
        out_specs=row(d),
        out_shape=jax.ShapeDtypeStruct((m, d), F32),
        scratch_shapes=[pltpu.VMEM((FFN_HALO + TM, d), BF16), pltpu.VMEM((2, FFN_HALO + TM, FF_CHUNK), F32),
                        pltpu.VMEM((2, FFN_HALO + TM, FF_CHUNK), F32)],
        compiler_params=pltpu.CompilerParams(dimension_semantics=("parallel",), vmem_limit_bytes=VMEM_LIMIT),
        name="ffn_ple",
    )(h1, h1, p2, *weights)


def _sel_map_t(seq):
    n_cmp = (seq - CMP_LEN) // CMP_STRIDE + 1
    n_sel = seq // SEL_LEN
    c0 = np.arange(n_cmp) * CMP_STRIDE
    j0 = np.arange(n_sel) * SEL_LEN
    lo = np.maximum(c0[None, :], j0[:, None])
    hi = np.minimum(c0[None, :] + CMP_LEN, j0[:, None] + SEL_LEN)
    out = np.zeros((N_SEL, N_CMP_PAD), np.float32)
    out[:n_sel, :n_cmp] = np.maximum(hi - lo, 0) / CMP_LEN
    return jnp.asarray(out, BF16)


def _inproj_weights(w):
    d = w.shape[0]
    nq = NSA_HEADS * HEAD_DIM
    kv0 = nq
    gate0 = kv0 + 3 * 2 * NSA_GROUPS * HEAD_DIM
    conv0 = gate0 + 3 * NSA_HEADS
    merge0 = conv0 + _ZCN

    def kv(br, g):
        lo = kv0 + (br * NSA_GROUPS + g) * HEAD_DIM
        return w[:, lo:lo + HEAD_DIM]

    z = jnp.zeros((d, HEAD_DIM), w.dtype)
    tok = [kv(2, 0), z, kv(2, 1), z, kv(4, 0), kv(4, 1), w[:, kv0:kv0 + _CVN], w[:, conv0:merge0],
           w[:, merge0:merge0 + _ZMN]]
    feat = [w[:, :nq] * Q_SCALE, kv(3, 0), kv(3, 1), kv(5, 0), kv(5, 1), w[:, gate0:conv0],
            jnp.zeros((d, _ZGN - 3 * NSA_HEADS), w.dtype)]
    w_tok = jnp.concatenate(tok, axis=1).astype(BF16)
    w_feat = jnp.concatenate(feat, axis=1).T.astype(BF16)
    assert w_tok.shape[1] == _N_TOK and w_feat.shape[0] == _N_FEAT
    return w_tok, w_feat


def kernel(x, p, g_mix, w_in, cmp_pos_k, cmp_pos_v, w_cmp_k1, w_cmp_k2, w_cmp_v1, w_cmp_v2, w_o_nsa, conv_w,
           conv_b, conv_ln_g, conv_ln_b, w_conv_out, b_conv_out, w_out, g_ffn, w_up, ffn_conv_w, ffn_conv_b,
           w_down, g_ple, w_ple_gate, w_ple, g_final):
    batch, seq, d = x.shape
    depth = w_in.shape[0]
    m = batch * seq
    assert seq % TM == 0 and seq % TK == 0
    sel = _sel_map_t(seq)
    row = lambda v: v.reshape(1, -1)
    h = x.reshape(m, d)
    for i in range(depth):
        w_tok, w_feat = _inproj_weights(w_in[i])
        pos = jnp.concatenate([cmp_pos_k[i]] * NSA_GROUPS + [cmp_pos_v[i]] * NSA_GROUPS, axis=1)
        w1 = jnp.stack([w_cmp_k1[i], w_cmp_v1[i]]).astype(BF16)
        zpad = jnp.zeros_like(w_cmp_k2[i])
        w2 = jnp.stack([jnp.concatenate([w_cmp_k2[i], zpad], axis=1),
                        jnp.concatenate([zpad, w_cmp_v2[i]], axis=1)]).astype(BF16)
        w2t = jnp.stack([jnp.concatenate([zpad, w_cmp_k2[i]], axis=1).T,
                         jnp.concatenate([w_cmp_v2[i], zpad], axis=1).T]).astype(BF16)

        qt, ka, kw, cv, c, sg, vst, vwt, gtt = _inproj(h, row(g_mix[i]), w_tok, w_feat, seq)
        kcv, kcvt = _compress(cv, pos, w1, w2, w2t, batch, seq)
        o = _attention(qt, kcv, kcvt, sel, ka, kw, vst, vwt, gtt, batch, seq)
        h = _merge(o, c, sg, h, w_o_nsa[i].astype(BF16), conv_w[i], row(conv_b[i]), row(conv_ln_g[i]),
                   row(conv_ln_b[i]), w_conv_out[i].astype(BF16), row(b_conv_out[i]), w_out[i].astype(BF16), seq)
        h = _ffn(h, p[i].reshape(m, -1), row(g_ffn[i]), w_up[i].astype(BF16), ffn_conv_w[i], row(ffn_conv_b[i]),
                 w_down[i].astype(BF16), row(g_ple[i]), w_ple_gate[i].astype(BF16), w_ple[i].astype(BF16),
                 row(g_final), seq, final_norm=(i == depth - 1))
    return h.reshape(batch, seq, d)
```

```python
import functools

import numpy as np
import jax
import jax.numpy as jnp
from jax import lax
from jax.experimental import pallas as pl
from jax.experimental.pallas import tpu as pltpu

F32 = jnp.float32
BF16 = jnp.bfloat16

NSA_HEADS = 8
NSA_GROUPS = 2
NSA_HPG = NSA_HEADS // NSA_GROUPS
HEAD_DIM = 64
CMP_LEN = 32
CMP_STRIDE = 16
SEL_LEN = 64
SEL_SHIFT = 6
SEL_TOP = 16
WINDOW = 512
FORCE_SCORE = 1e4
CONV_WIDTH = 31
FFN_CONV_WIDTH = 3
EPS = 1e-6
NEG = -1e30
MASK_BIAS = -1e9
Q_SCALE = HEAD_DIM ** -0.5 * float(np.log2(np.e))

LANES = 128
SUBLANES = 8
BF16_ROWS = 16
VMEM_LIMIT = 56 * 1024 * 1024

TM = 512
TQ = 128
TK = 512
VT = 256
WIN_TILES = 3
CONV_HALO = 32
FFN_HALO = BF16_ROWS
CONV_ROWS = 64
FF_CHUNK = 256


def _sigmoid(x):
    return 0.5 * (jnp.tanh(0.5 * x) + 1.0)


def _gelu_tanh(x):
    return 0.5 * x * (1.0 + jnp.tanh(np.sqrt(2.0 / np.pi).astype(np.float32) * (x + 0.044715 * (x * x * x))))


def _rms(x, g):
    return x * lax.rsqrt(jnp.mean(x * x, axis=-1, keepdims=True) + EPS) * g


def _zero_after(x):
    bits = lax.bitcast_convert_type(x, jnp.uint32)
    bits = lax.shift_right_logical(lax.shift_right_logical(bits, jnp.uint32(16)), jnp.uint32(16))
    return lax.bitcast_convert_type(bits, F32)


def _dot(a, b):
    return jnp.dot(a, b, preferred_element_type=F32)


def _dot_nt(a, b):
    return lax.dot_general(a, b, (((1,), (1,)), ((), ())), preferred_element_type=F32)


def _resident(shape):
    nd = len(shape)
    return pl.BlockSpec(shape, lambda *_: (0,) * nd, pipeline_mode=pl.Buffered(1))


_KA0, _KAN = 0, 256
_KW0, _KWN = 256, 128
_CV0, _CVN = 384, 256
_ZC0, _ZCN = 640, 1024
_ZM0, _ZMN = 1664, 2048
_N_TOK = 3712
_Q0, _QN = 0, 512
_VS0, _VSN = 512, 128
_VW0, _VWN = 640, 128
_ZG0, _ZGN = 768, 32
_N_FEAT = 800


def _inproj_kernel(x_ref, g_ref, wt_ref, wf_ref, qt_ref, ka_ref, kw_ref, cv_ref, c_ref, sg_ref, vst_ref, vwt_ref,
                   gtt_ref, *, tiles_per_seq):
    u = _rms(x_ref[...], g_ref[...]).astype(BF16)

    def tok(lo, n):
        return _dot(u, wt_ref[:, lo:lo + n])

    def feat(lo, n):
        return _dot_nt(wf_ref[lo:lo + n, :], u)

    ka = tok(_KA0, _KAN)
    s0 = (pl.program_id(0) % tiles_per_seq) * TM
    row = lax.broadcasted_iota(jnp.int32, ka.shape, 0) + s0
    col = lax.broadcasted_iota(jnp.int32, ka.shape, 1)
    onehot = ((col & HEAD_DIM) != 0) & ((col & (HEAD_DIM - 1)) == (row >> SEL_SHIFT))
    ka_ref[...] = jnp.where(onehot, 1.0, ka).astype(BF16)
    kw_ref[...] = tok(_KW0, _KWN).astype(BF16)
    cv_ref[...] = tok(_CV0, _CVN)
    zc = tok(_ZC0, _ZCN)
    half = _ZCN // 2
    c_ref[...] = zc[:, :half] * _sigmoid(zc[:, half:])
    sg_ref[...] = _sigmoid(tok(_ZM0, _ZMN))

    qt_ref[...] = feat(_Q0, _QN).astype(BF16)
    gtt_ref[...] = _sigmoid(feat(_ZG0, _ZGN))
    ones = jnp.ones((HEAD_DIM, VT), BF16)
    for src0, dst_ref in ((_VS0, vst_ref), (_VW0, vwt_ref)):
        v = feat(src0, NSA_GROUPS * HEAD_DIM).astype(BF16)
        for t in range(TM // VT):
            for g in range(NSA_GROUPS):
                r0 = g * 2 * HEAD_DIM
                dst_ref[t, r0:r0 + HEAD_DIM, :] = v[g * HEAD_DIM:(g + 1) * HEAD_DIM, t * VT:(t + 1) * VT]
                dst_ref[t, r0 + HEAD_DIM:r0 + 2 * HEAD_DIM, :] = ones


def _inproj(x2, g_mix, w_tok, w_feat, seq):
    m, d = x2.shape
    row = lambda n: pl.BlockSpec((TM, n), lambda i: (i, 0))
    colb = lambda n: pl.BlockSpec((n, TM), lambda i: (0, i))
    vt_spec = pl.BlockSpec((TM // VT, NSA_GROUPS * LANES, VT), lambda i: (i, 0, 0))
    vt_shape = jax.ShapeDtypeStruct((m // VT, NSA_GROUPS * LANES, VT), BF16)
    tok_outs = [(_KAN, BF16), (_KWN, BF16), (_CVN, F32), (_ZCN // 2, F32), (_ZMN, F32)]
    return pl.pallas_call(
        functools.partial(_inproj_kernel, tiles_per_seq=seq // TM),
        grid=(m // TM,),
        in_specs=[row(d), _resident((1, d)), _resident(w_tok.shape), _resident(w_feat.shape)],
        out_specs=[colb(_QN)] + [row(n) for n, _ in tok_outs] + [vt_spec, vt_spec, colb(_ZGN)],
        out_shape=([jax.ShapeDtypeStruct((_QN, m), BF16)]
                   + [jax.ShapeDtypeStruct((m, n), dt) for n, dt in tok_outs]
                   + [vt_shape, vt_shape, jax.ShapeDtypeStruct((_ZGN, m), F32)]),
        compiler_params=pltpu.CompilerParams(dimension_semantics=("parallel",), vmem_limit_bytes=VMEM_LIMIT),
        name="inproj",
    )(x2, g_mix, w_tok, w_feat)


N_CMP_PAD = 256
_CMP_HALF = CMP_LEN // 2


def _compress_kernel(ck_ref, cv_ref, pos_ref, w1_ref, w2_ref, w2t_ref, out_ref, outt_ref, *, n_cmp):
    hid = w1_ref.shape[-1]
    first = [jnp.zeros((N_CMP_PAD, hid), F32) for _ in range(4)]
    second = [jnp.zeros((N_CMP_PAD, hid), F32) for _ in range(4)]
    for l in range(_CMP_HALF):
        for kind, src_ref in enumerate((ck_ref, cv_ref)):
            xl = src_ref[pl.ds(l, N_CMP_PAD, stride=CMP_STRIDE), :]
            pcol = slice(kind * LANES, (kind + 1) * LANES)
            xa = (xl + pos_ref[l:l + 1, pcol]).astype(BF16)
            xb = (xl + pos_ref[_CMP_HALF + l:_CMP_HALF + l + 1, pcol]).astype(BF16)
            wa = w1_ref[kind, l * HEAD_DIM:(l + 1) * HEAD_DIM, :]
            wb = w1_ref[kind, (_CMP_HALF + l) * HEAD_DIM:(_CMP_HALF + l + 1) * HEAD_DIM, :]
            for g in range(NSA_GROUPS):
                kg = kind * NSA_GROUPS + g
                cols = slice(g * HEAD_DIM, (g + 1) * HEAD_DIM)
                first[kg] = first[kg] + _dot(xa[:, cols], wa)
                second[kg] = second[kg] + _dot(xb[:, cols], wb)
    rows = lax.broadcasted_iota(jnp.int32, (N_CMP_PAD, LANES), 0)
    lanes = lax.broadcasted_iota(jnp.int32, (LANES, N_CMP_PAD), 1)
    for g in range(NSA_GROUPS):
        acts = []
        for kind in range(2):
            kg = kind * NSA_GROUPS + g
            h1 = first[kg] + pltpu.roll(second[kg], N_CMP_PAD - 1, 0)
            acts.append(_gelu_tanh(h1).astype(BF16))
        kcv = _dot(acts[0], w2_ref[0]) + _dot(acts[1], w2_ref[1])
        kcvt = _dot_nt(w2t_ref[0], acts[0]) + _dot_nt(w2t_ref[1], acts[1])
        out_ref[0, g] = jnp.where(rows < n_cmp, kcv, 0.0).astype(BF16)
        outt_ref[0, g] = jnp.where(lanes < n_cmp, kcvt, 0.0).astype(BF16)


def _compress(cv, pos, w1, w2, w2t, batch, seq):
    n_cmp = (seq - CMP_LEN) // CMP_STRIDE + 1
    assert n_cmp < N_CMP_PAD and seq == N_CMP_PAD * CMP_STRIDE
    return pl.pallas_call(
        functools.partial(_compress_kernel, n_cmp=n_cmp),
        grid=(batch,),
        in_specs=[pl.BlockSpec((seq, LANES), lambda b: (b, 0)), pl.BlockSpec((seq, LANES), lambda b: (b, 1)),
                  _resident(pos.shape), _resident(w1.shape), _resident(w2.shape), _resident(w2t.shape)],
        out_specs=[pl.BlockSpec((1, NSA_GROUPS, N_CMP_PAD, LANES), lambda b: (b, 0, 0, 0)),
                   pl.BlockSpec((1, NSA_GROUPS, LANES, N_CMP_PAD), lambda b: (b, 0, 0, 0))],
        out_shape=[jax.ShapeDtypeStruct((batch, NSA_GROUPS, N_CMP_PAD, LANES), BF16),
                   jax.ShapeDtypeStruct((batch, NSA_GROUPS, LANES, N_CMP_PAD), BF16)],
        compiler_params=pltpu.CompilerParams(dimension_semantics=("parallel",), vmem_limit_bytes=VMEM_LIMIT),
        name="compress",
    )(cv, cv, pos, w1, w2, w2t)


N_SEL = 64


def _attn_kernel(qt_ref, kcv_ref, kcvt_ref, sel_ref, ka_ref, kw_ref, vst_ref, vwt_ref, gtt_ref, o_ref,
                 q0_ref, q1_ref, qs_ref, m_ref, acc_ref, s_ref, smax_ref, p_ref, alpha_ref, oc_ref, ow_ref):
    g = pl.program_id(1)
    qb = pl.program_id(2)
    t0 = qb * TQ
    cols = NSA_HPG * TQ

    zeros = jnp.zeros((HEAD_DIM, TQ), BF16)
    for h in range(NSA_HPG):
        qh = qt_ref[h * HEAD_DIM:(h + 1) * HEAD_DIM, :]
        cs = slice(h * TQ, (h + 1) * TQ)
        q0_ref[0:HEAD_DIM, cs] = qh
        q0_ref[HEAD_DIM:, cs] = zeros
        q1_ref[0:HEAD_DIM, cs] = zeros
        q1_ref[HEAD_DIM:, cs] = qh
        qs_ref[0:HEAD_DIM, cs] = qh
    q0 = q0_ref[...]
    tq = t0 + (lax.broadcasted_iota(jnp.int32, (1, cols), 1) & (TQ - 1))

    sc = _dot(kcv_ref[0, 0], q0)
    wt0 = jnp.maximum(qb // (VT // TQ) - (WIN_TILES - 1), 0)
    kstart = pl.multiple_of(wt0 * VT, VT)
    span = WIN_TILES * VT
    qwin = jnp.where(g == 0, q0, q1_ref[...])
    sw = _dot(kw_ref[pl.ds(kstart, span), :], qwin)
    blocks0 = TK // SEL_LEN
    raw0 = _dot(ka_ref[0:TK, :], q0)
    s_ref[0] = raw0
    bmax0 = jnp.max(raw0.reshape(blocks0, SEL_LEN, cols), axis=1)

    tq1 = t0 + lax.broadcasted_iota(jnp.int32, (1, TQ), 1)

    def per_head(x):
        return jnp.concatenate([x] * NSA_HPG, axis=1)

    cend = lax.broadcasted_iota(jnp.int32, (N_CMP_PAD, 1), 0) * CMP_STRIDE + (CMP_LEN - 1)
    sc = sc + per_head(jnp.where(cend <= tq1, 0.0, NEG))
    ec = jnp.exp2(sc - jnp.max(sc, axis=0, keepdims=True))
    lc = jnp.sum(ec, axis=0, keepdims=True)
    visible = tq >= CMP_LEN - 1
    dist = tq1 - (kstart + lax.broadcasted_iota(jnp.int32, (span, 1), 0))
    sw = sw + per_head(jnp.where(dist.astype(jnp.uint32) < WINDOW, 0.0, NEG))
    wmax = jnp.max(sw, axis=0, keepdims=True)
    early = jnp.maximum(wmax, jnp.max(bmax0, axis=0, keepdims=True))
    pc = ec * (jnp.where(visible, 1.0 / lc, 0.0) + _zero_after(early))
    oc_ref[...] = _dot(kcvt_ref[0, 0], pc.astype(BF16))[0:HEAD_DIM]

    psum = pc[:, 0:TQ]
    for h in range(1, NSA_HPG):
        psum = psum + pc[:, h * TQ:(h + 1) * TQ]
    sel = sel_ref[...]
    p_hi = psum.astype(BF16)
    rem = psum - p_hi.astype(F32)
    p_mid = rem.astype(BF16)
    p_lo = (rem - p_mid.astype(F32)).astype(BF16)
    imp = (_dot(sel, p_hi) + _dot(sel, p_mid)) + _dot(sel, p_lo)

    pw = jnp.exp2(sw - wmax).astype(BF16)
    ow = _dot(vwt_ref[wt0], pw[0:VT])
    for c in range(1, WIN_TILES):
        ow = ow + _dot(vwt_ref[wt0 + c], pw[c * VT:(c + 1) * VT])
    ow_ref[...] = ow[0:HEAD_DIM] * (1.0 / ow[HEAD_DIM:HEAD_DIM + 1])

    j = lax.broadcasted_iota(jnp.int32, (N_SEL, TQ), 0)
    cur = (t0 + lax.broadcasted_iota(jnp.int32, (N_SEL, TQ), 1)) >> SEL_SHIFT

    forced = (j == 0) | (j == cur) | (j == cur - 1)
    score = jnp.where(j <= cur, jnp.where(forced, FORCE_SCORE, imp), NEG)
    blocks = [score[r * SUBLANES:(r + 1) * SUBLANES] for r in range(N_SEL // SUBLANES)]
    ranks = [jnp.zeros((SUBLANES, TQ), jnp.int32) for _ in blocks]
    jloc = lax.broadcasted_iota(jnp.int32, (SUBLANES, TQ), 0)
    for i in range(N_SEL):
        vi = jnp.broadcast_to(score[i:i + 1, :], (SUBLANES, TQ))
        for r, blk in enumerate(blocks):
            if r * SUBLANES > i:
                before = vi >= blk
            elif (r + 1) * SUBLANES - 1 <= i:
                before = vi > blk
            else:
                before = (vi > blk) | ((vi == blk) & (jloc > i - r * SUBLANES))
            ranks[r] = ranks[r] + before.astype(jnp.int32)
    bias = jnp.where(jnp.concatenate(ranks, axis=0) < SEL_TOP, 0.0, MASK_BIAS).astype(BF16)
    for h in range(NSA_HPG):
        qs_ref[HEAD_DIM:, h * TQ:(h + 1) * TQ] = bias
    bias0 = per_head(bias.astype(F32)[0:blocks0])
    s_ref[0] = (s_ref[0].reshape(blocks0, SEL_LEN, cols) + bias0[:, None, :]).reshape(TK, cols)
    smax_ref[0] = jnp.max(bmax0 + bias0, axis=0, keepdims=True)

    m_ref[...] = jnp.full(m_ref.shape, NEG, F32)
    acc_ref[...] = jnp.zeros(acc_ref.shape, F32)
    ksub = lax.broadcasted_iota(jnp.int32, (TK, 1), 0)
    even, odd = 0, 1

    def logits(kt, slot):
        k0 = pl.multiple_of(kt * TK, TK)
        s = _dot(ka_ref[pl.ds(k0, TK), :], qs_ref[...])
        s_ref[slot] = s
        smax_ref[slot] = jnp.max(s, axis=0, keepdims=True)

    def softmax(kt, slot, causal):
        s = s_ref[slot]
        if causal:
            s = jnp.where(kt * TK + ksub <= tq, s, NEG)
            smax = jnp.max(s, axis=0, keepdims=True)
        else:
            smax = smax_ref[slot]
        m_prev = m_ref[...]
        m_new = jnp.maximum(m_prev, smax)
        p_ref[slot] = jnp.exp2(s - m_new).astype(BF16)
        alpha_ref[slot] = jnp.exp2(m_prev - m_new)
        m_ref[...] = m_new

    def accumulate(kt, slot):
        p = p_ref[slot]
        v0 = jnp.maximum(kt, 0) * (TK // VT)
        pv = _dot(vst_ref[v0], p[0:VT])
        for c in range(1, TK // VT):
            pv = pv + _dot(vst_ref[v0 + c], p[c * VT:(c + 1) * VT])
        acc_ref[...] = alpha_ref[slot] * acc_ref[...] + pv

    p_ref[odd] = jnp.zeros(p_ref.shape[1:], BF16)
    alpha_ref[odd] = jnp.ones(alpha_ref.shape[1:], F32)
    last = qb // (TK // TQ)
    pairs = last // 2

    def body(jp, carry):
        a = 2 * jp
        accumulate(a - 1, odd)
        softmax(a, even, False)
        logits(a + 1, odd)
        accumulate(a, even)
        softmax(a + 1, odd, False)
        logits(a + 2, even)
        return carry

    lax.fori_loop(0, pairs, body, 0)
    a = 2 * pairs

    @pl.when(last == a)
    def _():
        accumulate(a - 1, odd)
        softmax(a, even, True)
        accumulate(a, even)

    @pl.when(last != a)
    def _():
        accumulate(a - 1, odd)
        softmax(a, even, False)
        logits(a + 1, odd)
        accumulate(a, even)
        softmax(a + 1, odd, True)
        accumulate(a + 1, odd)

    acc = acc_ref[...]
    o_s = acc[0:HEAD_DIM] * (1.0 / acc[HEAD_DIM:HEAD_DIM + 1])

    o_c = oc_ref[...]
    o_w = ow_ref[...]
    heads = []
    for h in range(NSA_HPG):
        cs = slice(h * TQ, (h + 1) * TQ)

        def gate(br):
            return gtt_ref[pl.ds(br * NSA_HEADS + g * NSA_HPG + h, 1), :]

        heads.append(gate(0) * o_c[:, cs] + gate(1) * o_s[:, cs] + gate(2) * o_w[:, cs])
    o_ref[...] = jnp.concatenate(heads, axis=0).T.astype(BF16)


def _attention(qt, kcv, kcvt, sel, ka, kw, vst, vwt, gtt, batch, seq):
    nq = seq // TQ
    cols = NSA_HPG * TQ
    assert seq // SEL_LEN == N_SEL and WIN_TILES * VT <= seq and TK % VT == 0 and VT % TQ == 0
    assert (WIN_TILES - 1) * VT >= WINDOW
    vt_spec = pl.BlockSpec((seq // VT, LANES, VT), lambda b, g, i: (b, g, 0))
    return pl.pallas_call(
        _attn_kernel,
        grid=(batch, NSA_GROUPS, nq),
        in_specs=[
            pl.BlockSpec((NSA_HPG * HEAD_DIM, TQ), lambda b, g, i: (g, b * nq + i)),
            pl.BlockSpec((1, 1, N_CMP_PAD, LANES), lambda b, g, i: (b, g, 0, 0)),
            pl.BlockSpec((1, 1, LANES, N_CMP_PAD), lambda b, g, i: (b, g, 0, 0)),
            pl.BlockSpec(sel.shape, lambda b, g, i: (0, 0)),
            pl.BlockSpec((seq, LANES), lambda b, g, i: (b, g)),
            pl.BlockSpec((seq, LANES), lambda b, g, i: (b, 0)),
            vt_spec, vt_spec,
            pl.BlockSpec((_ZGN, TQ), lambda b, g, i: (0, b * nq + i)),
        ],
        out_specs=pl.BlockSpec((TQ, NSA_HPG * HEAD_DIM), lambda b, g, i: (b * nq + i, g)),
        out_shape=jax.ShapeDtypeStruct((batch * seq, NSA_HEADS * HEAD_DIM), BF16),
        scratch_shapes=[
            pltpu.VMEM((LANES, cols), BF16),
            pltpu.VMEM((LANES, cols), BF16),
            pltpu.VMEM((LANES, cols), BF16),
            pltpu.VMEM((1, cols), F32),
            pltpu.VMEM((LANES, cols), F32),
            pltpu.VMEM((2, TK, cols), F32),
            pltpu.VMEM((2, 1, cols), F32),
            pltpu.VMEM((2, TK, cols), BF16),
            pltpu.VMEM((2, 1, cols), F32),
            pltpu.VMEM((HEAD_DIM, cols), F32),
            pltpu.VMEM((HEAD_DIM, cols), F32),
        ],
        compiler_params=pltpu.CompilerParams(dimension_semantics=("parallel", "parallel", "arbitrary"),
                                             vmem_limit_bytes=VMEM_LIMIT),
        name="nsa_attention",
    )(qt, kcv, kcvt, sel, ka, kw, vst, vwt, gtt)


def _merge_kernel(o_ref, c_ref, halo_ref, sg_ref, x_ref, wo_ref, cw_ref, cb_ref, lg_ref, lb_ref, wco_ref,
                  bco_ref, wout_ref, h_ref, cext_ref, cn_ref, *, tiles_per_seq):
    first = (pl.program_id(0) % tiles_per_seq) == 0
    cext_ref[0:CONV_HALO, :] = jnp.where(first, 0.0, halo_ref[...])
    cext_ref[CONV_HALO:, :] = c_ref[...]
    shift = CONV_HALO - (CONV_WIDTH - 1)
    for r in range(TM // CONV_ROWS):
        r0 = r * CONV_ROWS
        acc = jnp.broadcast_to(cb_ref[...], (CONV_ROWS, cb_ref.shape[-1]))
        for res in range(SUBLANES):
            taps = [k for k in range(CONV_WIDTH) if (shift + k) % SUBLANES == res]
            n_rows = CONV_ROWS + (SUBLANES if res else 0)
            part = None
            for k in taps:
                base = r0 + shift + k - res
                term = cw_ref[k:k + 1, :] * cext_ref[base:base + n_rows, :]
                part = term if part is None else part + term
            acc = acc + part[res:res + CONV_ROWS]
        xc = acc - jnp.mean(acc, axis=-1, keepdims=True)
        y = xc * lax.rsqrt(jnp.mean(xc * xc, axis=-1, keepdims=True) + EPS) * lg_ref[...] + lb_ref[...]
        cn_ref[r0:r0 + CONV_ROWS, :] = (y * _sigmoid(y)).astype(BF16)
    y_b = _dot(cn_ref[...], wco_ref[...]) + bco_ref[...]
    y_a = _dot(o_ref[...], wo_ref[...])
    d = y_a.shape[-1]
    mix = sg_ref[:, :d] * y_a + sg_ref[:, d:] * y_b
    h_ref[...] = x_ref[...] + _dot(mix.astype(BF16), wout_ref[...])


def _merge(o, c, sg, x2, wo, cw, cb, lg, lb, wco, bco, wout, seq):
    m, d = x2.shape
    ch = c.shape[-1]
    row = lambda n: pl.BlockSpec((TM, n), lambda i: (i, 0))
    halo = pl.BlockSpec((CONV_HALO, ch), lambda i: (jnp.maximum(i * (TM // CONV_HALO) - 1, 0), 0))
    weights = [wo, cw, cb, lg, lb, wco, bco, wout]
    return pl.pallas_call(
        functools.partial(_merge_kernel, tiles_per_seq=seq // TM),
        grid=(m // TM,),
        in_specs=[row(o.shape[-1]), row(ch), halo, row(sg.shape[-1]), row(d)] + [_resident(w.shape) for w in weights],
        out_specs=row(d),
        out_shape=jax.ShapeDtypeStruct((m, d), F32),
        scratch_shapes=[pltpu.VMEM((CONV_HALO + TM, ch), F32), pltpu.VMEM((TM, ch), BF16)],
        compiler_params=pltpu.CompilerParams(dimension_semantics=("parallel",), vmem_limit_bytes=VMEM_LIMIT),
        name="merge",
    )(o, c, c, sg, x2, *weights)


def _ffn_kernel(h_ref, halo_ref, p_ref, gf_ref, wup_ref, fw_ref, fb_ref, wdn_ref, gp_ref, wpg_ref, wple_ref,
                gfin_ref, out_ref, u_ref, vg_ref, vv_ref, *, tiles_per_seq, d_ff, final_norm):
    first = (pl.program_id(0) % tiles_per_seq) == 0
    h = h_ref[...]
    u_ref[0:FFN_HALO, :] = _rms(jnp.where(first, 0.0, halo_ref[...]), gf_ref[...]).astype(BF16)
    u_ref[FFN_HALO:, :] = _rms(h, gf_ref[...]).astype(BF16)
    u = u_ref[...]
    shift = FFN_HALO - (FFN_CONV_WIDTH - 1)

    def conv(v_ref, slot, c0):
        acc = fb_ref[:, c0:c0 + FF_CHUNK]
        for k in range(FFN_CONV_WIDTH):
            acc = acc + fw_ref[k:k + 1, c0:c0 + FF_CHUNK] * v_ref[slot, shift + k:shift + k + TM, :]
        return acc

    def up(jc):
        c0 = jc * FF_CHUNK
        vg_ref[jc % 2] = _dot(u, wup_ref[:, c0:c0 + FF_CHUNK])
        vv_ref[jc % 2] = _dot(u, wup_ref[:, d_ff + c0:d_ff + c0 + FF_CHUNK])

    n_chunks = d_ff // FF_CHUNK
    acc = jnp.zeros(h.shape, F32)
    up(0)
    for jc in range(n_chunks):
        c0 = jc * FF_CHUNK
        if jc + 1 < n_chunks:
            up(jc + 1)
        a = _gelu_tanh(conv(vg_ref, jc % 2, c0)) * conv(vv_ref, jc % 2, d_ff + c0)
        acc = acc + _dot(a.astype(BF16), wdn_ref[c0:c0 + FF_CHUNK, :])
    h = h + acc
    gate = _sigmoid(_dot(_rms(h, gp_ref[...]).astype(BF16), wpg_ref[...]))
    h = h + gate * _dot(p_ref[...].astype(BF16), wple_ref[...])
    out_ref[...] = _rms(h, gfin_ref[...]) if final_norm else h


def _ffn(h1, p2, gf, wup, fw, fb, wdn, gp, wpg, wple, gfin, seq, final_norm):
    m, d = h1.shape
    d_ff = wdn.shape[0]
    assert d_ff % FF_CHUNK == 0
    row = lambda n: pl.BlockSpec((TM, n), lambda i: (i, 0))
    halo = pl.BlockSpec((FFN_HALO, d), lambda i: (jnp.maximum(i * (TM // FFN_HALO) - 1, 0), 0))
    weights = [gf, wup, fw, fb, wdn, gp, wpg, wple, gfin]
    return pl.pallas_call(
        functools.partial(_ffn_kernel, tiles_per_seq=seq // TM, d_ff=d_ff, final_norm=final_norm),
        grid=(m // TM,),
        in_specs=[row(d), halo, row(p2.shape[-1])] + [_resident(w.shape) for w in weights],
        out_specs=row(d),
        out_shape=jax.ShapeDtypeStruct((m, d), F32),
        scratch_shapes=[pltpu.VMEM((FFN_HALO + TM, d), BF16), pltpu.VMEM((2, FFN_HALO + TM, FF_CHUNK), F32),
                        pltpu.VMEM((2, FFN_HALO + TM, FF_CHUNK), F32)],
        compiler_params=pltpu.CompilerParams(dimension_semantics=("parallel",), vmem_limit_bytes=VMEM_LIMIT),
        name="ffn_ple",
    )(h1, h1, p2, *weights)


def _sel_map_t(seq):
    n_cmp = (seq - CMP_LEN) // CMP_STRIDE + 1
    n_sel = seq // SEL_LEN
    c0 = np.arange(n_cmp) * CMP_STRIDE
    j0 = np.arange(n_sel) * SEL_LEN
    lo = np.maximum(c0[None, :], j0[:, None])
    hi = np.minimum(c0[None, :] + CMP_LEN, j0[:, None] + SEL_LEN)
    out = np.zeros((N_SEL, N_CMP_PAD), np.float32)
    out[:n_sel, :n_cmp] = np.maximum(hi - lo, 0) / CMP_LEN
    return jnp.asarray(out, BF16)


def _inproj_weights(w):
    d = w.shape[0]
    nq = NSA_HEADS * HEAD_DIM
    kv0 = nq
    gate0 = kv0 + 3 * 2 * NSA_GROUPS * HEAD_DIM
    conv0 = gate0 + 3 * NSA_HEADS
    merge0 = conv0 + _ZCN

    def kv(br, g):
        lo = kv0 + (br * NSA_GROUPS + g) * HEAD_DIM
        return w[:, lo:lo + HEAD_DIM]

    z = jnp.zeros((d, HEAD_DIM), w.dtype)
    tok = [kv(2, 0), z, kv(2, 1), z, kv(4, 0), kv(4, 1), w[:, kv0:kv0 + _CVN], w[:, conv0:merge0],
           w[:, merge0:merge0 + _ZMN]]
    feat = [w[:, :nq] * Q_SCALE, kv(3, 0), kv(3, 1), kv(5, 0), kv(5, 1), w[:, gate0:conv0],
            jnp.zeros((d, _ZGN - 3 * NSA_HEADS), w.dtype)]
    w_tok = jnp.concatenate(tok, axis=1).astype(BF16)
    w_feat = jnp.concatenate(feat, axis=1).T.astype(BF16)
    assert w_tok.shape[1] == _N_TOK and w_feat.shape[0] == _N_FEAT
    return w_tok, w_feat


def kernel(x, p, g_mix, w_in, cmp_pos_k, cmp_pos_v, w_cmp_k1, w_cmp_k2, w_cmp_v1, w_cmp_v2, w_o_nsa, conv_w,
           conv_b, conv_ln_g, conv_ln_b, w_conv_out, b_conv_out, w_out, g_ffn, w_up, ffn_conv_w, ffn_conv_b,
           w_down, g_ple, w_ple_gate, w_ple, g_final):
    batch, seq, d = x.shape
    depth = w_in.shape[0]
    m = batch * seq
    assert seq % TM == 0 and seq % TK == 0
    sel = _sel_map_t(seq)
    row = lambda v: v.reshape(1, -1)
    h = x.reshape(m, d)
    for i in range(depth):
        w_tok, w_feat = _inproj_weights(w_in[i])
        pos = jnp.concatenate([cmp_pos_k[i]] * NSA_GROUPS + [cmp_pos_v[i]] * NSA_GROUPS, axis=1)
        w1 = jnp.stack([w_cmp_k1[i], w_cmp_v1[i]]).astype(BF16)
        zpad = jnp.zeros_like(w_cmp_k2[i])
        w2 = jnp.stack([jnp.concatenate([w_cmp_k2[i], zpad], axis=1),
                        jnp.concatenate([zpad, w_cmp_v2[i]], axis=1)]).astype(BF16)
        w2t = jnp.stack([jnp.concatenate([zpad, w_cmp_k2[i]], axis=1).T,
                         jnp.concatenate([w_cmp_v2[i], zpad], axis=1).T]).astype(BF16)

        qt, ka, kw, cv, c, sg, vst, vwt, gtt = _inproj(h, row(g_mix[i]), w_tok, w_feat, seq)
        kcv, kcvt = _compress(cv, pos, w1, w2, w2t, batch, seq)
        o = _attention(qt, kcv, kcvt, sel, ka, kw, vst, vwt, gtt, batch, seq)
        h = _merge(o, c, sg, h, w_o_nsa[i].astype(BF16), conv_w[i], row(conv_b[i]), row(conv_ln_g[i]),
                   row(conv_ln_b[i]), w_conv_out[i].astype(BF16), row(b_conv_out[i]), w_out[i].astype(BF16), seq)
        h = _ffn(h, p[i].reshape(m, -1), row(g_ffn[i]), w_up[i].astype(BF16), ffn_conv_w[i], row(ffn_conv_b[i]),
                 w_down[i].astype(BF16), row(g_ple[i]), w_ple_gate[i].astype(BF16), w_ple[i].astype(BF16),
                 row(g_final), seq, final_norm=(i == depth - 1))
    return h.reshape(batch, seq, d)
```

```python
import functools

import numpy as np
import jax
import jax.numpy as jnp
from jax import lax
from jax.experimental import pallas as pl
from jax.experimental.pallas import tpu as pltpu

F32 = jnp.float32
BF16 = jnp.bfloat16

NSA_HEADS = 8
NSA_GROUPS = 2
NSA_HPG = NSA_HEADS // NSA_GROUPS
HEAD_DIM = 64
CMP_LEN = 32
CMP_STRIDE = 16
SEL_LEN = 64
SEL_SHIFT = 6
SEL_TOP = 16
WINDOW = 512
FORCE_SCORE = 1e4
CONV_WIDTH = 31
FFN_CONV_WIDTH = 3
EPS = 1e-6
NEG = -1e30
MASK_BIAS = -1e9
Q_SCALE = HEAD_DIM ** -0.5 * float(np.log2(np.e))

LANES = 128
SUBLANES = 8
BF16_ROWS = 16
VMEM_LIMIT = 56 * 1024 * 1024

TM = 512
TQ = 256
TK = 512
VT = 256
WIN_TILES = 3
CONV_HALO = 32
FFN_HALO = BF16_ROWS
CONV_ROWS = 64
FF_CHUNK = 256


def _sigmoid(x):
    return 0.5 * (jnp.tanh(0.5 * x) + 1.0)


def _gelu_tanh(x):
    return 0.5 * x * (1.0 + jnp.tanh(np.sqrt(2.0 / np.pi).astype(np.float32) * (x + 0.044715 * (x * x * x))))


def _rms(x, g):
    return x * lax.rsqrt(jnp.mean(x * x, axis=-1, keepdims=True) + EPS) * g


def _zero_after(x):
    bits = lax.bitcast_convert_type(x, jnp.uint32)
    bits = lax.shift_right_logical(lax.shift_right_logical(bits, jnp.uint32(16)), jnp.uint32(16))
    return lax.bitcast_convert_type(bits, F32)


def _dot(a, b):
    return jnp.dot(a, b, preferred_element_type=F32)


def _dot_nt(a, b):
    return lax.dot_general(a, b, (((1,), (1,)), ((), ())), preferred_element_type=F32)


def _resident(shape):
    nd = len(shape)
    return pl.BlockSpec(shape, lambda *_: (0,) * nd, pipeline_mode=pl.Buffered(1))


_KA0, _KAN = 0, 256
_KW0, _KWN = 256, 128
_CV0, _CVN = 384, 256
_ZC0, _ZCN = 640, 1024
_ZM0, _ZMN = 1664, 2048
_N_TOK = 3712
_Q0, _QN = 0, 512
_VS0, _VSN = 512, 128
_VW0, _VWN = 640, 128
_ZG0, _ZGN = 768, 32
_N_FEAT = 800


def _inproj_kernel(x_ref, g_ref, wt_ref, wf_ref, qt_ref, ka_ref, kw_ref, cv_ref, c_ref, sg_ref, vst_ref, vwt_ref,
                   gtt_ref, *, tiles_per_seq):
    u = _rms(x_ref[...], g_ref[...]).astype(BF16)

    def tok(lo, n):
        return _dot(u, wt_ref[:, lo:lo + n])

    def feat(lo, n):
        return _dot_nt(wf_ref[lo:lo + n, :], u)

    ka = tok(_KA0, _KAN)
    s0 = (pl.program_id(0) % tiles_per_seq) * TM
    row = lax.broadcasted_iota(jnp.int32, ka.shape, 0) + s0
    col = lax.broadcasted_iota(jnp.int32, ka.shape, 1)
    onehot = ((col & HEAD_DIM) != 0) & ((col & (HEAD_DIM - 1)) == (row >> SEL_SHIFT))
    ka_ref[...] = jnp.where(onehot, 1.0, ka).astype(BF16)
    kw_ref[...] = tok(_KW0, _KWN).astype(BF16)
    cv_ref[...] = tok(_CV0, _CVN)
    zc = tok(_ZC0, _ZCN)
    half = _ZCN // 2
    c_ref[...] = zc[:, :half] * _sigmoid(zc[:, half:])
    sg_ref[...] = _sigmoid(tok(_ZM0, _ZMN))

    qt_ref[...] = feat(_Q0, _QN).astype(BF16)
    gtt_ref[...] = _sigmoid(feat(_ZG0, _ZGN))
    ones = jnp.ones((HEAD_DIM, VT), BF16)
    for src0, dst_ref in ((_VS0, vst_ref), (_VW0, vwt_ref)):
        v = feat(src0, NSA_GROUPS * HEAD_DIM).astype(BF16)
        for t in range(TM // VT):
            for g in range(NSA_GROUPS):
                r0 = g * 2 * HEAD_DIM
                dst_ref[t, r0:r0 + HEAD_DIM, :] = v[g * HEAD_DIM:(g + 1) * HEAD_DIM, t * VT:(t + 1) * VT]
                dst_ref[t, r0 + HEAD_DIM:r0 + 2 * HEAD_DIM, :] = ones


def _inproj(x2, g_mix, w_tok, w_feat, seq):
    m, d = x2.shape
    row = lambda n: pl.BlockSpec((TM, n), lambda i: (i, 0))
    colb = lambda n: pl.BlockSpec((n, TM), lambda i: (0, i))
    vt_spec = pl.BlockSpec((TM // VT, NSA_GROUPS * LANES, VT), lambda i: (i, 0, 0))
    vt_shape = jax.ShapeDtypeStruct((m // VT, NSA_GROUPS * LANES, VT), BF16)
    tok_outs = [(_KAN, BF16), (_KWN, BF16), (_CVN, F32), (_ZCN // 2, F32), (_ZMN, F32)]
    return pl.pallas_call(
        functools.partial(_inproj_kernel, tiles_per_seq=seq // TM),
        grid=(m // TM,),
        in_specs=[row(d), _resident((1, d)), _resident(w_tok.shape), _resident(w_feat.shape)],
        out_specs=[colb(_QN)] + [row(n) for n, _ in tok_outs] + [vt_spec, vt_spec, colb(_ZGN)],
        out_shape=([jax.ShapeDtypeStruct((_QN, m), BF16)]
                   + [jax.ShapeDtypeStruct((m, n), dt) for n, dt in tok_outs]
                   + [vt_shape, vt_shape, jax.ShapeDtypeStruct((_ZGN, m), F32)]),
        compiler_params=pltpu.CompilerParams(dimension_semantics=("parallel",), vmem_limit_bytes=VMEM_LIMIT),
        name="inproj",
    )(x2, g_mix, w_tok, w_feat)


N_CMP_PAD = 256
_CMP_HALF = CMP_LEN // 2


def _compress_kernel(ck_ref, cv_ref, pos_ref, w1_ref, w2_ref, w2t_ref, out_ref, outt_ref, *, n_cmp):
    hid = w1_ref.shape[-1]
    first = [jnp.zeros((N_CMP_PAD, hid), F32) for _ in range(4)]
    second = [jnp.zeros((N_CMP_PAD, hid), F32) for _ in range(4)]
    for l in range(_CMP_HALF):
        for kind, src_ref in enumerate((ck_ref, cv_ref)):
            xl = src_ref[pl.ds(l, N_CMP_PAD, stride=CMP_STRIDE), :]
            pcol = slice(kind * LANES, (kind + 1) * LANES)
            xa = (xl + pos_ref[l:l + 1, pcol]).astype(BF16)
            xb = (xl + pos_ref[_CMP_HALF + l:_CMP_HALF + l + 1, pcol]).astype(BF16)
            wa = w1_ref[kind, l * HEAD_DIM:(l + 1) * HEAD_DIM, :]
            wb = w1_ref[kind, (_CMP_HALF + l) * HEAD_DIM:(_CMP_HALF + l + 1) * HEAD_DIM, :]
            for g in range(NSA_GROUPS):
                kg = kind * NSA_GROUPS + g
                cols = slice(g * HEAD_DIM, (g + 1) * HEAD_DIM)
                first[kg] = first[kg] + _dot(xa[:, cols], wa)
                second[kg] = second[kg] + _dot(xb[:, cols], wb)
    rows = lax.broadcasted_iota(jnp.int32, (N_CMP_PAD, LANES), 0)
    lanes = lax.broadcasted_iota(jnp.int32, (LANES, N_CMP_PAD), 1)
    for g in range(NSA_GROUPS):
        acts = []
        for kind in range(2):
            kg = kind * NSA_GROUPS + g
            h1 = first[kg] + pltpu.roll(second[kg], N_CMP_PAD - 1, 0)
            acts.append(_gelu_tanh(h1).astype(BF16))
        kcv = _dot(acts[0], w2_ref[0]) + _dot(acts[1], w2_ref[1])
        kcvt = _dot_nt(w2t_ref[0], acts[0]) + _dot_nt(w2t_ref[1], acts[1])
        out_ref[0, g] = jnp.where(rows < n_cmp, kcv, 0.0).astype(BF16)
        outt_ref[0, g] = jnp.where(lanes < n_cmp, kcvt, 0.0).astype(BF16)


def _compress(cv, pos, w1, w2, w2t, batch, seq):
    n_cmp = (seq - CMP_LEN) // CMP_STRIDE + 1
    assert n_cmp < N_CMP_PAD and seq == N_CMP_PAD * CMP_STRIDE
    return pl.pallas_call(
        functools.partial(_compress_kernel, n_cmp=n_cmp),
        grid=(batch,),
        in_specs=[pl.BlockSpec((seq, LANES), lambda b: (b, 0)), pl.BlockSpec((seq, LANES), lambda b: (b, 1)),
                  _resident(pos.shape), _resident(w1.shape), _resident(w2.shape), _resident(w2t.shape)],
        out_specs=[pl.BlockSpec((1, NSA_GROUPS, N_CMP_PAD, LANES), lambda b: (b, 0, 0, 0)),
                   pl.BlockSpec((1, NSA_GROUPS, LANES, N_CMP_PAD), lambda b: (b, 0, 0, 0))],
        out_shape=[jax.ShapeDtypeStruct((batch, NSA_GROUPS, N_CMP_PAD, LANES), BF16),
                   jax.ShapeDtypeStruct((batch, NSA_GROUPS, LANES, N_CMP_PAD), BF16)],
        compiler_params=pltpu.CompilerParams(dimension_semantics=("parallel",), vmem_limit_bytes=VMEM_LIMIT),
        name="compress",
    )(cv, cv, pos, w1, w2, w2t)


N_SEL = 64


def _attn_kernel(qt_ref, kcv_ref, kcvt_ref, sel_ref, ka_ref, kw_ref, vst_ref, vwt_ref, gtt_ref, o_ref,
                 q0_ref, q1_ref, qs_ref, m_ref, acc_ref, s_ref, smax_ref, p_ref, alpha_ref, oc_ref, ow_ref):
    g = pl.program_id(1)
    qb = pl.program_id(2)
    t0 = qb * TQ
    cols = NSA_HPG * TQ

    zeros = jnp.zeros((HEAD_DIM, TQ), BF16)
    for h in range(NSA_HPG):
        qh = qt_ref[h * HEAD_DIM:(h + 1) * HEAD_DIM, :]
        cs = slice(h * TQ, (h + 1) * TQ)
        q0_ref[0:HEAD_DIM, cs] = qh
        q0_ref[HEAD_DIM:, cs] = zeros
        q1_ref[0:HEAD_DIM, cs] = zeros
        q1_ref[HEAD_DIM:, cs] = qh
        qs_ref[0:HEAD_DIM, cs] = qh
    q0 = q0_ref[...]
    tq = t0 + (lax.broadcasted_iota(jnp.int32, (1, cols), 1) & (TQ - 1))

    sc = _dot(kcv_ref[0, 0], q0)
    wt0 = jnp.maximum(qb // (VT // TQ) - (WIN_TILES - 1), 0)
    kstart = pl.multiple_of(wt0 * VT, VT)
    span = WIN_TILES * VT
    qwin = jnp.where(g == 0, q0, q1_ref[...])
    sw = _dot(kw_ref[pl.ds(kstart, span), :], qwin)
    blocks0 = TK // SEL_LEN
    raw0 = _dot(ka_ref[0:TK, :], q0)
    s_ref[0] = raw0
    bmax0 = jnp.max(raw0.reshape(blocks0, SEL_LEN, cols), axis=1)

    tq1 = t0 + lax.broadcasted_iota(jnp.int32, (1, TQ), 1)

    def per_head(x):
        return jnp.concatenate([x] * NSA_HPG, axis=1)

    cend = lax.broadcasted_iota(jnp.int32, (N_CMP_PAD, 1), 0) * CMP_STRIDE + (CMP_LEN - 1)
    sc = sc + per_head(jnp.where(cend <= tq1, 0.0, NEG))
    ec = jnp.exp2(sc - jnp.max(sc, axis=0, keepdims=True))
    lc = jnp.sum(ec, axis=0, keepdims=True)
    visible = tq >= CMP_LEN - 1
    dist = tq1 - (kstart + lax.broadcasted_iota(jnp.int32, (span, 1), 0))
    sw = sw + per_head(jnp.where(dist.astype(jnp.uint32) < WINDOW, 0.0, NEG))
    wmax = jnp.max(sw, axis=0, keepdims=True)
    early = jnp.maximum(wmax, jnp.max(bmax0, axis=0, keepdims=True))
    pc = ec * (jnp.where(visible, 1.0 / lc, 0.0) + _zero_after(early))
    oc_ref[...] = _dot(kcvt_ref[0, 0], pc.astype(BF16))[0:HEAD_DIM]

    psum = pc[:, 0:TQ]
    for h in range(1, NSA_HPG):
        psum = psum + pc[:, h * TQ:(h + 1) * TQ]
    sel = sel_ref[...]
    p_hi = psum.astype(BF16)
    rem = psum - p_hi.astype(F32)
    p_mid = rem.astype(BF16)
    p_lo = (rem - p_mid.astype(F32)).astype(BF16)
    imp = (_dot(sel, p_hi) + _dot(sel, p_mid)) + _dot(sel, p_lo)

    pw = jnp.exp2(sw - wmax).astype(BF16)
    ow = _dot(vwt_ref[wt0], pw[0:VT])
    for c in range(1, WIN_TILES):
        ow = ow + _dot(vwt_ref[wt0 + c], pw[c * VT:(c + 1) * VT])
    ow_ref[...] = ow[0:HEAD_DIM] * (1.0 / ow[HEAD_DIM:HEAD_DIM + 1])

    j = lax.broadcasted_iota(jnp.int32, (N_SEL, TQ), 0)
    cur = (t0 + lax.broadcasted_iota(jnp.int32, (N_SEL, TQ), 1)) >> SEL_SHIFT

    forced = (j == 0) | (j == cur) | (j == cur - 1)
    score = jnp.where(j <= cur, jnp.where(forced, FORCE_SCORE, imp), NEG)
    blocks = [score[r * SUBLANES:(r + 1) * SUBLANES] for r in range(N_SEL // SUBLANES)]
    ranks = [jnp.zeros((SUBLANES, TQ), jnp.int32) for _ in blocks]
    jloc = lax.broadcasted_iota(jnp.int32, (SUBLANES, TQ), 0)
    for i in range(N_SEL):
        vi = jnp.broadcast_to(score[i:i + 1, :], (SUBLANES, TQ))
        for r, blk in enumerate(blocks):
            if r * SUBLANES > i:
                before = vi >= blk
            elif (r + 1) * SUBLANES - 1 <= i:
                before = vi > blk
            else:
                before = (vi > blk) | ((vi == blk) & (jloc > i - r * SUBLANES))
            ranks[r] = ranks[r] + before.astype(jnp.int32)
    bias = jnp.where(jnp.concatenate(ranks, axis=0) < SEL_TOP, 0.0, MASK_BIAS).astype(BF16)
    for h in range(NSA_HPG):
        qs_ref[HEAD_DIM:, h * TQ:(h + 1) * TQ] = bias
    bias0 = per_head(bias.astype(F32)[0:blocks0])
    s_ref[0] = (s_ref[0].reshape(blocks0, SEL_LEN, cols) + bias0[:, None, :]).reshape(TK, cols)
    smax_ref[0] = jnp.max(bmax0 + bias0, axis=0, keepdims=True)

    m_ref[...] = jnp.full(m_ref.shape, NEG, F32)
    acc_ref[...] = jnp.zeros(acc_ref.shape, F32)
    ksub = lax.broadcasted_iota(jnp.int32, (TK, 1), 0)
    even, odd = 0, 1

    def logits(kt, slot):
        k0 = pl.multiple_of(kt * TK, TK)
        s = _dot(ka_ref[pl.ds(k0, TK), :], qs_ref[...])
        s_ref[slot] = s
        smax_ref[slot] = jnp.max(s, axis=0, keepdims=True)

    def softmax(kt, slot, causal):
        s = s_ref[slot]
        if causal:
            s = jnp.where(kt * TK + ksub <= tq, s, NEG)
            smax = jnp.max(s, axis=0, keepdims=True)
        else:
            smax = smax_ref[slot]
        m_prev = m_ref[...]
        m_new = jnp.maximum(m_prev, smax)
        p_ref[slot] = jnp.exp2(s - m_new).astype(BF16)
        alpha_ref[slot] = jnp.exp2(m_prev - m_new)
        m_ref[...] = m_new

    def accumulate(kt, slot):
        p = p_ref[slot]
        v0 = jnp.maximum(kt, 0) * (TK // VT)
        pv = _dot(vst_ref[v0], p[0:VT])
        for c in range(1, TK // VT):
            pv = pv + _dot(vst_ref[v0 + c], p[c * VT:(c + 1) * VT])
        acc_ref[...] = alpha_ref[slot] * acc_ref[...] + pv

    p_ref[odd] = jnp.zeros(p_ref.shape[1:], BF16)
    alpha_ref[odd] = jnp.ones(alpha_ref.shape[1:], F32)
    last = qb // (TK // TQ)
    pairs = last // 2

    def body(jp, carry):
        a = 2 * jp
        accumulate(a - 1, odd)
        softmax(a, even, False)
        logits(a + 1, odd)
        accumulate(a, even)
        softmax(a + 1, odd, False)
        logits(a + 2, even)
        return carry

    lax.fori_loop(0, pairs, body, 0)
    a = 2 * pairs

    @pl.when(last == a)
    def _():
        accumulate(a - 1, odd)
        softmax(a, even, True)
        accumulate(a, even)

    @pl.when(last != a)
    def _():
        accumulate(a - 1, odd)
        softmax(a, even, False)
        logits(a + 1, odd)
        accumulate(a, even)
        softmax(a + 1, odd, True)
        accumulate(a + 1, odd)

    acc = acc_ref[...]
    o_s = acc[0:HEAD_DIM] * (1.0 / acc[HEAD_DIM:HEAD_DIM + 1])

    o_c = oc_ref[...]
    o_w = ow_ref[...]
    heads = []
    for h in range(NSA_HPG):
        cs = slice(h * TQ, (h + 1) * TQ)

        def gate(br):
            return gtt_ref[pl.ds(br * NSA_HEADS + g * NSA_HPG + h, 1), :]

        heads.append(gate(0) * o_c[:, cs] + gate(1) * o_s[:, cs] + gate(2) * o_w[:, cs])
    o_ref[...] = jnp.concatenate(heads, axis=0).T.astype(BF16)


def _attention(qt, kcv, kcvt, sel, ka, kw, vst, vwt, gtt, batch, seq):
    nq = seq // TQ
    cols = NSA_HPG * TQ
    assert seq // SEL_LEN == N_SEL and WIN_TILES * VT <= seq and TK % VT == 0 and VT % TQ == 0
    assert (WIN_TILES - 1) * VT >= WINDOW
    vt_spec = pl.BlockSpec((seq // VT, LANES, VT), lambda b, g, i: (b, g, 0))
    return pl.pallas_call(
        _attn_kernel,
        grid=(batch, NSA_GROUPS, nq),
        in_specs=[
            pl.BlockSpec((NSA_HPG * HEAD_DIM, TQ), lambda b, g, i: (g, b * nq + i)),
            pl.BlockSpec((1, 1, N_CMP_PAD, LANES), lambda b, g, i: (b, g, 0, 0)),
            pl.BlockSpec((1, 1, LANES, N_CMP_PAD), lambda b, g, i: (b, g, 0, 0)),
            pl.BlockSpec(sel.shape, lambda b, g, i: (0, 0)),
            pl.BlockSpec((seq, LANES), lambda b, g, i: (b, g)),
            pl.BlockSpec((seq, LANES), lambda b, g, i: (b, 0)),
            vt_spec, vt_spec,
            pl.BlockSpec((_ZGN, TQ), lambda b, g, i: (0, b * nq + i)),
        ],
        out_specs=pl.BlockSpec((TQ, NSA_HPG * HEAD_DIM), lambda b, g, i: (b * nq + i, g)),
        out_shape=jax.ShapeDtypeStruct((batch * seq, NSA_HEADS * HEAD_DIM), BF16),
        scratch_shapes=[
            pltpu.VMEM((LANES, cols), BF16),
            pltpu.VMEM((LANES, cols), BF16),
            pltpu.VMEM((LANES, cols), BF16),
            pltpu.VMEM((1, cols), F32),
            pltpu.VMEM((LANES, cols), F32),
            pltpu.VMEM((2, TK, cols), F32),
            pltpu.VMEM((2, 1, cols), F32),
            pltpu.VMEM((2, TK, cols), BF16),
            pltpu.VMEM((2, 1, cols), F32),
            pltpu.VMEM((HEAD_DIM, cols), F32),
            pltpu.VMEM((HEAD_DIM, cols), F32),
        ],
        compiler_params=pltpu.CompilerParams(dimension_semantics=("parallel", "parallel", "arbitrary"),
                                             vmem_limit_bytes=VMEM_LIMIT),
        name="nsa_attention",
    )(qt, kcv, kcvt, sel, ka, kw, vst, vwt, gtt)


def _merge_kernel(o_ref, c_ref, halo_ref, sg_ref, x_ref, wo_ref, cw_ref, cb_ref, lg_ref, lb_ref, wco_ref,
                  bco_ref, wout_ref, h_ref, cext_ref, cn_ref, *, tiles_per_seq):
    first = (pl.program_id(0) % tiles_per_seq) == 0
    cext_ref[0:CONV_HALO, :] = jnp.where(first, 0.0, halo_ref[...])
    cext_ref[CONV_HALO:, :] = c_ref[...]
    shift = CONV_HALO - (CONV_WIDTH - 1)
    for r in range(TM // CONV_ROWS):
        r0 = r * CONV_ROWS
        acc = jnp.broadcast_to(cb_ref[...], (CONV_ROWS, cb_ref.shape[-1]))
        for res in range(SUBLANES):
            taps = [k for k in range(CONV_WIDTH) if (shift + k) % SUBLANES == res]
            n_rows = CONV_ROWS + (SUBLANES if res else 0)
            part = None
            for k in taps:
                base = r0 + shift + k - res
                term = cw_ref[k:k + 1, :] * cext_ref[base:base + n_rows, :]
                part = term if part is None else part + term
            acc = acc + part[res:res + CONV_ROWS]
        xc = acc - jnp.mean(acc, axis=-1, keepdims=True)
        y = xc * lax.rsqrt(jnp.mean(xc * xc, axis=-1, keepdims=True) + EPS) * lg_ref[...] + lb_ref[...]
        cn_ref[r0:r0 + CONV_ROWS, :] = (y * _sigmoid(y)).astype(BF16)
    y_b = _dot(cn_ref[...], wco_ref[...]) + bco_ref[...]
    y_a = _dot(o_ref[...], wo_ref[...])
    d = y_a.shape[-1]
    mix = sg_ref[:, :d] * y_a + sg_ref[:, d:] * y_b
    h_ref[...] = x_ref[...] + _dot(mix.astype(BF16), wout_ref[...])


def _merge(o, c, sg, x2, wo, cw, cb, lg, lb, wco, bco, wout, seq):
    m, d = x2.shape
    ch = c.shape[-1]
    row = lambda n: pl.BlockSpec((TM, n), lambda i: (i, 0))
    halo = pl.BlockSpec((CONV_HALO, ch), lambda i: (jnp.maximum(i * (TM // CONV_HALO) - 1, 0), 0))
    weights = [wo, cw, cb, lg, lb, wco, bco, wout]
    return pl.pallas_call(
        functools.partial(_merge_kernel, tiles_per_seq=seq // TM),
        grid=(m // TM,),
        in_specs=[row(o.shape[-1]), row(ch), halo, row(sg.shape[-1]), row(d)] + [_resident(w.shape) for w in weights],
        out_specs=row(d),
        out_shape=jax.ShapeDtypeStruct((m, d), F32),
        scratch_shapes=[pltpu.VMEM((CONV_HALO + TM, ch), F32), pltpu.VMEM((TM, ch), BF16)],
        compiler_params=pltpu.CompilerParams(dimension_semantics=("parallel",), vmem_limit_bytes=VMEM_LIMIT),
        name="merge",
    )(o, c, c, sg, x2, *weights)


def _ffn_kernel(h_ref, halo_ref, p_ref, gf_ref, wup_ref, fw_ref, fb_ref, wdn_ref, gp_ref, wpg_ref, wple_ref,
                gfin_ref, out_ref, u_ref, vg_ref, vv_ref, *, tiles_per_seq, d_ff, final_norm):
    first = (pl.program_id(0) % tiles_per_seq) == 0
    h = h_ref[...]
    u_ref[0:FFN_HALO, :] = _rms(jnp.where(first, 0.0, halo_ref[...]), gf_ref[...]).astype(BF16)
    u_ref[FFN_HALO:, :] = _rms(h, gf_ref[...]).astype(BF16)
    u = u_ref[...]
    shift = FFN_HALO - (FFN_CONV_WIDTH - 1)

    def conv(v_ref, slot, c0):
        acc = fb_ref[:, c0:c0 + FF_CHUNK]
        for k in range(FFN_CONV_WIDTH):
            acc = acc + fw_ref[k:k + 1, c0:c0 + FF_CHUNK] * v_ref[slot, shift + k:shift + k + TM, :]
        return acc

    def up(jc):
        c0 = jc * FF_CHUNK
        vg_ref[jc % 2] = _dot(u, wup_ref[:, c0:c0 + FF_CHUNK])
        vv_ref[jc % 2] = _dot(u, wup_ref[:, d_ff + c0:d_ff + c0 + FF_CHUNK])

    n_chunks = d_ff // FF_CHUNK
    acc = jnp.zeros(h.shape, F32)
    up(0)
    for jc in range(n_chunks):
        c0 = jc * FF_CHUNK
        if jc + 1 < n_chunks:
            up(jc + 1)
        a = _gelu_tanh(conv(vg_ref, jc % 2, c0)) * conv(vv_ref, jc % 2, d_ff + c0)
        acc = acc + _dot(a.astype(BF16), wdn_ref[c0:c0 + FF_CHUNK, :])
    h = h + acc
    gate = _sigmoid(_dot(_rms(h, gp_ref[...]).astype(BF16), wpg_ref[...]))
    h = h + gate * _dot(p_ref[...].astype(BF16), wple_ref[...])
    out_ref[...] = _rms(h, gfin_ref[...]) if final_norm else h


def _ffn(h1, p2, gf, wup, fw, fb, wdn, gp, wpg, wple, gfin, seq, final_norm):
    m, d = h1.shape
    d_ff = wdn.shape[0]
    assert d_ff % FF_CHUNK == 0
    row = lambda n: pl.BlockSpec((TM, n), lambda i: (i, 0))
    halo = pl.BlockSpec((FFN_HALO, d), lambda i: (jnp.maximum(i * (TM // FFN_HALO) - 1, 0), 0))
    weights = [gf, wup, fw, fb, wdn, gp, wpg, wple, gfin]
    return pl.pallas_call(
        functools.partial(_ffn_kernel, tiles_per_seq=seq // TM, d_ff=d_ff, final_norm=final_norm),
        grid=(m // TM,),
        in_specs=[row(d), halo, row(p2.shape[-1])] + [_resident(w.shape) for w in weights],
        out_specs=row(d),
        out_shape=jax.ShapeDtypeStruct((m, d), F32),
        scratch_shapes=[pltpu.VMEM((FFN_HALO + TM, d), BF16), pltpu.VMEM((2, FFN_HALO + TM, FF_CHUNK), F32),
                        pltpu.VMEM((2, FFN_HALO + TM, FF_CHUNK), F32)],
        compiler_params=pltpu.CompilerParams(dimension_semantics=("parallel",), vmem_limit_bytes=VMEM_LIMIT),
        name="ffn_ple",
    )(h1, h1, p2, *weights)


def _sel_map_t(seq):
    n_cmp = (seq - CMP_LEN) // CMP_STRIDE + 1
    n_sel = seq // SEL_LEN
    c0 = np.arange(n_cmp) * CMP_STRIDE
    j0 = np.arange(n_sel) * SEL_LEN
    lo = np.maximum(c0[None, :], j0[:, None])
    hi = np.minimum(c0[None, :] + CMP_LEN, j0[:, None] + SEL_LEN)
    out = np.zeros((N_SEL, N_CMP_PAD), np.float32)
    out[:n_sel, :n_cmp] = np.maximum(hi - lo, 0) / CMP_LEN
    return jnp.asarray(out, BF16)


def _inproj_weights(w):
    d = w.shape[0]
    nq = NSA_HEADS * HEAD_DIM
    kv0 = nq
    gate0 = kv0 + 3 * 2 * NSA_GROUPS * HEAD_DIM
    conv0 = gate0 + 3 * NSA_HEADS
    merge0 = conv0 + _ZCN

    def kv(br, g):
        lo = kv0 + (br * NSA_GROUPS + g) * HEAD_DIM
        return w[:, lo:lo + HEAD_DIM]

    z = jnp.zeros((d, HEAD_DIM), w.dtype)
    tok = [kv(2, 0), z, kv(2, 1), z, kv(4, 0), kv(4, 1), w[:, kv0:kv0 + _CVN], w[:, conv0:merge0],
           w[:, merge0:merge0 + _ZMN]]
    feat = [w[:, :nq] * Q_SCALE, kv(3, 0), kv(3, 1), kv(5, 0), kv(5, 1), w[:, gate0:conv0],
            jnp.zeros((d, _ZGN - 3 * NSA_HEADS), w.dtype)]
    w_tok = jnp.concatenate(tok, axis=1).astype(BF16)
    w_feat = jnp.concatenate(feat, axis=1).T.astype(BF16)
    assert w_tok.shape[1] == _N_TOK and w_feat.shape[0] == _N_FEAT
    return w_tok, w_feat


def kernel(x, p, g_mix, w_in, cmp_pos_k, cmp_pos_v, w_cmp_k1, w_cmp_k2, w_cmp_v1, w_cmp_v2, w_o_nsa, conv_w,
           conv_b, conv_ln_g, conv_ln_b, w_conv_out, b_conv_out, w_out, g_ffn, w_up, ffn_conv_w, ffn_conv_b,
           w_down, g_ple, w_ple_gate, w_ple, g_final):
    batch, seq, d = x.shape
    depth = w_in.shape[0]
    m = batch * seq
    assert seq % TM == 0 and seq % TK == 0
    sel = _sel_map_t(seq)
    row = lambda v: v.reshape(1, -1)
    h = x.reshape(m, d)
    for i in range(depth):
        w_tok, w_feat = _inproj_weights(w_in[i])
        pos = jnp.concatenate([cmp_pos_k[i]] * NSA_GROUPS + [cmp_pos_v[i]] * NSA_GROUPS, axis=1)
        w1 = jnp.stack([w_cmp_k1[i], w_cmp_v1[i]]).astype(BF16)
        zpad = jnp.zeros_like(w_cmp_k2[i])
        w2 = jnp.stack([jnp.concatenate([w_cmp_k2[i], zpad], axis=1),
                        jnp.concatenate([zpad, w_cmp_v2[i]], axis=1)]).astype(BF16)
        w2t = jnp.stack([jnp.concatenate([zpad, w_cmp_k2[i]], axis=1).T,
                         jnp.concatenate([w_cmp_v2[i], zpad], axis=1).T]).astype(BF16)

        qt, ka, kw, cv, c, sg, vst, vwt, gtt = _inproj(h, row(g_mix[i]), w_tok, w_feat, seq)
        kcv, kcvt = _compress(cv, pos, w1, w2, w2t, batch, seq)
        o = _attention(qt, kcv, kcvt, sel, ka, kw, vst, vwt, gtt, batch, seq)
        h = _merge(o, c, sg, h, w_o_nsa[i].astype(BF16), conv_w[i], row(conv_b[i]), row(conv_ln_g[i]),
                   row(conv_ln_b[i]), w_conv_out[i].astype(BF16), row(b_conv_out[i]), w_out[i].astype(BF16), seq)
        h = _ffn(h, p[i].reshape(m, -1), row(g_ffn[i]), w_up[i].astype(BF16), ffn_conv_w[i], row(ffn_conv_b[i]),
                 w_down[i].astype(BF16), row(g_ple[i]), w_ple_gate[i].astype(BF16), w_ple[i].astype(BF16),
                 row(g_final), seq, final_norm=(i == depth - 1))
    return h.reshape(batch, seq, d)
```

```python
import functools

import numpy as np
import jax
import jax.numpy as jnp
from jax import lax
from jax.experimental import pallas as pl
from jax.experimental.pallas import tpu as pltpu

F32 = jnp.float32
BF16 = jnp.bfloat16

NSA_HEADS = 8
NSA_GROUPS = 2
NSA_HPG = NSA_HEADS // NSA_GROUPS
HEAD_DIM = 64
CMP_LEN = 32
CMP_STRIDE = 16
SEL_LEN = 64
SEL_SHIFT = 6
SEL_TOP = 16
WINDOW = 512
FORCE_SCORE = 1e4
CONV_WIDTH = 31
FFN_CONV_WIDTH = 3
EPS = 1e-6
NEG = -1e30
MASK_BIAS = -1e9
Q_SCALE = HEAD_DIM ** -0.5 * float(np.log2(np.e))

LANES = 128
SUBLANES = 8
BF16_ROWS = 16
VMEM_LIMIT = 56 * 1024 * 1024

TM = 512
TQ = 256
TK = 512
VT = 256
WIN_TILES = (WINDOW + TQ) // VT + (1 if TQ % VT else 0)
CONV_HALO = 32
FFN_HALO = BF16_ROWS
CONV_ROWS = 64
FF_CHUNK = 1408


def _sigmoid(x):
    return 0.5 * (jnp.tanh(0.5 * x) + 1.0)


def _gelu_tanh(x):
    return 0.5 * x * (1.0 + jnp.tanh(np.sqrt(2.0 / np.pi).astype(np.float32) * (x + 0.044715 * (x * x * x))))


def _rms(x, g):
    return x * lax.rsqrt(jnp.mean(x * x, axis=-1, keepdims=True) + EPS) * g


def _zero_after(x):
    bits = lax.bitcast_convert_type(x, jnp.uint32)
    bits = lax.shift_right_logical(lax.shift_right_logical(bits, jnp.uint32(16)), jnp.uint32(16))
    return lax.bitcast_convert_type(bits, F32)


def _dot(a, b):
    return jnp.dot(a, b, preferred_element_type=F32)


def _dot_nt(a, b):
    return lax.dot_general(a, b, (((1,), (1,)), ((), ())), preferred_element_type=F32)


def _resident(shape):
    nd = len(shape)
    return pl.BlockSpec(shape, lambda *_: (0,) * nd, pipeline_mode=pl.Buffered(1))


_KA0, _KAN = 0, 256
_KW0, _KWN = 256, 128
_CV0, _CVN = 384, 256
_ZC0, _ZCN = 640, 1024
_ZM0, _ZMN = 1664, 2048
_N_TOK = 3712
_Q0, _QN = 0, 512
_VS0, _VSN = 512, 128
_VW0, _VWN = 640, 128
_ZG0, _ZGN = 768, 32
_N_FEAT = 800


def _inproj_kernel(x_ref, g_ref, wt_ref, wf_ref, qt_ref, ka_ref, kw_ref, cv_ref, c_ref, sg_ref, vst_ref, vwt_ref,
                   gtt_ref, *, tiles_per_seq):
    u = _rms(x_ref[...], g_ref[...]).astype(BF16)

    def tok(lo, n):
        return _dot(u, wt_ref[:, lo:lo + n])

    def feat(lo, n):
        return _dot_nt(wf_ref[lo:lo + n, :], u)

    ka = tok(_KA0, _KAN)
    s0 = (pl.program_id(0) % tiles_per_seq) * TM
    row = lax.broadcasted_iota(jnp.int32, ka.shape, 0) + s0
    col = lax.broadcasted_iota(jnp.int32, ka.shape, 1)
    onehot = ((col & HEAD_DIM) != 0) & ((col & (HEAD_DIM - 1)) == (row >> SEL_SHIFT))
    ka_ref[...] = jnp.where(onehot, 1.0, ka).astype(BF16)
    kw_ref[...] = tok(_KW0, _KWN).astype(BF16)
    cv_ref[...] = tok(_CV0, _CVN)
    zc = tok(_ZC0, _ZCN)
    half = _ZCN // 2
    c_ref[...] = zc[:, :half] * _sigmoid(zc[:, half:])
    sg_ref[...] = _sigmoid(tok(_ZM0, _ZMN))

    qt_ref[...] = feat(_Q0, _QN).astype(BF16)
    gtt_ref[...] = _sigmoid(feat(_ZG0, _ZGN))
    ones = jnp.ones((HEAD_DIM, VT), BF16)
    for src0, dst_ref in ((_VS0, vst_ref), (_VW0, vwt_ref)):
        v = feat(src0, NSA_GROUPS * HEAD_DIM).astype(BF16)
        for t in range(TM // VT):
            for g in range(NSA_GROUPS):
                r0 = g * 2 * HEAD_DIM
                dst_ref[t, r0:r0 + HEAD_DIM, :] = v[g * HEAD_DIM:(g + 1) * HEAD_DIM, t * VT:(t + 1) * VT]
                dst_ref[t, r0 + HEAD_DIM:r0 + 2 * HEAD_DIM, :] = ones


def _inproj(x2, g_mix, w_tok, w_feat, seq):
    m, d = x2.shape
    row = lambda n: pl.BlockSpec((TM, n), lambda i: (i, 0))
    colb = lambda n: pl.BlockSpec((n, TM), lambda i: (0, i))
    vt_spec = pl.BlockSpec((TM // VT, NSA_GROUPS * LANES, VT), lambda i: (i, 0, 0))
    vt_shape = jax.ShapeDtypeStruct((m // VT, NSA_GROUPS * LANES, VT), BF16)
    tok_outs = [(_KAN, BF16), (_KWN, BF16), (_CVN, F32), (_ZCN // 2, F32), (_ZMN, F32)]
    return pl.pallas_call(
        functools.partial(_inproj_kernel, tiles_per_seq=seq // TM),
        grid=(m // TM,),
        in_specs=[row(d), _resident((1, d)), _resident(w_tok.shape), _resident(w_feat.shape)],
        out_specs=[colb(_QN)] + [row(n) for n, _ in tok_outs] + [vt_spec, vt_spec, colb(_ZGN)],
        out_shape=([jax.ShapeDtypeStruct((_QN, m), BF16)]
                   + [jax.ShapeDtypeStruct((m, n), dt) for n, dt in tok_outs]
                   + [vt_shape, vt_shape, jax.ShapeDtypeStruct((_ZGN, m), F32)]),
        compiler_params=pltpu.CompilerParams(dimension_semantics=("parallel",), vmem_limit_bytes=VMEM_LIMIT),
        name="inproj",
    )(x2, g_mix, w_tok, w_feat)


N_CMP_PAD = 256
_CMP_HALF = CMP_LEN // 2


def _compress_kernel(ck_ref, cv_ref, pos_ref, w1_ref, w2_ref, w2t_ref, out_ref, outt_ref, *, n_cmp):
    hid = w1_ref.shape[-1]
    first = [jnp.zeros((N_CMP_PAD, hid), F32) for _ in range(4)]
    second = [jnp.zeros((N_CMP_PAD, hid), F32) for _ in range(4)]
    for l in range(_CMP_HALF):
        for kind, src_ref in enumerate((ck_ref, cv_ref)):
            xl = src_ref[pl.ds(l, N_CMP_PAD, stride=CMP_STRIDE), :]
            pcol = slice(kind * LANES, (kind + 1) * LANES)
            xa = (xl + pos_ref[l:l + 1, pcol]).astype(BF16)
            xb = (xl + pos_ref[_CMP_HALF + l:_CMP_HALF + l + 1, pcol]).astype(BF16)
            wa = w1_ref[kind, l * HEAD_DIM:(l + 1) * HEAD_DIM, :]
            wb = w1_ref[kind, (_CMP_HALF + l) * HEAD_DIM:(_CMP_HALF + l + 1) * HEAD_DIM, :]
            for g in range(NSA_GROUPS):
                kg = kind * NSA_GROUPS + g
                cols = slice(g * HEAD_DIM, (g + 1) * HEAD_DIM)
                first[kg] = first[kg] + _dot(xa[:, cols], wa)
                second[kg] = second[kg] + _dot(xb[:, cols], wb)
    rows = lax.broadcasted_iota(jnp.int32, (N_CMP_PAD, LANES), 0)
    lanes = lax.broadcasted_iota(jnp.int32, (LANES, N_CMP_PAD), 1)
    for g in range(NSA_GROUPS):
        acts = []
        for kind in range(2):
            kg = kind * NSA_GROUPS + g
            h1 = first[kg] + pltpu.roll(second[kg], N_CMP_PAD - 1, 0)
            acts.append(_gelu_tanh(h1).astype(BF16))
        kcv = _dot(acts[0], w2_ref[0]) + _dot(acts[1], w2_ref[1])
        kcvt = _dot_nt(w2t_ref[0], acts[0]) + _dot_nt(w2t_ref[1], acts[1])
        out_ref[0, g] = jnp.where(rows < n_cmp, kcv, 0.0).astype(BF16)
        outt_ref[0, g] = jnp.where(lanes < n_cmp, kcvt, 0.0).astype(BF16)


def _compress(cv, pos, w1, w2, w2t, batch, seq):
    n_cmp = (seq - CMP_LEN) // CMP_STRIDE + 1
    assert n_cmp < N_CMP_PAD and seq == N_CMP_PAD * CMP_STRIDE
    return pl.pallas_call(
        functools.partial(_compress_kernel, n_cmp=n_cmp),
        grid=(batch,),
        in_specs=[pl.BlockSpec((seq, LANES), lambda b: (b, 0)), pl.BlockSpec((seq, LANES), lambda b: (b, 1)),
                  _resident(pos.shape), _resident(w1.shape), _resident(w2.shape), _resident(w2t.shape)],
        out_specs=[pl.BlockSpec((1, NSA_GROUPS, N_CMP_PAD, LANES), lambda b: (b, 0, 0, 0)),
                   pl.BlockSpec((1, NSA_GROUPS, LANES, N_CMP_PAD), lambda b: (b, 0, 0, 0))],
        out_shape=[jax.ShapeDtypeStruct((batch, NSA_GROUPS, N_CMP_PAD, LANES), BF16),
                   jax.ShapeDtypeStruct((batch, NSA_GROUPS, LANES, N_CMP_PAD), BF16)],
        compiler_params=pltpu.CompilerParams(dimension_semantics=("parallel",), vmem_limit_bytes=VMEM_LIMIT),
        name="compress",
    )(cv, cv, pos, w1, w2, w2t)


N_SEL = 64


def _attn_kernel(qt_ref, kcv_ref, kcvt_ref, sel_ref, ka_ref, kw_ref, vst_ref, vwt_ref, gtt_ref, o_ref,
                 q0_ref, q1_ref, qs_ref, m_ref, acc_ref, s_ref, smax_ref, p_ref, alpha_ref, oc_ref, ow_ref):
    g = pl.program_id(1)
    qb = pl.program_id(2)
    t0 = qb * TQ
    cols = NSA_HPG * TQ

    zeros = jnp.zeros((HEAD_DIM, TQ), BF16)
    for h in range(NSA_HPG):
        qh = qt_ref[h * HEAD_DIM:(h + 1) * HEAD_DIM, :]
        cs = slice(h * TQ, (h + 1) * TQ)
        q0_ref[0:HEAD_DIM, cs] = qh
        q0_ref[HEAD_DIM:, cs] = zeros
        q1_ref[0:HEAD_DIM, cs] = zeros
        q1_ref[HEAD_DIM:, cs] = qh
        qs_ref[0:HEAD_DIM, cs] = qh
        qs_ref[HEAD_DIM:, cs] = zeros
    q0 = q0_ref[...]
    tq = t0 + (lax.broadcasted_iota(jnp.int32, (1, cols), 1) & (TQ - 1))

    sc = _dot(kcv_ref[0, 0], q0)
    wt0 = jnp.maximum(t0 - WINDOW, 0) // VT
    kstart = pl.multiple_of(wt0 * VT, VT)
    span = WIN_TILES * VT
    qwin = jnp.where(g == 0, q0, q1_ref[...])
    sw = _dot(kw_ref[pl.ds(kstart, span), :], qwin)

    tq1 = t0 + lax.broadcasted_iota(jnp.int32, (1, TQ), 1)

    def per_head(x):
        return jnp.concatenate([x] * NSA_HPG, axis=1)

    cend = lax.broadcasted_iota(jnp.int32, (N_CMP_PAD, 1), 0) * CMP_STRIDE + (CMP_LEN - 1)
    sc = sc + per_head(jnp.where(cend <= tq1, 0.0, NEG))
    ec = jnp.exp2(sc - jnp.max(sc, axis=0, keepdims=True))
    lc = jnp.sum(ec, axis=0, keepdims=True)
    visible = tq >= CMP_LEN - 1
    dist = tq1 - (kstart + lax.broadcasted_iota(jnp.int32, (span, 1), 0))
    sw = sw + per_head(jnp.where(dist.astype(jnp.uint32) < WINDOW, 0.0, NEG))
    wmax = jnp.max(sw, axis=0, keepdims=True)
    pc = ec * (jnp.where(visible, 1.0 / lc, 0.0) + _zero_after(wmax))
    oc_ref[...] = _dot(kcvt_ref[0, 0], pc.astype(BF16))[0:HEAD_DIM]

    psum = pc[:, 0:TQ]
    for h in range(1, NSA_HPG):
        psum = psum + pc[:, h * TQ:(h + 1) * TQ]
    sel = sel_ref[...]
    p_hi = psum.astype(BF16)
    rem = psum - p_hi.astype(F32)
    p_mid = rem.astype(BF16)
    p_lo = (rem - p_mid.astype(F32)).astype(BF16)
    imp = (_dot(sel, p_hi) + _dot(sel, p_mid)) + _dot(sel, p_lo)

    pw = jnp.exp2(sw - wmax).astype(BF16)
    ow = _dot(vwt_ref[wt0], pw[0:VT])
    for c in range(1, WIN_TILES):
        ow = ow + _dot(vwt_ref[wt0 + c], pw[c * VT:(c + 1) * VT])
    ow_ref[...] = ow[0:HEAD_DIM] * (1.0 / ow[HEAD_DIM:HEAD_DIM + 1])

    j = lax.broadcasted_iota(jnp.int32, (N_SEL, TQ), 0)
    cur = (t0 + lax.broadcasted_iota(jnp.int32, (N_SEL, TQ), 1)) >> SEL_SHIFT

    forced = (j == 0) | (j == cur) | (j == cur - 1)
    score = jnp.where(j <= cur, jnp.where(forced, FORCE_SCORE, imp), NEG)
    jloc = lax.broadcasted_iota(jnp.int32, (SUBLANES, TQ), 0)

    def set_bias(row_blocks, score):
        ranks = []
        for r in row_blocks:
            blk = score[r * SUBLANES:(r + 1) * SUBLANES]
            rank = jnp.zeros((SUBLANES, TQ), jnp.int32)
            for i in range(N_SEL):
                vi = jnp.broadcast_to(score[i:i + 1, :], (SUBLANES, TQ))
                if r * SUBLANES > i:
                    before = vi >= blk
                elif (r + 1) * SUBLANES - 1 <= i:
                    before = vi > blk
                else:
                    before = (vi > blk) | ((vi == blk) & (jloc > i - r * SUBLANES))
                rank = rank + before.astype(jnp.int32)
            ranks.append(rank)
        bias = jnp.where(jnp.concatenate(ranks, axis=0) < SEL_TOP, 0.0, MASK_BIAS).astype(BF16)
        lo = HEAD_DIM + row_blocks[0] * SUBLANES
        for h in range(NSA_HPG):
            qs_ref[lo:lo + bias.shape[0], h * TQ:(h + 1) * TQ] = bias

    m_ref[...] = jnp.full(m_ref.shape, NEG, F32)
    acc_ref[...] = jnp.zeros(acc_ref.shape, F32)
    ksub = lax.broadcasted_iota(jnp.int32, (TK, 1), 0)
    even, odd = 0, 1

    def logits(kt, slot):
        k0 = pl.multiple_of(kt * TK, TK)
        s = _dot(ka_ref[pl.ds(k0, TK), :], qs_ref[...])
        s_ref[slot] = s
        smax_ref[slot] = jnp.max(s, axis=0, keepdims=True)
        return s[0:SUBLANES, 0:TQ]

    def softmax(kt, slot, causal):
        s = s_ref[slot]
        if causal:
            s = jnp.where(kt * TK + ksub <= tq, s, NEG)
            smax = jnp.max(s, axis=0, keepdims=True)
        else:
            smax = smax_ref[slot]
        m_prev = m_ref[...]
        m_new = jnp.maximum(m_prev, smax)
        p_ref[slot] = jnp.exp2(s - m_new).astype(BF16)
        alpha_ref[slot] = jnp.exp2(m_prev - m_new)
        m_ref[...] = m_new

    def accumulate(kt, slot):
        p = p_ref[slot]
        v0 = jnp.maximum(kt, 0) * (TK // VT)
        pv = _dot(vst_ref[v0], p[0:VT])
        for c in range(1, TK // VT):
            pv = pv + _dot(vst_ref[v0 + c], p[c * VT:(c + 1) * VT])
        acc_ref[...] = alpha_ref[slot] * acc_ref[...] + pv

    p_ref[odd] = jnp.zeros(p_ref.shape[1:], BF16)
    alpha_ref[odd] = jnp.ones(alpha_ref.shape[1:], F32)
    last = qb // (TK // TQ)
    pairs = last // 2
    n_row_blocks = N_SEL // SUBLANES
    first_blocks = max(TK // SEL_LEN, BF16_ROWS) // SUBLANES
    set_bias(list(range(first_blocks)), score)
    started = logits(0, even)
    set_bias(list(range(first_blocks, n_row_blocks)), score + _zero_after(jnp.concatenate([started] * n_row_blocks, axis=0)))

    def body(jp, carry):
        a = 2 * jp
        accumulate(a - 1, odd)
        softmax(a, even, False)
        logits(a + 1, odd)
        accumulate(a, even)
        softmax(a + 1, odd, False)
        logits(a + 2, even)
        return carry

    lax.fori_loop(0, pairs, body, 0)
    a = 2 * pairs

    @pl.when(last == a)
    def _():
        accumulate(a - 1, odd)
        softmax(a, even, True)
        accumulate(a, even)

    @pl.when(last != a)
    def _():
        accumulate(a - 1, odd)
        softmax(a, even, False)
        logits(a + 1, odd)
        accumulate(a, even)
        softmax(a + 1, odd, True)
        accumulate(a + 1, odd)

    acc = acc_ref[...]
    o_s = acc[0:HEAD_DIM] * (1.0 / acc[HEAD_DIM:HEAD_DIM + 1])

    o_c = oc_ref[...]
    o_w = ow_ref[...]
    heads = []
    for h in range(NSA_HPG):
        cs = slice(h * TQ, (h + 1) * TQ)

        def gate(br):
            return gtt_ref[pl.ds(br * NSA_HEADS + g * NSA_HPG + h, 1), :]

        heads.append(gate(0) * o_c[:, cs] + gate(1) * o_s[:, cs] + gate(2) * o_w[:, cs])
    o_ref[...] = jnp.concatenate(heads, axis=0).T.astype(BF16)


def _attention(qt, kcv, kcvt, sel, ka, kw, vst, vwt, gtt, batch, seq):
    nq = seq // TQ
    cols = NSA_HPG * TQ
    assert seq // SEL_LEN == N_SEL and WIN_TILES * VT <= seq and TK % VT == 0 and TK % TQ == 0
    assert WINDOW % VT == 0 and (TQ % VT == 0 or VT % TQ == 0)
    vt_spec = pl.BlockSpec((seq // VT, LANES, VT), lambda b, g, i: (b, g, 0))
    return pl.pallas_call(
        _attn_kernel,
        grid=(batch, NSA_GROUPS, nq),
        in_specs=[
            pl.BlockSpec((NSA_HPG * HEAD_DIM, TQ), lambda b, g, i: (g, b * nq + i)),
            pl.BlockSpec((1, 1, N_CMP_PAD, LANES), lambda b, g, i: (b, g, 0, 0)),
            pl.BlockSpec((1, 1, LANES, N_CMP_PAD), lambda b, g, i: (b, g, 0, 0)),
            pl.BlockSpec(sel.shape, lambda b, g, i: (0, 0)),
            pl.BlockSpec((seq, LANES), lambda b, g, i: (b, g)),
            pl.BlockSpec((seq, LANES), lambda b, g, i: (b, 0)),
            vt_spec, vt_spec,
            pl.BlockSpec((_ZGN, TQ), lambda b, g, i: (0, b * nq + i)),
        ],
        out_specs=pl.BlockSpec((TQ, NSA_HPG * HEAD_DIM), lambda b, g, i: (b * nq + i, g)),
        out_shape=jax.ShapeDtypeStruct((batch * seq, NSA_HEADS * HEAD_DIM), BF16),
        scratch_shapes=[
            pltpu.VMEM((LANES, cols), BF16),
            pltpu.VMEM((LANES, cols), BF16),
            pltpu.VMEM((LANES, cols), BF16),
            pltpu.VMEM((1, cols), F32),
            pltpu.VMEM((LANES, cols), F32),
            pltpu.VMEM((2, TK, cols), F32),
            pltpu.VMEM((2, 1, cols), F32),
            pltpu.VMEM((2, TK, cols), BF16),
            pltpu.VMEM((2, 1, cols), F32),
            pltpu.VMEM((HEAD_DIM, cols), F32),
            pltpu.VMEM((HEAD_DIM, cols), F32),
        ],
        compiler_params=pltpu.CompilerParams(dimension_semantics=("parallel", "parallel", "arbitrary"),
                                             vmem_limit_bytes=VMEM_LIMIT),
        name="nsa_attention",
    )(qt, kcv, kcvt, sel, ka, kw, vst, vwt, gtt)


def _merge_kernel(o_ref, c_ref, halo_ref, sg_ref, x_ref, wo_ref, cw_ref, cb_ref, lg_ref, lb_ref, wco_ref,
                  bco_ref, wout_ref, h_ref, cext_ref, cn_ref, *, tiles_per_seq):
    first = (pl.program_id(0) % tiles_per_seq) == 0
    cext_ref[0:CONV_HALO, :] = jnp.where(first, 0.0, halo_ref[...])
    cext_ref[CONV_HALO:, :] = c_ref[...]
    shift = CONV_HALO - (CONV_WIDTH - 1)
    for r in range(TM // CONV_ROWS):
        r0 = r * CONV_ROWS
        acc = jnp.broadcast_to(cb_ref[...], (CONV_ROWS, cb_ref.shape[-1]))
        for res in range(SUBLANES):
            taps = [k for k in range(CONV_WIDTH) if (shift + k) % SUBLANES == res]
            n_rows = CONV_ROWS + (SUBLANES if res else 0)
            part = None
            for k in taps:
                base = r0 + shift + k - res
                term = cw_ref[k:k + 1, :] * cext_ref[base:base + n_rows, :]
                part = term if part is None else part + term
            acc = acc + part[res:res + CONV_ROWS]
        xc = acc - jnp.mean(acc, axis=-1, keepdims=True)
        y = xc * lax.rsqrt(jnp.mean(xc * xc, axis=-1, keepdims=True) + EPS) * lg_ref[...] + lb_ref[...]
        cn_ref[r0:r0 + CONV_ROWS, :] = (y * _sigmoid(y)).astype(BF16)
    y_b = _dot(cn_ref[...], wco_ref[...]) + bco_ref[...]
    y_a = _dot(o_ref[...], wo_ref[...])
    d = y_a.shape[-1]
    mix = sg_ref[:, :d] * y_a + sg_ref[:, d:] * y_b
    h_ref[...] = x_ref[...] + _dot(mix.astype(BF16), wout_ref[...])


def _merge(o, c, sg, x2, wo, cw, cb, lg, lb, wco, bco, wout, seq):
    m, d = x2.shape
    ch = c.shape[-1]
    row = lambda n: pl.BlockSpec((TM, n), lambda i: (i, 0))
    halo = pl.BlockSpec((CONV_HALO, ch), lambda i: (jnp.maximum(i * (TM // CONV_HALO) - 1, 0), 0))
    weights = [wo, cw, cb, lg, lb, wco, bco, wout]
    return pl.pallas_call(
        functools.partial(_merge_kernel, tiles_per_seq=seq // TM),
        grid=(m // TM,),
        in_specs=[row(o.shape[-1]), row(ch), halo, row(sg.shape[-1]), row(d)] + [_resident(w.shape) for w in weights],
        out_specs=row(d),
        out_shape=jax.ShapeDtypeStruct((m, d), F32),
        scratch_shapes=[pltpu.VMEM((CONV_HALO + TM, ch), F32), pltpu.VMEM((TM, ch), BF16)],
        compiler_params=pltpu.CompilerParams(dimension_semantics=("parallel",), vmem_limit_bytes=VMEM_LIMIT),
        name="merge",
    )(o, c, c, sg, x2, *weights)


def _ffn_kernel(h_ref, halo_ref, p_ref, gf_ref, wup_ref, fw_ref, fb_ref, wdn_ref, gp_ref, wpg_ref, wple_ref,
                gfin_ref, out_ref, u_ref, vg_ref, vv_ref, *, tiles_per_seq, d_ff, final_norm):
    first = (pl.program_id(0) % tiles_per_seq) == 0
    h = h_ref[...]
    u_ref[0:FFN_HALO, :] = _rms(jnp.where(first, 0.0, halo_ref[...]), gf_ref[...]).astype(BF16)
    u_ref[FFN_HALO:, :] = _rms(h, gf_ref[...]).astype(BF16)
    u = u_ref[...]
    shift = FFN_HALO - (FFN_CONV_WIDTH - 1)

    starts = list(range(0, d_ff, FF_CHUNK))
    chunks = [(c0, min(FF_CHUNK, d_ff - c0)) for c0 in starts]

    def conv(v_ref, slot, c0, width):
        acc = fb_ref[:, c0:c0 + width]
        for k in range(FFN_CONV_WIDTH):
            acc = acc + fw_ref[k:k + 1, c0:c0 + width] * v_ref[slot, shift + k:shift + k + TM, 0:width]
        return acc

    def up(jc):
        c0, width = chunks[jc]
        vg_ref[jc % 2, :, 0:width] = _dot(u, wup_ref[:, c0:c0 + width])
        vv_ref[jc % 2, :, 0:width] = _dot(u, wup_ref[:, d_ff + c0:d_ff + c0 + width])

    acc = jnp.zeros(h.shape, F32)
    up(0)
    for jc, (c0, width) in enumerate(chunks):
        if jc + 1 < len(chunks):
            up(jc + 1)
        a = _gelu_tanh(conv(vg_ref, jc % 2, c0, width)) * conv(vv_ref, jc % 2, d_ff + c0, width)
        acc = acc + _dot(a.astype(BF16), wdn_ref[c0:c0 + width, :])
    h = h + acc
    gate = _sigmoid(_dot(_rms(h, gp_ref[...]).astype(BF16), wpg_ref[...]))
    h = h + gate * _dot(p_ref[...].astype(BF16), wple_ref[...])
    out_ref[...] = _rms(h, gfin_ref[...]) if final_norm else h


def _ffn(h1, p2, gf, wup, fw, fb, wdn, gp, wpg, wple, gfin, seq, final_norm):
    m, d = h1.shape
    d_ff = wdn.shape[0]
    assert d_ff % LANES == 0 and FF_CHUNK % LANES == 0
    row = lambda n: pl.BlockSpec((TM, n), lambda i: (i, 0))
    halo = pl.BlockSpec((FFN_HALO, d), lambda i: (jnp.maximum(i * (TM // FFN_HALO) - 1, 0), 0))
    weights = [gf, wup, fw, fb, wdn, gp, wpg, wple, gfin]
    return pl.pallas_call(
        functools.partial(_ffn_kernel, tiles_per_seq=seq // TM, d_ff=d_ff, final_norm=final_norm),
        grid=(m // TM,),
        in_specs=[row(d), halo, row(p2.shape[-1])] + [_resident(w.shape) for w in weights],
        out_specs=row(d),
        out_shape=jax.ShapeDtypeStruct((m, d), F32),
        scratch_shapes=[pltpu.VMEM((FFN_HALO + TM, d), BF16), pltpu.VMEM((2, FFN_HALO + TM, FF_CHUNK), F32),
                        pltpu.VMEM((2, FFN_HALO + TM, FF_CHUNK), F32)],
        compiler_params=pltpu.CompilerParams(dimension_semantics=("parallel",), vmem_limit_bytes=VMEM_LIMIT),
        name="ffn_ple",
    )(h1, h1, p2, *weights)


def _sel_map_t(seq):
    n_cmp = (seq - CMP_LEN) // CMP_STRIDE + 1
    n_sel = seq // SEL_LEN
    c0 = np.arange(n_cmp) * CMP_STRIDE
    j0 = np.arange(n_sel) * SEL_LEN
    lo = np.maximum(c0[None, :], j0[:, None])
    hi = np.minimum(c0[None, :] + CMP_LEN, j0[:, None] + SEL_LEN)
    out = np.zeros((N_SEL, N_CMP_PAD), np.float32)
    out[:n_sel, :n_cmp] = np.maximum(hi - lo, 0) / CMP_LEN
    return jnp.asarray(out, BF16)


def _inproj_weights(w):
    d = w.shape[0]
    nq = NSA_HEADS * HEAD_DIM
    kv0 = nq
    gate0 = kv0 + 3 * 2 * NSA_GROUPS * HEAD_DIM
    conv0 = gate0 + 3 * NSA_HEADS
    merge0 = conv0 + _ZCN

    def kv(br, g):
        lo = kv0 + (br * NSA_GROUPS + g) * HEAD_DIM
        return w[:, lo:lo + HEAD_DIM]

    z = jnp.zeros((d, HEAD_DIM), w.dtype)
    tok = [kv(2, 0), z, kv(2, 1), z, kv(4, 0), kv(4, 1), w[:, kv0:kv0 + _CVN], w[:, conv0:merge0],
           w[:, merge0:merge0 + _ZMN]]
    feat = [w[:, :nq] * Q_SCALE, kv(3, 0), kv(3, 1), kv(5, 0), kv(5, 1), w[:, gate0:conv0],
            jnp.zeros((d, _ZGN - 3 * NSA_HEADS), w.dtype)]
    w_tok = jnp.concatenate(tok, axis=1).astype(BF16)
    w_feat = jnp.concatenate(feat, axis=1).T.astype(BF16)
    assert w_tok.shape[1] == _N_TOK and w_feat.shape[0] == _N_FEAT
    return w_tok, w_feat


def kernel(x, p, g_mix, w_in, cmp_pos_k, cmp_pos_v, w_cmp_k1, w_cmp_k2, w_cmp_v1, w_cmp_v2, w_o_nsa, conv_w,
           conv_b, conv_ln_g, conv_ln_b, w_conv_out, b_conv_out, w_out, g_ffn, w_up, ffn_conv_w, ffn_conv_b,
           w_down, g_ple, w_ple_gate, w_ple, g_final):
    batch, seq, d = x.shape
    depth = w_in.shape[0]
    m = batch * seq
    assert seq % TM == 0 and seq % TK == 0
    sel = _sel_map_t(seq)
    row = lambda v: v.reshape(1, -1)
    h = x.reshape(m, d)
    for i in range(depth):
        w_tok, w_feat = _inproj_weights(w_in[i])
        pos = jnp.concatenate([cmp_pos_k[i]] * NSA_GROUPS + [cmp_pos_v[i]] * NSA_GROUPS, axis=1)
        w1 = jnp.stack([w_cmp_k1[i], w_cmp_v1[i]]).astype(BF16)
        zpad = jnp.zeros_like(w_cmp_k2[i])
        w2 = jnp.stack([jnp.concatenate([w_cmp_k2[i], zpad], axis=1),
                        jnp.concatenate([zpad, w_cmp_v2[i]], axis=1)]).astype(BF16)
        w2t = jnp.stack([jnp.concatenate([zpad, w_cmp_k2[i]], axis=1).T,
                         jnp.concatenate([w_cmp_v2[i], zpad], axis=1).T]).astype(BF16)

        qt, ka, kw, cv, c, sg, vst, vwt, gtt = _inproj(h, row(g_mix[i]), w_tok, w_feat, seq)
        kcv, kcvt = _compress(cv, pos, w1, w2, w2t, batch, seq)
        o = _attention(qt, kcv, kcvt, sel, ka, kw, vst, vwt, gtt, batch, seq)
        h = _merge(o, c, sg, h, w_o_nsa[i].astype(BF16), conv_w[i], row(conv_b[i]), row(conv_ln_g[i]),
                   row(conv_ln_b[i]), w_conv_out[i].astype(BF16), row(b_conv_out[i]), w_out[i].astype(BF16), seq)
        h = _ffn(h, p[i].reshape(m, -1), row(g_ffn[i]), w_up[i].astype(BF16), ffn_conv_w[i], row(ffn_conv_b[i]),
                 w_down[i].astype(BF16), row(g_ple[i]), w_ple_gate[i].astype(BF16), w_ple[i].astype(BF16),
                 row(g_final), seq, final_norm=(i == depth - 1))
    return h.reshape(batch, seq, d)
```

```python
import functools

import numpy as np
import jax
import jax.numpy as jnp
from jax import lax
from jax.experimental import pallas as pl
from jax.experimental.pallas import tpu as pltpu

F32 = jnp.float32
BF16 = jnp.bfloat16

NSA_HEADS = 8
NSA_GROUPS = 2
NSA_HPG = NSA_HEADS // NSA_GROUPS
HEAD_DIM = 64
CMP_LEN = 32
CMP_STRIDE = 16
SEL_LEN = 64
SEL_SHIFT = 6
SEL_TOP = 16
WINDOW = 512
FORCE_SCORE = 1e4
CONV_WIDTH = 31
FFN_CONV_WIDTH = 3
EPS = 1e-6
NEG = -1e30
MASK_BIAS = -1e9
Q_SCALE = HEAD_DIM ** -0.5 * float(np.log2(np.e))

LANES = 128
SUBLANES = 8
BF16_ROWS = 16
VMEM_LIMIT = 56 * 1024 * 1024

TM = 512
TQ = 256
TK = 512
VT = 256
WIN_TILES = (WINDOW + TQ) // VT + (1 if TQ % VT else 0)
CONV_HALO = 32
FFN_HALO = BF16_ROWS
CONV_ROWS = 64
FF_CHUNK = 2816


def _sigmoid(x):
    return 0.5 * (jnp.tanh(0.5 * x) + 1.0)


def _gelu_tanh(x):
    return 0.5 * x * (1.0 + jnp.tanh(np.sqrt(2.0 / np.pi).astype(np.float32) * (x + 0.044715 * (x * x * x))))


def _rms(x, g):
    return x * lax.rsqrt(jnp.mean(x * x, axis=-1, keepdims=True) + EPS) * g


def _zero_after(x):
    bits = lax.bitcast_convert_type(x, jnp.uint32)
    bits = lax.shift_right_logical(lax.shift_right_logical(bits, jnp.uint32(16)), jnp.uint32(16))
    return lax.bitcast_convert_type(bits, F32)


def _dot(a, b):
    return jnp.dot(a, b, preferred_element_type=F32)


def _dot_nt(a, b):
    return lax.dot_general(a, b, (((1,), (1,)), ((), ())), preferred_element_type=F32)


def _resident(shape):
    nd = len(shape)
    return pl.BlockSpec(shape, lambda *_: (0,) * nd, pipeline_mode=pl.Buffered(1))


_KA0, _KAN = 0, 256
_KW0, _KWN = 256, 128
_CV0, _CVN = 384, 256
_ZC0, _ZCN = 640, 1024
_ZM0, _ZMN = 1664, 2048
_N_TOK = 3712
_Q0, _QN = 0, 512
_VS0, _VSN = 512, 128
_VW0, _VWN = 640, 128
_ZG0, _ZGN = 768, 32
_N_FEAT = 800


def _inproj_kernel(x_ref, g_ref, wt_ref, wf_ref, qt_ref, ka_ref, kw_ref, cv_ref, c_ref, sg_ref, vst_ref, vwt_ref,
                   gtt_ref, *, tiles_per_seq):
    u = _rms(x_ref[...], g_ref[...]).astype(BF16)

    def tok(lo, n):
        return _dot(u, wt_ref[:, lo:lo + n])

    def feat(lo, n):
        return _dot_nt(wf_ref[lo:lo + n, :], u)

    ka = tok(_KA0, _KAN)
    s0 = (pl.program_id(0) % tiles_per_seq) * TM
    row = lax.broadcasted_iota(jnp.int32, ka.shape, 0) + s0
    col = lax.broadcasted_iota(jnp.int32, ka.shape, 1)
    onehot = ((col & HEAD_DIM) != 0) & ((col & (HEAD_DIM - 1)) == (row >> SEL_SHIFT))
    ka_ref[...] = jnp.where(onehot, 1.0, ka).astype(BF16)
    kw_ref[...] = tok(_KW0, _KWN).astype(BF16)
    cv_ref[...] = tok(_CV0, _CVN)
    zc = tok(_ZC0, _ZCN)
    half = _ZCN // 2
    c_ref[...] = zc[:, :half] * _sigmoid(zc[:, half:])
    sg_ref[...] = _sigmoid(tok(_ZM0, _ZMN))

    qt_ref[...] = feat(_Q0, _QN).astype(BF16)
    gtt_ref[...] = _sigmoid(feat(_ZG0, _ZGN))
    ones = jnp.ones((HEAD_DIM, VT), BF16)
    for src0, dst_ref in ((_VS0, vst_ref), (_VW0, vwt_ref)):
        v = feat(src0, NSA_GROUPS * HEAD_DIM).astype(BF16)
        for t in range(TM // VT):
            for g in range(NSA_GROUPS):
                r0 = g * 2 * HEAD_DIM
                dst_ref[t, r0:r0 + HEAD_DIM, :] = v[g * HEAD_DIM:(g + 1) * HEAD_DIM, t * VT:(t + 1) * VT]
                dst_ref[t, r0 + HEAD_DIM:r0 + 2 * HEAD_DIM, :] = ones


def _inproj(x2, g_mix, w_tok, w_feat, seq):
    m, d = x2.shape
    row = lambda n: pl.BlockSpec((TM, n), lambda i: (i, 0))
    colb = lambda n: pl.BlockSpec((n, TM), lambda i: (0, i))
    vt_spec = pl.BlockSpec((TM // VT, NSA_GROUPS * LANES, VT), lambda i: (i, 0, 0))
    vt_shape = jax.ShapeDtypeStruct((m // VT, NSA_GROUPS * LANES, VT), BF16)
    tok_outs = [(_KAN, BF16), (_KWN, BF16), (_CVN, F32), (_ZCN // 2, F32), (_ZMN, F32)]
    return pl.pallas_call(
        functools.partial(_inproj_kernel, tiles_per_seq=seq // TM),
        grid=(m // TM,),
        in_specs=[row(d), _resident((1, d)), _resident(w_tok.shape), _resident(w_feat.shape)],
        out_specs=[colb(_QN)] + [row(n) for n, _ in tok_outs] + [vt_spec, vt_spec, colb(_ZGN)],
        out_shape=([jax.ShapeDtypeStruct((_QN, m), BF16)]
                   + [jax.ShapeDtypeStruct((m, n), dt) for n, dt in tok_outs]
                   + [vt_shape, vt_shape, jax.ShapeDtypeStruct((_ZGN, m), F32)]),
        compiler_params=pltpu.CompilerParams(dimension_semantics=("parallel",), vmem_limit_bytes=VMEM_LIMIT),
        name="inproj",
    )(x2, g_mix, w_tok, w_feat)


N_CMP_PAD = 256
_CMP_HALF = CMP_LEN // 2


def _compress_kernel(ck_ref, cv_ref, pos_ref, w1_ref, w2_ref, w2t_ref, out_ref, outt_ref, *, n_cmp):
    hid = w1_ref.shape[-1]
    first = [jnp.zeros((N_CMP_PAD, hid), F32) for _ in range(4)]
    second = [jnp.zeros((N_CMP_PAD, hid), F32) for _ in range(4)]
    for l in range(_CMP_HALF):
        for kind, src_ref in enumerate((ck_ref, cv_ref)):
            xl = src_ref[pl.ds(l, N_CMP_PAD, stride=CMP_STRIDE), :]
            pcol = slice(kind * LANES, (kind + 1) * LANES)
            xa = (xl + pos_ref[l:l + 1, pcol]).astype(BF16)
            xb = (xl + pos_ref[_CMP_HALF + l:_CMP_HALF + l + 1, pcol]).astype(BF16)
            wa = w1_ref[kind, l * HEAD_DIM:(l + 1) * HEAD_DIM, :]
            wb = w1_ref[kind, (_CMP_HALF + l) * HEAD_DIM:(_CMP_HALF + l + 1) * HEAD_DIM, :]
            for g in range(NSA_GROUPS):
                kg = kind * NSA_GROUPS + g
                cols = slice(g * HEAD_DIM, (g + 1) * HEAD_DIM)
                first[kg] = first[kg] + _dot(xa[:, cols], wa)
                second[kg] = second[kg] + _dot(xb[:, cols], wb)
    rows = lax.broadcasted_iota(jnp.int32, (N_CMP_PAD, LANES), 0)
    lanes = lax.broadcasted_iota(jnp.int32, (LANES, N_CMP_PAD), 1)
    for g in range(NSA_GROUPS):
        acts = []
        for kind in range(2):
            kg = kind * NSA_GROUPS + g
            h1 = first[kg] + pltpu.roll(second[kg], N_CMP_PAD - 1, 0)
            acts.append(_gelu_tanh(h1).astype(BF16))
        kcv = _dot(acts[0], w2_ref[0]) + _dot(acts[1], w2_ref[1])
        kcvt = _dot_nt(w2t_ref[0], acts[0]) + _dot_nt(w2t_ref[1], acts[1])
        out_ref[0, g] = jnp.where(rows < n_cmp, kcv, 0.0).astype(BF16)
        outt_ref[0, g] = jnp.where(lanes < n_cmp, kcvt, 0.0).astype(BF16)


def _compress(cv, pos, w1, w2, w2t, batch, seq):
    n_cmp = (seq - CMP_LEN) // CMP_STRIDE + 1
    assert n_cmp < N_CMP_PAD and seq == N_CMP_PAD * CMP_STRIDE
    return pl.pallas_call(
        functools.partial(_compress_kernel, n_cmp=n_cmp),
        grid=(batch,),
        in_specs=[pl.BlockSpec((seq, LANES), lambda b: (b, 0)), pl.BlockSpec((seq, LANES), lambda b: (b, 1)),
                  _resident(pos.shape), _resident(w1.shape), _resident(w2.shape), _resident(w2t.shape)],
        out_specs=[pl.BlockSpec((1, NSA_GROUPS, N_CMP_PAD, LANES), lambda b: (b, 0, 0, 0)),
                   pl.BlockSpec((1, NSA_GROUPS, LANES, N_CMP_PAD), lambda b: (b, 0, 0, 0))],
        out_shape=[jax.ShapeDtypeStruct((batch, NSA_GROUPS, N_CMP_PAD, LANES), BF16),
                   jax.ShapeDtypeStruct((batch, NSA_GROUPS, LANES, N_CMP_PAD), BF16)],
        compiler_params=pltpu.CompilerParams(dimension_semantics=("parallel",), vmem_limit_bytes=VMEM_LIMIT),
        name="compress",
    )(cv, cv, pos, w1, w2, w2t)


N_SEL = 64


def _attn_kernel(qt_ref, kcv_ref, kcvt_ref, sel_ref, ka_ref, kw_ref, vst_ref, vwt_ref, gtt_ref, o_ref,
                 q0_ref, q1_ref, qs_ref, m_ref, acc_ref, s_ref, smax_ref, p_ref, alpha_ref, oc_ref, ow_ref):
    g = pl.program_id(1)
    qb = pl.program_id(2)
    t0 = qb * TQ
    cols = NSA_HPG * TQ

    zeros = jnp.zeros((HEAD_DIM, TQ), BF16)
    for h in range(NSA_HPG):
        qh = qt_ref[h * HEAD_DIM:(h + 1) * HEAD_DIM, :]
        cs = slice(h * TQ, (h + 1) * TQ)
        q0_ref[0:HEAD_DIM, cs] = qh
        q0_ref[HEAD_DIM:, cs] = zeros
        q1_ref[0:HEAD_DIM, cs] = zeros
        q1_ref[HEAD_DIM:, cs] = qh
        qs_ref[0:HEAD_DIM, cs] = qh
        qs_ref[HEAD_DIM:, cs] = zeros
    q0 = q0_ref[...]
    tq = t0 + (lax.broadcasted_iota(jnp.int32, (1, cols), 1) & (TQ - 1))

    sc = _dot(kcv_ref[0, 0], q0)
    wt0 = jnp.maximum(t0 - WINDOW, 0) // VT
    kstart = pl.multiple_of(wt0 * VT, VT)
    span = WIN_TILES * VT
    qwin = jnp.where(g == 0, q0, q1_ref[...])
    sw = _dot(kw_ref[pl.ds(kstart, span), :], qwin)

    tq1 = t0 + lax.broadcasted_iota(jnp.int32, (1, TQ), 1)

    def per_head(x):
        return jnp.concatenate([x] * NSA_HPG, axis=1)

    cend = lax.broadcasted_iota(jnp.int32, (N_CMP_PAD, 1), 0) * CMP_STRIDE + (CMP_LEN - 1)
    sc = sc + per_head(jnp.where(cend <= tq1, 0.0, NEG))
    ec = jnp.exp2(sc - jnp.max(sc, axis=0, keepdims=True))
    lc = jnp.sum(ec, axis=0, keepdims=True)
    visible = tq >= CMP_LEN - 1
    dist = tq1 - (kstart + lax.broadcasted_iota(jnp.int32, (span, 1), 0))
    sw = sw + per_head(jnp.where(dist.astype(jnp.uint32) < WINDOW, 0.0, NEG))
    wmax = jnp.max(sw, axis=0, keepdims=True)
    pc = ec * (jnp.where(visible, 1.0 / lc, 0.0) + _zero_after(wmax))
    oc_ref[...] = _dot(kcvt_ref[0, 0], pc.astype(BF16))[0:HEAD_DIM]

    psum = pc[:, 0:TQ]
    for h in range(1, NSA_HPG):
        psum = psum + pc[:, h * TQ:(h + 1) * TQ]
    sel = sel_ref[...]
    p_hi = psum.astype(BF16)
    rem = psum - p_hi.astype(F32)
    p_mid = rem.astype(BF16)
    p_lo = (rem - p_mid.astype(F32)).astype(BF16)
    imp = (_dot(sel, p_hi) + _dot(sel, p_mid)) + _dot(sel, p_lo)

    pw = jnp.exp2(sw - wmax).astype(BF16)
    ow = _dot(vwt_ref[wt0], pw[0:VT])
    for c in range(1, WIN_TILES):
        ow = ow + _dot(vwt_ref[wt0 + c], pw[c * VT:(c + 1) * VT])
    ow_ref[...] = ow[0:HEAD_DIM] * (1.0 / ow[HEAD_DIM:HEAD_DIM + 1])

    j = lax.broadcasted_iota(jnp.int32, (N_SEL, TQ), 0)
    cur = (t0 + lax.broadcasted_iota(jnp.int32, (N_SEL, TQ), 1)) >> SEL_SHIFT

    forced = (j == 0) | (j == cur) | (j == cur - 1)
    score = jnp.where(j <= cur, jnp.where(forced, FORCE_SCORE, imp), NEG)
    jloc = lax.broadcasted_iota(jnp.int32, (SUBLANES, TQ), 0)

    def set_bias(row_blocks, score):
        ranks = []
        for r in row_blocks:
            blk = score[r * SUBLANES:(r + 1) * SUBLANES]
            rank = jnp.zeros((SUBLANES, TQ), jnp.int32)
            for i in range(N_SEL):
                vi = jnp.broadcast_to(score[i:i + 1, :], (SUBLANES, TQ))
                if r * SUBLANES > i:
                    before = vi >= blk
                elif (r + 1) * SUBLANES - 1 <= i:
                    before = vi > blk
                else:
                    before = (vi > blk) | ((vi == blk) & (jloc > i - r * SUBLANES))
                rank = rank + before.astype(jnp.int32)
            ranks.append(rank)
        bias = jnp.where(jnp.concatenate(ranks, axis=0) < SEL_TOP, 0.0, MASK_BIAS).astype(BF16)
        lo = HEAD_DIM + row_blocks[0] * SUBLANES
        for h in range(NSA_HPG):
            qs_ref[lo:lo + bias.shape[0], h * TQ:(h + 1) * TQ] = bias

    m_ref[...] = jnp.full(m_ref.shape, NEG, F32)
    acc_ref[...] = jnp.zeros(acc_ref.shape, F32)
    ksub = lax.broadcasted_iota(jnp.int32, (TK, 1), 0)
    even, odd = 0, 1

    def logits(kt, slot):
        k0 = pl.multiple_of(kt * TK, TK)
        s = _dot(ka_ref[pl.ds(k0, TK), :], qs_ref[...])
        s_ref[slot] = s
        smax_ref[slot] = jnp.max(s, axis=0, keepdims=True)
        return s[0:SUBLANES, 0:TQ]

    def softmax(kt, slot, causal):
        s = s_ref[slot]
        if causal:
            s = jnp.where(kt * TK + ksub <= tq, s, NEG)
            smax = jnp.max(s, axis=0, keepdims=True)
        else:
            smax = smax_ref[slot]
        m_prev = m_ref[...]
        m_new = jnp.maximum(m_prev, smax)
        p_ref[slot] = jnp.exp2(s - m_new).astype(BF16)
        alpha_ref[slot] = jnp.exp2(m_prev - m_new)
        m_ref[...] = m_new

    def accumulate(kt, slot):
        p = p_ref[slot]
        v0 = jnp.maximum(kt, 0) * (TK // VT)
        pv = _dot(vst_ref[v0], p[0:VT])
        for c in range(1, TK // VT):
            pv = pv + _dot(vst_ref[v0 + c], p[c * VT:(c + 1) * VT])
        acc_ref[...] = alpha_ref[slot] * acc_ref[...] + pv

    p_ref[odd] = jnp.zeros(p_ref.shape[1:], BF16)
    alpha_ref[odd] = jnp.ones(alpha_ref.shape[1:], F32)
    last = qb // (TK // TQ)
    pairs = last // 2
    n_row_blocks = N_SEL // SUBLANES
    first_blocks = max(TK // SEL_LEN, BF16_ROWS) // SUBLANES
    set_bias(list(range(first_blocks)), score)
    started = logits(0, even)
    set_bias(list(range(first_blocks, n_row_blocks)), score + _zero_after(jnp.concatenate([started] * n_row_blocks, axis=0)))

    def body(jp, carry):
        a = 2 * jp
        accumulate(a - 1, odd)
        softmax(a, even, False)
        logits(a + 1, odd)
        accumulate(a, even)
        softmax(a + 1, odd, False)
        logits(a + 2, even)
        return carry

    lax.fori_loop(0, pairs, body, 0)
    a = 2 * pairs

    @pl.when(last == a)
    def _():
        accumulate(a - 1, odd)
        softmax(a, even, True)
        accumulate(a, even)

    @pl.when(last != a)
    def _():
        accumulate(a - 1, odd)
        softmax(a, even, False)
        logits(a + 1, odd)
        accumulate(a, even)
        softmax(a + 1, odd, True)
        accumulate(a + 1, odd)

    acc = acc_ref[...]
    o_s = acc[0:HEAD_DIM] * (1.0 / acc[HEAD_DIM:HEAD_DIM + 1])

    o_c = oc_ref[...]
    o_w = ow_ref[...]
    heads = []
    for h in range(NSA_HPG):
        cs = slice(h * TQ, (h + 1) * TQ)

        def gate(br):
            return gtt_ref[pl.ds(br * NSA_HEADS + g * NSA_HPG + h, 1), :]

        heads.append(gate(0) * o_c[:, cs] + gate(1) * o_s[:, cs] + gate(2) * o_w[:, cs])
    o_ref[...] = jnp.concatenate(heads, axis=0).T.astype(BF16)


def _attention(qt, kcv, kcvt, sel, ka, kw, vst, vwt, gtt, batch, seq):
    nq = seq // TQ
    cols = NSA_HPG * TQ
    assert seq // SEL_LEN == N_SEL and WIN_TILES * VT <= seq and TK % VT == 0 and TK % TQ == 0
    assert WINDOW % VT == 0 and (TQ % VT == 0 or VT % TQ == 0)
    vt_spec = pl.BlockSpec((seq // VT, LANES, VT), lambda b, g, i: (b, g, 0))
    return pl.pallas_call(
        _attn_kernel,
        grid=(batch, NSA_GROUPS, nq),
        in_specs=[
            pl.BlockSpec((NSA_HPG * HEAD_DIM, TQ), lambda b, g, i: (g, b * nq + i)),
            pl.BlockSpec((1, 1, N_CMP_PAD, LANES), lambda b, g, i: (b, g, 0, 0)),
            pl.BlockSpec((1, 1, LANES, N_CMP_PAD), lambda b, g, i: (b, g, 0, 0)),
            pl.BlockSpec(sel.shape, lambda b, g, i: (0, 0)),
            pl.BlockSpec((seq, LANES), lambda b, g, i: (b, g)),
            pl.BlockSpec((seq, LANES), lambda b, g, i: (b, 0)),
            vt_spec, vt_spec,
            pl.BlockSpec((_ZGN, TQ), lambda b, g, i: (0, b * nq + i)),
        ],
        out_specs=pl.BlockSpec((TQ, NSA_HPG * HEAD_DIM), lambda b, g, i: (b * nq + i, g)),
        out_shape=jax.ShapeDtypeStruct((batch * seq, NSA_HEADS * HEAD_DIM), BF16),
        scratch_shapes=[
            pltpu.VMEM((LANES, cols), BF16),
            pltpu.VMEM((LANES, cols), BF16),
            pltpu.VMEM((LANES, cols), BF16),
            pltpu.VMEM((1, cols), F32),
            pltpu.VMEM((LANES, cols), F32),
            pltpu.VMEM((2, TK, cols), F32),
            pltpu.VMEM((2, 1, cols), F32),
            pltpu.VMEM((2, TK, cols), BF16),
            pltpu.VMEM((2, 1, cols), F32),
            pltpu.VMEM((HEAD_DIM, cols), F32),
            pltpu.VMEM((HEAD_DIM, cols), F32),
        ],
        compiler_params=pltpu.CompilerParams(dimension_semantics=("parallel", "parallel", "arbitrary"),
                                             vmem_limit_bytes=VMEM_LIMIT),
        name="nsa_attention",
    )(qt, kcv, kcvt, sel, ka, kw, vst, vwt, gtt)


def _merge_kernel(o_ref, c_ref, halo_ref, sg_ref, x_ref, wo_ref, cw_ref, cb_ref, lg_ref, lb_ref, wco_ref,
                  bco_ref, wout_ref, h_ref, cext_ref, cn_ref, *, tiles_per_seq):
    first = (pl.program_id(0) % tiles_per_seq) == 0
    cext_ref[0:CONV_HALO, :] = jnp.where(first, 0.0, halo_ref[...])
    cext_ref[CONV_HALO:, :] = c_ref[...]
    shift = CONV_HALO - (CONV_WIDTH - 1)
    for r in range(TM // CONV_ROWS):
        r0 = r * CONV_ROWS
        acc = jnp.broadcast_to(cb_ref[...], (CONV_ROWS, cb_ref.shape[-1]))
        for res in range(SUBLANES):
            taps = [k for k in range(CONV_WIDTH) if (shift + k) % SUBLANES == res]
            n_rows = CONV_ROWS + (SUBLANES if res else 0)
            part = None
            for k in taps:
                base = r0 + shift + k - res
                term = cw_ref[k:k + 1, :] * cext_ref[base:base + n_rows, :]
                part = term if part is None else part + term
            acc = acc + part[res:res + CONV_ROWS]
        xc = acc - jnp.mean(acc, axis=-1, keepdims=True)
        y = xc * lax.rsqrt(jnp.mean(xc * xc, axis=-1, keepdims=True) + EPS) * lg_ref[...] + lb_ref[...]
        cn_ref[r0:r0 + CONV_ROWS, :] = (y * _sigmoid(y)).astype(BF16)
    y_b = _dot(cn_ref[...], wco_ref[...]) + bco_ref[...]
    y_a = _dot(o_ref[...], wo_ref[...])
    d = y_a.shape[-1]
    mix = sg_ref[:, :d] * y_a + sg_ref[:, d:] * y_b
    h_ref[...] = x_ref[...] + _dot(mix.astype(BF16), wout_ref[...])


def _merge(o, c, sg, x2, wo, cw, cb, lg, lb, wco, bco, wout, seq):
    m, d = x2.shape
    ch = c.shape[-1]
    row = lambda n: pl.BlockSpec((TM, n), lambda i: (i, 0))
    halo = pl.BlockSpec((CONV_HALO, ch), lambda i: (jnp.maximum(i * (TM // CONV_HALO) - 1, 0), 0))
    weights = [wo, cw, cb, lg, lb, wco, bco, wout]
    return pl.pallas_call(
        functools.partial(_merge_kernel, tiles_per_seq=seq // TM),
        grid=(m // TM,),
        in_specs=[row(o.shape[-1]), row(ch), halo, row(sg.shape[-1]), row(d)] + [_resident(w.shape) for w in weights],
        out_specs=row(d),
        out_shape=jax.ShapeDtypeStruct((m, d), F32),
        scratch_shapes=[pltpu.VMEM((CONV_HALO + TM, ch), F32), pltpu.VMEM((TM, ch), BF16)],
        compiler_params=pltpu.CompilerParams(dimension_semantics=("parallel",), vmem_limit_bytes=VMEM_LIMIT),
        name="merge",
    )(o, c, c, sg, x2, *weights)


def _ffn_kernel(h_ref, halo_ref, p_ref, gf_ref, wup_ref, fw_ref, fb_ref, wdn_ref, gp_ref, wpg_ref, wple_ref,
                gfin_ref, out_ref, u_ref, vg_ref, vv_ref, *, tiles_per_seq, d_ff, final_norm):
    first = (pl.program_id(0) % tiles_per_seq) == 0
    h = h_ref[...]
    u_ref[0:FFN_HALO, :] = _rms(jnp.where(first, 0.0, halo_ref[...]), gf_ref[...]).astype(BF16)
    u_ref[FFN_HALO:, :] = _rms(h, gf_ref[...]).astype(BF16)
    u = u_ref[...]
    shift = FFN_HALO - (FFN_CONV_WIDTH - 1)

    starts = list(range(0, d_ff, FF_CHUNK))
    chunks = [(c0, min(FF_CHUNK, d_ff - c0)) for c0 in starts]

    def conv(v_ref, slot, c0, width):
        acc = fb_ref[:, c0:c0 + width]
        for k in range(FFN_CONV_WIDTH):
            acc = acc + fw_ref[k:k + 1, c0:c0 + width] * v_ref[slot, shift + k:shift + k + TM, 0:width]
        return acc

    def up(jc):
        c0, width = chunks[jc]
        vg_ref[jc % 2, :, 0:width] = _dot(u, wup_ref[:, c0:c0 + width])
        vv_ref[jc % 2, :, 0:width] = _dot(u, wup_ref[:, d_ff + c0:d_ff + c0 + width])

    acc = jnp.zeros(h.shape, F32)
    up(0)
    for jc, (c0, width) in enumerate(chunks):
        if jc + 1 < len(chunks):
            up(jc + 1)
        a = _gelu_tanh(conv(vg_ref, jc % 2, c0, width)) * conv(vv_ref, jc % 2, d_ff + c0, width)
        acc = acc + _dot(a.astype(BF16), wdn_ref[c0:c0 + width, :])
    h = h + acc
    gate = _sigmoid(_dot(_rms(h, gp_ref[...]).astype(BF16), wpg_ref[...]))
    h = h + gate * _dot(p_ref[...].astype(BF16), wple_ref[...])
    out_ref[...] = _rms(h, gfin_ref[...]) if final_norm else h


def _ffn(h1, p2, gf, wup, fw, fb, wdn, gp, wpg, wple, gfin, seq, final_norm):
    m, d = h1.shape
    d_ff = wdn.shape[0]
    assert d_ff % LANES == 0 and FF_CHUNK % LANES == 0
    slots = min(2, pl.cdiv(d_ff, FF_CHUNK))
    row = lambda n: pl.BlockSpec((TM, n), lambda i: (i, 0))
    halo = pl.BlockSpec((FFN_HALO, d), lambda i: (jnp.maximum(i * (TM // FFN_HALO) - 1, 0), 0))
    weights = [gf, wup, fw, fb, wdn, gp, wpg, wple, gfin]
    return pl.pallas_call(
        functools.partial(_ffn_kernel, tiles_per_seq=seq // TM, d_ff=d_ff, final_norm=final_norm),
        grid=(m // TM,),
        in_specs=[row(d), halo, row(p2.shape[-1])] + [_resident(w.shape) for w in weights],
        out_specs=row(d),
        out_shape=jax.ShapeDtypeStruct((m, d), F32),
        scratch_shapes=[pltpu.VMEM((FFN_HALO + TM, d), BF16), pltpu.VMEM((slots, FFN_HALO + TM, FF_CHUNK), F32),
                        pltpu.VMEM((slots, FFN_HALO + TM, FF_CHUNK), F32)],
        compiler_params=pltpu.CompilerParams(dimension_semantics=("parallel",), vmem_limit_bytes=VMEM_LIMIT),
        name="ffn_ple",
    )(h1, h1, p2, *weights)


def _sel_map_t(seq):
    n_cmp = (seq - CMP_LEN) // CMP_STRIDE + 1
    n_sel = seq // SEL_LEN
    c0 = np.arange(n_cmp) * CMP_STRIDE
    j0 = np.arange(n_sel) * SEL_LEN
    lo = np.maximum(c0[None, :], j0[:, None])
    hi = np.minimum(c0[None, :] + CMP_LEN, j0[:, None] + SEL_LEN)
    out = np.zeros((N_SEL, N_CMP_PAD), np.float32)
    out[:n_sel, :n_cmp] = np.maximum(hi - lo, 0) / CMP_LEN
    return jnp.asarray(out, BF16)


def _inproj_weights(w):
    d = w.shape[0]
    nq = NSA_HEADS * HEAD_DIM
    kv0 = nq
    gate0 = kv0 + 3 * 2 * NSA_GROUPS * HEAD_DIM
    conv0 = gate0 + 3 * NSA_HEADS
    merge0 = conv0 + _ZCN

    def kv(br, g):
        lo = kv0 + (br * NSA_GROUPS + g) * HEAD_DIM
        return w[:, lo:lo + HEAD_DIM]

    z = jnp.zeros((d, HEAD_DIM), w.dtype)
    tok = [kv(2, 0), z, kv(2, 1), z, kv(4, 0), kv(4, 1), w[:, kv0:kv0 + _CVN], w[:, conv0:merge0],
           w[:, merge0:merge0 + _ZMN]]
    feat = [w[:, :nq] * Q_SCALE, kv(3, 0), kv(3, 1), kv(5, 0), kv(5, 1), w[:, gate0:conv0],
            jnp.zeros((d, _ZGN - 3 * NSA_HEADS), w.dtype)]
    w_tok = jnp.concatenate(tok, axis=1).astype(BF16)
    w_feat = jnp.concatenate(feat, axis=1).T.astype(BF16)
    assert w_tok.shape[1] == _N_TOK and w_feat.shape[0] == _N_FEAT
    return w_tok, w_feat


def kernel(x, p, g_mix, w_in, cmp_pos_k, cmp_pos_v, w_cmp_k1, w_cmp_k2, w_cmp_v1, w_cmp_v2, w_o_nsa, conv_w,
           conv_b, conv_ln_g, conv_ln_b, w_conv_out, b_conv_out, w_out, g_ffn, w_up, ffn_conv_w, ffn_conv_b,
           w_down, g_ple, w_ple_gate, w_ple, g_final):
    batch, seq, d = x.shape
    depth = w_in.shape[0]
    m = batch * seq
    assert seq % TM == 0 and seq % TK == 0
    sel = _sel_map_t(seq)
    row = lambda v: v.reshape(1, -1)
    h = x.reshape(m, d)
    for i in range(depth):
        w_tok, w_feat = _inproj_weights(w_in[i])
        pos = jnp.concatenate([cmp_pos_k[i]] * NSA_GROUPS + [cmp_pos_v[i]] * NSA_GROUPS, axis=1)
        w1 = jnp.stack([w_cmp_k1[i], w_cmp_v1[i]]).astype(BF16)
        zpad = jnp.zeros_like(w_cmp_k2[i])
        w2 = jnp.stack([jnp.concatenate([w_cmp_k2[i], zpad], axis=1),
                        jnp.concatenate([zpad, w_cmp_v2[i]], axis=1)]).astype(BF16)
        w2t = jnp.stack([jnp.concatenate([zpad, w_cmp_k2[i]], axis=1).T,
                         jnp.concatenate([w_cmp_v2[i], zpad], axis=1).T]).astype(BF16)

        qt, ka, kw, cv, c, sg, vst, vwt, gtt = _inproj(h, row(g_mix[i]), w_tok, w_feat, seq)
        kcv, kcvt = _compress(cv, pos, w1, w2, w2t, batch, seq)
        o = _attention(qt, kcv, kcvt, sel, ka, kw, vst, vwt, gtt, batch, seq)
        h = _merge(o, c, sg, h, w_o_nsa[i].astype(BF16), conv_w[i], row(conv_b[i]), row(conv_ln_g[i]),
                   row(conv_ln_b[i]), w_conv_out[i].astype(BF16), row(b_conv_out[i]), w_out[i].astype(BF16), seq)
        h = _ffn(h, p[i].reshape(m, -1), row(g_ffn[i]), w_up[i].astype(BF16), ffn_conv_w[i], row(ffn_conv_b[i]),
                 w_down[i].astype(BF16), row(g_ple[i]), w_ple_gate[i].astype(BF16), w_ple[i].astype(BF16),
                 row(g_final), seq, final_norm=(i == depth - 1))
    return h.reshape(batch, seq, d)
```

```python
import functools

import numpy as np
import jax
import jax.numpy as jnp
from jax import lax
from jax.experimental import pallas as pl
from jax.experimental.pallas import tpu as pltpu

F32 = jnp.float32
BF16 = jnp.bfloat16

NSA_HEADS = 8
NSA_GROUPS = 2
NSA_HPG = NSA_HEADS // NSA_GROUPS
HEAD_DIM = 64
CMP_LEN = 32
CMP_STRIDE = 16
SEL_LEN = 64
SEL_SHIFT = 6
SEL_TOP = 16
WINDOW = 512
FORCE_SCORE = 1e4
CONV_WIDTH = 31
FFN_CONV_WIDTH = 3
EPS = 1e-6
NEG = -1e30
MASK_BIAS = -1e9
Q_SCALE = HEAD_DIM ** -0.5 * float(np.log2(np.e))

LANES = 128
SUBLANES = 8
BF16_ROWS = 16
VMEM_LIMIT = 56 * 1024 * 1024

TM = 512
TQ = 256
TK = 512
VT = 256
WIN_TILES = (WINDOW + TQ) // VT + (1 if TQ % VT else 0)
CONV_HALO = 32
FFN_HALO = BF16_ROWS


def _sigmoid(x):
    return 0.5 * (jnp.tanh(0.5 * x) + 1.0)


def _gelu_tanh(x):
    return 0.5 * x * (1.0 + jnp.tanh(np.sqrt(2.0 / np.pi).astype(np.float32) * (x + 0.044715 * (x * x * x))))


def _rms(x, g):
    return x * lax.rsqrt(jnp.mean(x * x, axis=-1, keepdims=True) + EPS) * g


def _zero_after(x):
    bits = lax.bitcast_convert_type(x, jnp.uint32)
    bits = lax.shift_right_logical(lax.shift_right_logical(bits, jnp.uint32(16)), jnp.uint32(16))
    return lax.bitcast_convert_type(bits, F32)


def _dot(a, b):
    return jnp.dot(a, b, preferred_element_type=F32)


def _dot_nt(a, b):
    return lax.dot_general(a, b, (((1,), (1,)), ((), ())), preferred_element_type=F32)


def _resident(shape):
    nd = len(shape)
    return pl.BlockSpec(shape, lambda *_: (0,) * nd, pipeline_mode=pl.Buffered(1))


_KA0, _KAN = 0, 256
_KW0, _KWN = 256, 128
_CV0, _CVN = 384, 256
_ZC0, _ZCN = 640, 1024
_ZM0, _ZMN = 1664, 2048
_N_TOK = 3712
_Q0, _QN = 0, 512
_VS0, _VSN = 512, 128
_VW0, _VWN = 640, 128
_ZG0, _ZGN = 768, 32
_N_FEAT = 800


def _inproj_kernel(x_ref, g_ref, wt_ref, wf_ref, qt_ref, ka_ref, kw_ref, cv_ref, c_ref, sg_ref, vst_ref, vwt_ref,
                   gtt_ref, *, tiles_per_seq):
    u = _rms(x_ref[...], g_ref[...]).astype(BF16)

    z_tok = _dot(u, wt_ref[...])
    z_feat = _dot_nt(wf_ref[...], u)

    def tok(lo, n):
        return z_tok[:, lo:lo + n]

    def feat(lo, n):
        return z_feat[lo:lo + n, :]

    ka = tok(_KA0, _KAN)
    s0 = (pl.program_id(0) % tiles_per_seq) * TM
    row = lax.broadcasted_iota(jnp.int32, ka.shape, 0) + s0
    col = lax.broadcasted_iota(jnp.int32, ka.shape, 1)
    onehot = ((col & HEAD_DIM) != 0) & ((col & (HEAD_DIM - 1)) == (row >> SEL_SHIFT))
    ka_ref[...] = jnp.where(onehot, 1.0, ka).astype(BF16)
    kw_ref[...] = tok(_KW0, _KWN).astype(BF16)
    cv_ref[...] = tok(_CV0, _CVN)
    zc = tok(_ZC0, _ZCN)
    half = _ZCN // 2
    c_ref[...] = zc[:, :half] * _sigmoid(zc[:, half:])
    sg_ref[...] = _sigmoid(tok(_ZM0, _ZMN))

    qt_ref[...] = feat(_Q0, _QN).astype(BF16)
    gtt_ref[...] = _sigmoid(feat(_ZG0, _ZGN))
    ones = jnp.ones((HEAD_DIM, VT), BF16)
    for src0, dst_ref in ((_VS0, vst_ref), (_VW0, vwt_ref)):
        v = feat(src0, NSA_GROUPS * HEAD_DIM).astype(BF16)
        for t in range(TM // VT):
            for g in range(NSA_GROUPS):
                r0 = g * 2 * HEAD_DIM
                dst_ref[t, r0:r0 + HEAD_DIM, :] = v[g * HEAD_DIM:(g + 1) * HEAD_DIM, t * VT:(t + 1) * VT]
                dst_ref[t, r0 + HEAD_DIM:r0 + 2 * HEAD_DIM, :] = ones


def _inproj(x2, g_mix, w_tok, w_feat, seq):
    m, d = x2.shape
    row = lambda n: pl.BlockSpec((TM, n), lambda i: (i, 0))
    colb = lambda n: pl.BlockSpec((n, TM), lambda i: (0, i))
    vt_spec = pl.BlockSpec((TM // VT, NSA_GROUPS * LANES, VT), lambda i: (i, 0, 0))
    vt_shape = jax.ShapeDtypeStruct((m // VT, NSA_GROUPS * LANES, VT), BF16)
    tok_outs = [(_KAN, BF16), (_KWN, BF16), (_CVN, F32), (_ZCN // 2, F32), (_ZMN, F32)]
    return pl.pallas_call(
        functools.partial(_inproj_kernel, tiles_per_seq=seq // TM),
        grid=(m // TM,),
        in_specs=[row(d), _resident((1, d)), _resident(w_tok.shape), _resident(w_feat.shape)],
        out_specs=[colb(_QN)] + [row(n) for n, _ in tok_outs] + [vt_spec, vt_spec, colb(_ZGN)],
        out_shape=([jax.ShapeDtypeStruct((_QN, m), BF16)]
                   + [jax.ShapeDtypeStruct((m, n), dt) for n, dt in tok_outs]
                   + [vt_shape, vt_shape, jax.ShapeDtypeStruct((_ZGN, m), F32)]),
        compiler_params=pltpu.CompilerParams(dimension_semantics=("parallel",), vmem_limit_bytes=VMEM_LIMIT),
        name="inproj",
    )(x2, g_mix, w_tok, w_feat)


N_CMP_PAD = 256
_CMP_HALF = CMP_LEN // 2


def _compress_kernel(ck_ref, cv_ref, pos_ref, w1_ref, w2_ref, w2t_ref, out_ref, outt_ref, *, n_cmp):
    hid = w1_ref.shape[-1]
    first = [jnp.zeros((N_CMP_PAD, hid), F32) for _ in range(4)]
    second = [jnp.zeros((N_CMP_PAD, hid), F32) for _ in range(4)]
    for l in range(_CMP_HALF):
        for kind, src_ref in enumerate((ck_ref, cv_ref)):
            xl = src_ref[pl.ds(l, N_CMP_PAD, stride=CMP_STRIDE), :]
            pcol = slice(kind * LANES, (kind + 1) * LANES)
            xa = (xl + pos_ref[l:l + 1, pcol]).astype(BF16)
            xb = (xl + pos_ref[_CMP_HALF + l:_CMP_HALF + l + 1, pcol]).astype(BF16)
            wa = w1_ref[kind, l * HEAD_DIM:(l + 1) * HEAD_DIM, :]
            wb = w1_ref[kind, (_CMP_HALF + l) * HEAD_DIM:(_CMP_HALF + l + 1) * HEAD_DIM, :]
            for g in range(NSA_GROUPS):
                kg = kind * NSA_GROUPS + g
                cols = slice(g * HEAD_DIM, (g + 1) * HEAD_DIM)
                first[kg] = first[kg] + _dot(xa[:, cols], wa)
                second[kg] = second[kg] + _dot(xb[:, cols], wb)
    rows = lax.broadcasted_iota(jnp.int32, (N_CMP_PAD, LANES), 0)
    lanes = lax.broadcasted_iota(jnp.int32, (LANES, N_CMP_PAD), 1)
    for g in range(NSA_GROUPS):
        acts = []
        for kind in range(2):
            kg = kind * NSA_GROUPS + g
            h1 = first[kg] + pltpu.roll(second[kg], N_CMP_PAD - 1, 0)
            acts.append(_gelu_tanh(h1).astype(BF16))
        kcv = _dot(acts[0], w2_ref[0]) + _dot(acts[1], w2_ref[1])
        kcvt = _dot_nt(w2t_ref[0], acts[0]) + _dot_nt(w2t_ref[1], acts[1])
        out_ref[0, g] = jnp.where(rows < n_cmp, kcv, 0.0).astype(BF16)
        outt_ref[0, g] = jnp.where(lanes < n_cmp, kcvt, 0.0).astype(BF16)


def _compress(cv, pos, w1, w2, w2t, batch, seq):
    n_cmp = (seq - CMP_LEN) // CMP_STRIDE + 1
    assert n_cmp < N_CMP_PAD and seq == N_CMP_PAD * CMP_STRIDE
    return pl.pallas_call(
        functools.partial(_compress_kernel, n_cmp=n_cmp),
        grid=(batch,),
        in_specs=[pl.BlockSpec((seq, LANES), lambda b: (b, 0)), pl.BlockSpec((seq, LANES), lambda b: (b, 1)),
                  _resident(pos.shape), _resident(w1.shape), _resident(w2.shape), _resident(w2t.shape)],
        out_specs=[pl.BlockSpec((1, NSA_GROUPS, N_CMP_PAD, LANES), lambda b: (b, 0, 0, 0)),
                   pl.BlockSpec((1, NSA_GROUPS, LANES, N_CMP_PAD), lambda b: (b, 0, 0, 0))],
        out_shape=[jax.ShapeDtypeStruct((batch, NSA_GROUPS, N_CMP_PAD, LANES), BF16),
                   jax.ShapeDtypeStruct((batch, NSA_GROUPS, LANES, N_CMP_PAD), BF16)],
        compiler_params=pltpu.CompilerParams(dimension_semantics=("parallel",), vmem_limit_bytes=VMEM_LIMIT),
        name="compress",
    )(cv, cv, pos, w1, w2, w2t)


N_SEL = 64


def _attn_kernel(qt_ref, kcv_ref, kcvt_ref, sel_ref, ka_ref, kw_ref, vst_ref, vwt_ref, gtt_ref, o_ref,
                 q0_ref, q1_ref, qs_ref, m_ref, acc_ref, s_ref, smax_ref, p_ref, alpha_ref, oc_ref, ow_ref):
    g = pl.program_id(1)
    qb = pl.program_id(2)
    t0 = qb * TQ
    cols = NSA_HPG * TQ

    zeros = jnp.zeros((HEAD_DIM, TQ), BF16)
    for h in range(NSA_HPG):
        qh = qt_ref[h * HEAD_DIM:(h + 1) * HEAD_DIM, :]
        cs = slice(h * TQ, (h + 1) * TQ)
        q0_ref[0:HEAD_DIM, cs] = qh
        q0_ref[HEAD_DIM:, cs] = zeros
        q1_ref[0:HEAD_DIM, cs] = zeros
        q1_ref[HEAD_DIM:, cs] = qh
        qs_ref[0:HEAD_DIM, cs] = qh
        qs_ref[HEAD_DIM:, cs] = zeros
    q0 = q0_ref[...]
    tq = t0 + (lax.broadcasted_iota(jnp.int32, (1, cols), 1) & (TQ - 1))

    sc = _dot(kcv_ref[0, 0], q0)
    wt0 = jnp.maximum(t0 - WINDOW, 0) // VT
    kstart = pl.multiple_of(wt0 * VT, VT)
    span = WIN_TILES * VT
    qwin = jnp.where(g == 0, q0, q1_ref[...])
    sw = _dot(kw_ref[pl.ds(kstart, span), :], qwin)

    tq1 = t0 + lax.broadcasted_iota(jnp.int32, (1, TQ), 1)

    def per_head(x):
        return jnp.concatenate([x] * NSA_HPG, axis=1)

    cend = lax.broadcasted_iota(jnp.int32, (N_CMP_PAD, 1), 0) * CMP_STRIDE + (CMP_LEN - 1)
    sc = sc + per_head(jnp.where(cend <= tq1, 0.0, NEG))
    ec = jnp.exp2(sc - jnp.max(sc, axis=0, keepdims=True))
    lc = jnp.sum(ec, axis=0, keepdims=True)
    visible = tq >= CMP_LEN - 1
    dist = tq1 - (kstart + lax.broadcasted_iota(jnp.int32, (span, 1), 0))
    sw = sw + per_head(jnp.where(dist.astype(jnp.uint32) < WINDOW, 0.0, NEG))
    wmax = jnp.max(sw, axis=0, keepdims=True)
    pc = ec * (jnp.where(visible, 1.0 / lc, 0.0) + _zero_after(wmax))
    oc_ref[...] = _dot(kcvt_ref[0, 0], pc.astype(BF16))[0:HEAD_DIM]

    psum = pc[:, 0:TQ]
    for h in range(1, NSA_HPG):
        psum = psum + pc[:, h * TQ:(h + 1) * TQ]
    sel = sel_ref[...]
    p_hi = psum.astype(BF16)
    rem = psum - p_hi.astype(F32)
    p_mid = rem.astype(BF16)
    p_lo = (rem - p_mid.astype(F32)).astype(BF16)
    imp = (_dot(sel, p_hi) + _dot(sel, p_mid)) + _dot(sel, p_lo)

    pw = jnp.exp2(sw - wmax).astype(BF16)
    ow = _dot(vwt_ref[wt0], pw[0:VT])
    for c in range(1, WIN_TILES):
        ow = ow + _dot(vwt_ref[wt0 + c], pw[c * VT:(c + 1) * VT])
    ow_ref[...] = ow[0:HEAD_DIM] * (1.0 / ow[HEAD_DIM:HEAD_DIM + 1])

    j = lax.broadcasted_iota(jnp.int32, (N_SEL, TQ), 0)
    cur = (t0 + lax.broadcasted_iota(jnp.int32, (N_SEL, TQ), 1)) >> SEL_SHIFT

    forced = (j == 0) | (j == cur) | (j == cur - 1)
    score = jnp.where(j <= cur, jnp.where(forced, FORCE_SCORE, imp), NEG)
    jloc = lax.broadcasted_iota(jnp.int32, (SUBLANES, TQ), 0)

    def set_bias(row_blocks, score):
        ranks = []
        for r in row_blocks:
            blk = score[r * SUBLANES:(r + 1) * SUBLANES]
            rank = jnp.zeros((SUBLANES, TQ), jnp.int32)
            for i in range(N_SEL):
                vi = jnp.broadcast_to(score[i:i + 1, :], (SUBLANES, TQ))
                if r * SUBLANES > i:
                    before = vi >= blk
                elif (r + 1) * SUBLANES - 1 <= i:
                    before = vi > blk
                else:
                    before = (vi > blk) | ((vi == blk) & (jloc > i - r * SUBLANES))
                rank = rank + before.astype(jnp.int32)
            ranks.append(rank)
        bias = jnp.where(jnp.concatenate(ranks, axis=0) < SEL_TOP, 0.0, MASK_BIAS).astype(BF16)
        lo = HEAD_DIM + row_blocks[0] * SUBLANES
        for h in range(NSA_HPG):
            qs_ref[lo:lo + bias.shape[0], h * TQ:(h + 1) * TQ] = bias

    m_ref[...] = jnp.full(m_ref.shape, NEG, F32)
    acc_ref[...] = jnp.zeros(acc_ref.shape, F32)
    ksub = lax.broadcasted_iota(jnp.int32, (TK, 1), 0)
    even, odd = 0, 1

    def logits(kt, slot):
        k0 = pl.multiple_of(kt * TK, TK)
        s = _dot(ka_ref[pl.ds(k0, TK), :], qs_ref[...])
        s_ref[slot] = s
        smax_ref[slot] = jnp.max(s, axis=0, keepdims=True)
        return s[0:SUBLANES, 0:TQ]

    def softmax(kt, slot, causal):
        s = s_ref[slot]
        if causal:
            s = jnp.where(kt * TK + ksub <= tq, s, NEG)
            smax = jnp.max(s, axis=0, keepdims=True)
        else:
            smax = smax_ref[slot]
        m_prev = m_ref[...]
        m_new = jnp.maximum(m_prev, smax)
        p_ref[slot] = jnp.exp2(s - m_new).astype(BF16)
        alpha_ref[slot] = jnp.exp2(m_prev - m_new)
        m_ref[...] = m_new

    def accumulate(kt, slot):
        p = p_ref[slot]
        v0 = jnp.maximum(kt, 0) * (TK // VT)
        pv = _dot(vst_ref[v0], p[0:VT])
        for c in range(1, TK // VT):
            pv = pv + _dot(vst_ref[v0 + c], p[c * VT:(c + 1) * VT])
        acc_ref[...] = alpha_ref[slot] * acc_ref[...] + pv

    p_ref[odd] = jnp.zeros(p_ref.shape[1:], BF16)
    alpha_ref[odd] = jnp.ones(alpha_ref.shape[1:], F32)
    last = qb // (TK // TQ)
    pairs = last // 2
    n_row_blocks = N_SEL // SUBLANES
    first_blocks = max(TK // SEL_LEN, BF16_ROWS) // SUBLANES
    set_bias(list(range(first_blocks)), score)
    started = logits(0, even)
    set_bias(list(range(first_blocks, n_row_blocks)), score + _zero_after(jnp.concatenate([started] * n_row_blocks, axis=0)))

    def body(jp, carry):
        a = 2 * jp
        accumulate(a - 1, odd)
        softmax(a, even, False)
        logits(a + 1, odd)
        accumulate(a, even)
        softmax(a + 1, odd, False)
        logits(a + 2, even)
        return carry

    lax.fori_loop(0, pairs, body, 0)
    a = 2 * pairs

    @pl.when(last == a)
    def _():
        accumulate(a - 1, odd)
        softmax(a, even, True)
        accumulate(a, even)

    @pl.when(last != a)
    def _():
        accumulate(a - 1, odd)
        softmax(a, even, False)
        logits(a + 1, odd)
        accumulate(a, even)
        softmax(a + 1, odd, True)
        accumulate(a + 1, odd)

    acc = acc_ref[...]
    o_s = acc[0:HEAD_DIM] * (1.0 / acc[HEAD_DIM:HEAD_DIM + 1])

    o_c = oc_ref[...]
    o_w = ow_ref[...]
    heads = []
    for h in range(NSA_HPG):
        cs = slice(h * TQ, (h + 1) * TQ)

        def gate(br):
            return gtt_ref[pl.ds(br * NSA_HEADS + g * NSA_HPG + h, 1), :]

        heads.append(gate(0) * o_c[:, cs] + gate(1) * o_s[:, cs] + gate(2) * o_w[:, cs])
    o_ref[...] = jnp.concatenate(heads, axis=0).T.astype(BF16)


def _attention(qt, kcv, kcvt, sel, ka, kw, vst, vwt, gtt, batch, seq):
    nq = seq // TQ
    cols = NSA_HPG * TQ
    assert seq // SEL_LEN == N_SEL and WIN_TILES * VT <= seq and TK % VT == 0 and TK % TQ == 0
    assert WINDOW % VT == 0 and (TQ % VT == 0 or VT % TQ == 0)
    vt_spec = pl.BlockSpec((seq // VT, LANES, VT), lambda b, g, i: (b, g, 0))
    return pl.pallas_call(
        _attn_kernel,
        grid=(batch, NSA_GROUPS, nq),
        in_specs=[
            pl.BlockSpec((NSA_HPG * HEAD_DIM, TQ), lambda b, g, i: (g, b * nq + i)),
            pl.BlockSpec((1, 1, N_CMP_PAD, LANES), lambda b, g, i: (b, g, 0, 0)),
            pl.BlockSpec((1, 1, LANES, N_CMP_PAD), lambda b, g, i: (b, g, 0, 0)),
            pl.BlockSpec(sel.shape, lambda b, g, i: (0, 0)),
            pl.BlockSpec((seq, LANES), lambda b, g, i: (b, g)),
            pl.BlockSpec((seq, LANES), lambda b, g, i: (b, 0)),
            vt_spec, vt_spec,
            pl.BlockSpec((_ZGN, TQ), lambda b, g, i: (0, b * nq + i)),
        ],
        out_specs=pl.BlockSpec((TQ, NSA_HPG * HEAD_DIM), lambda b, g, i: (b * nq + i, g)),
        out_shape=jax.ShapeDtypeStruct((batch * seq, NSA_HEADS * HEAD_DIM), BF16),
        scratch_shapes=[
            pltpu.VMEM((LANES, cols), BF16),
            pltpu.VMEM((LANES, cols), BF16),
            pltpu.VMEM((LANES, cols), BF16),
            pltpu.VMEM((1, cols), F32),
            pltpu.VMEM((LANES, cols), F32),
            pltpu.VMEM((2, TK, cols), F32),
            pltpu.VMEM((2, 1, cols), F32),
            pltpu.VMEM((2, TK, cols), BF16),
            pltpu.VMEM((2, 1, cols), F32),
            pltpu.VMEM((HEAD_DIM, cols), F32),
            pltpu.VMEM((HEAD_DIM, cols), F32),
        ],
        compiler_params=pltpu.CompilerParams(dimension_semantics=("parallel", "parallel", "arbitrary"),
                                             vmem_limit_bytes=VMEM_LIMIT),
        name="nsa_attention",
    )(qt, kcv, kcvt, sel, ka, kw, vst, vwt, gtt)


def _merge_kernel(o_ref, c_ref, halo_ref, sg_ref, x_ref, wo_ref, cw_ref, cb_ref, lg_ref, lb_ref, wco_ref,
                  bco_ref, wout_ref, h_ref, cext_ref, *, tiles_per_seq):
    first = (pl.program_id(0) % tiles_per_seq) == 0
    cext_ref[0:CONV_HALO, :] = jnp.where(first, 0.0, halo_ref[...])
    cext_ref[CONV_HALO:, :] = c_ref[...]
    shift = CONV_HALO - (CONV_WIDTH - 1)
    acc = jnp.broadcast_to(cb_ref[...], c_ref.shape)
    for res in range(SUBLANES):
        taps = [k for k in range(CONV_WIDTH) if (shift + k) % SUBLANES == res]
        n_rows = TM + (SUBLANES if res else 0)
        part = None
        for k in taps:
            base = shift + k - res
            term = cw_ref[k:k + 1, :] * cext_ref[base:base + n_rows, :]
            part = term if part is None else part + term
        acc = acc + part[res:res + TM]
    xc = acc - jnp.mean(acc, axis=-1, keepdims=True)
    y = xc * lax.rsqrt(jnp.mean(xc * xc, axis=-1, keepdims=True) + EPS) * lg_ref[...] + lb_ref[...]
    y_b = _dot((y * _sigmoid(y)).astype(BF16), wco_ref[...]) + bco_ref[...]
    y_a = _dot(o_ref[...], wo_ref[...])
    d = y_a.shape[-1]
    mix = sg_ref[:, :d] * y_a + sg_ref[:, d:] * y_b
    h_ref[...] = x_ref[...] + _dot(mix.astype(BF16), wout_ref[...])


def _merge(o, c, sg, x2, wo, cw, cb, lg, lb, wco, bco, wout, seq):
    m, d = x2.shape
    ch = c.shape[-1]
    row = lambda n: pl.BlockSpec((TM, n), lambda i: (i, 0))
    halo = pl.BlockSpec((CONV_HALO, ch), lambda i: (jnp.maximum(i * (TM // CONV_HALO) - 1, 0), 0))
    weights = [wo, cw, cb, lg, lb, wco, bco, wout]
    return pl.pallas_call(
        functools.partial(_merge_kernel, tiles_per_seq=seq // TM),
        grid=(m // TM,),
        in_specs=[row(o.shape[-1]), row(ch), halo, row(sg.shape[-1]), row(d)] + [_resident(w.shape) for w in weights],
        out_specs=row(d),
        out_shape=jax.ShapeDtypeStruct((m, d), F32),
        scratch_shapes=[pltpu.VMEM((CONV_HALO + TM, ch), F32)],
        compiler_params=pltpu.CompilerParams(dimension_semantics=("parallel",), vmem_limit_bytes=VMEM_LIMIT),
        name="merge",
    )(o, c, c, sg, x2, *weights)


def _ffn_kernel(h_ref, halo_ref, p_ref, gf_ref, wup_ref, fw_ref, fb_ref, wdn_ref, gp_ref, wpg_ref, wple_ref,
                gfin_ref, out_ref, u_ref, v_ref, *, tiles_per_seq, final_norm):
    first = (pl.program_id(0) % tiles_per_seq) == 0
    h = h_ref[...]
    u_ref[0:FFN_HALO, :] = _rms(jnp.where(first, 0.0, halo_ref[...]), gf_ref[...]).astype(BF16)
    u_ref[FFN_HALO:, :] = _rms(h, gf_ref[...]).astype(BF16)
    v_ref[...] = _dot(u_ref[...], wup_ref[...])
    shift = FFN_HALO - (FFN_CONV_WIDTH - 1)
    v = fb_ref[...]
    for k in range(FFN_CONV_WIDTH):
        v = v + fw_ref[k:k + 1, :] * v_ref[shift + k:shift + k + TM, :]
    d_ff = wdn_ref.shape[0]
    a = _gelu_tanh(v[:, :d_ff]) * v[:, d_ff:]
    h = h + _dot(a.astype(BF16), wdn_ref[...])
    gate = _sigmoid(_dot(_rms(h, gp_ref[...]).astype(BF16), wpg_ref[...]))
    h = h + gate * _dot(p_ref[...].astype(BF16), wple_ref[...])
    out_ref[...] = _rms(h, gfin_ref[...]) if final_norm else h


def _ffn(h1, p3, layer, gf, wup, fw, fb, wdn, gp, wpg, wple, gfin, seq, final_norm):
    m, d = h1.shape
    assert wdn.shape[0] % LANES == 0
    row = lambda n: pl.BlockSpec((TM, n), lambda i: (i, 0))
    halo = pl.BlockSpec((FFN_HALO, d), lambda i: (jnp.maximum(i * (TM // FFN_HALO) - 1, 0), 0))
    weights = [gf, wup, fw, fb, wdn, gp, wpg, wple, gfin]
    return pl.pallas_call(
        functools.partial(_ffn_kernel, tiles_per_seq=seq // TM, final_norm=final_norm),
        grid=(m // TM,),
        in_specs=[row(d), halo, pl.BlockSpec((None, TM, p3.shape[-1]), lambda i: (layer, i, 0))]
        + [_resident(w.shape) for w in weights],
        out_specs=row(d),
        out_shape=jax.ShapeDtypeStruct((m, d), F32),
        scratch_shapes=[pltpu.VMEM((FFN_HALO + TM, d), BF16), pltpu.VMEM((FFN_HALO + TM, wup.shape[1]), F32)],
        compiler_params=pltpu.CompilerParams(dimension_semantics=("parallel",), vmem_limit_bytes=VMEM_LIMIT),
        name="ffn_ple",
    )(h1, h1, p3, *weights)


def _sel_map_t(seq):
    n_cmp = (seq - CMP_LEN) // CMP_STRIDE + 1
    n_sel = seq // SEL_LEN
    c0 = np.arange(n_cmp) * CMP_STRIDE
    j0 = np.arange(n_sel) * SEL_LEN
    lo = np.maximum(c0[None, :], j0[:, None])
    hi = np.minimum(c0[None, :] + CMP_LEN, j0[:, None] + SEL_LEN)
    out = np.zeros((N_SEL, N_CMP_PAD), np.float32)
    out[:n_sel, :n_cmp] = np.maximum(hi - lo, 0) / CMP_LEN
    return jnp.asarray(out, BF16)


def _inproj_weights(w):
    d = w.shape[0]
    nq = NSA_HEADS * HEAD_DIM
    kv0 = nq
    gate0 = kv0 + 3 * 2 * NSA_GROUPS * HEAD_DIM
    conv0 = gate0 + 3 * NSA_HEADS
    merge0 = conv0 + _ZCN

    def kv(br, g):
        lo = kv0 + (br * NSA_GROUPS + g) * HEAD_DIM
        return w[:, lo:lo + HEAD_DIM]

    z = jnp.zeros((d, HEAD_DIM), w.dtype)
    tok = [kv(2, 0), z, kv(2, 1), z, kv(4, 0), kv(4, 1), w[:, kv0:kv0 + _CVN], w[:, conv0:merge0],
           w[:, merge0:merge0 + _ZMN]]
    feat = [w[:, :nq] * Q_SCALE, kv(3, 0), kv(3, 1), kv(5, 0), kv(5, 1), w[:, gate0:conv0],
            jnp.zeros((d, _ZGN - 3 * NSA_HEADS), w.dtype)]
    w_tok = jnp.concatenate(tok, axis=1).astype(BF16)
    w_feat = jnp.concatenate(feat, axis=1).T.astype(BF16)
    assert w_tok.shape[1] == _N_TOK and w_feat.shape[0] == _N_FEAT
    return w_tok, w_feat


def kernel(x, p, g_mix, w_in, cmp_pos_k, cmp_pos_v, w_cmp_k1, w_cmp_k2, w_cmp_v1, w_cmp_v2, w_o_nsa, conv_w,
           conv_b, conv_ln_g, conv_ln_b, w_conv_out, b_conv_out, w_out, g_ffn, w_up, ffn_conv_w, ffn_conv_b,
           w_down, g_ple, w_ple_gate, w_ple, g_final):
    batch, seq, d = x.shape
    depth = w_in.shape[0]
    m = batch * seq
    assert seq % TM == 0 and seq % TK == 0
    sel = _sel_map_t(seq)
    row = lambda v: v.reshape(1, -1)
    h = x.reshape(m, d)
    for i in range(depth):
        w_tok, w_feat = _inproj_weights(w_in[i])
        pos = jnp.concatenate([cmp_pos_k[i]] * NSA_GROUPS + [cmp_pos_v[i]] * NSA_GROUPS, axis=1)
        w1 = jnp.stack([w_cmp_k1[i], w_cmp_v1[i]]).astype(BF16)
        zpad = jnp.zeros_like(w_cmp_k2[i])
        w2 = jnp.stack([jnp.concatenate([w_cmp_k2[i], zpad], axis=1),
                        jnp.concatenate([zpad, w_cmp_v2[i]], axis=1)]).astype(BF16)
        w2t = jnp.stack([jnp.concatenate([zpad, w_cmp_k2[i]], axis=1).T,
                         jnp.concatenate([w_cmp_v2[i], zpad], axis=1).T]).astype(BF16)

        qt, ka, kw, cv, c, sg, vst, vwt, gtt = _inproj(h, row(g_mix[i]), w_tok, w_feat, seq)
        kcv, kcvt = _compress(cv, pos, w1, w2, w2t, batch, seq)
        o = _attention(qt, kcv, kcvt, sel, ka, kw, vst, vwt, gtt, batch, seq)
        h = _merge(o, c, sg, h, w_o_nsa[i].astype(BF16), conv_w[i], row(conv_b[i]), row(conv_ln_g[i]),
                   row(conv_ln_b[i]), w_conv_out[i].astype(BF16), row(b_conv_out[i]), w_out[i].astype(BF16), seq)
        h = _ffn(h, p.reshape(depth, m, -1), i, row(g_ffn[i]), w_up[i].astype(BF16), ffn_conv_w[i], row(ffn_conv_b[i]),
                 w_down[i].astype(BF16), row(g_ple[i]), w_ple_gate[i].astype(BF16), w_ple[i].astype(BF16),
                 row(g_final), seq, final_norm=(i == depth - 1))
    return h.reshape(batch, seq, d)
```

```python
import functools

import numpy as np
import jax
import jax.numpy as jnp
from jax import lax
from jax.experimental import pallas as pl
from jax.experimental.pallas import tpu as pltpu

F32 = jnp.float32
BF16 = jnp.bfloat16

NSA_HEADS = 8
NSA_GROUPS = 2
NSA_HPG = NSA_HEADS // NSA_GROUPS
HEAD_DIM = 64
CMP_LEN = 32
CMP_STRIDE = 16
SEL_LEN = 64
SEL_SHIFT = 6
SEL_TOP = 16
WINDOW = 512
FORCE_SCORE = 1e4
CONV_WIDTH = 31
FFN_CONV_WIDTH = 3
EPS = 1e-6
NEG = -1e30
MASK_BIAS = -1e9
Q_SCALE = HEAD_DIM ** -0.5 * float(np.log2(np.e))

LANES = 128
SUBLANES = 8
BF16_ROWS = 16
VMEM_LIMIT = 56 * 1024 * 1024

TM = 512
TQ = 256
TK = 512
VT = 256
WIN_TILES = (WINDOW + TQ) // VT + (1 if TQ % VT else 0)
CONV_HALO = 32
FFN_HALO = BF16_ROWS


def _sigmoid(x):
    return 0.5 * (jnp.tanh(0.5 * x) + 1.0)


def _gelu_tanh(x):
    return 0.5 * x * (1.0 + jnp.tanh(np.sqrt(2.0 / np.pi).astype(np.float32) * (x + 0.044715 * (x * x * x))))


def _rms(x, g):
    return x * lax.rsqrt(jnp.mean(x * x, axis=-1, keepdims=True) + EPS) * g


def _zero_after(x):
    bits = lax.bitcast_convert_type(x, jnp.uint32)
    bits = lax.shift_right_logical(lax.shift_right_logical(bits, jnp.uint32(16)), jnp.uint32(16))
    return lax.bitcast_convert_type(bits, F32)


def _dot(a, b):
    return jnp.dot(a, b, preferred_element_type=F32)


def _dot_nt(a, b):
    return lax.dot_general(a, b, (((1,), (1,)), ((), ())), preferred_element_type=F32)


def _resident(shape):
    nd = len(shape)
    return pl.BlockSpec(shape, lambda *_: (0,) * nd, pipeline_mode=pl.Buffered(1))


_KA0, _KAN = 0, 256
_KW0, _KWN = 256, 128
_CV0, _CVN = 384, 256
_ZC0, _ZCN = 640, 1024
_ZM0, _ZMN = 1664, 2048
_N_TOK = 3712
_Q0, _QN = 0, 512
_VS0, _VSN = 512, 128
_VW0, _VWN = 640, 128
_ZG0, _ZGN = 768, 32
_N_FEAT = 800


def _inproj_kernel(x_ref, g_ref, wt_ref, wf_ref, qt_ref, ka_ref, kw_ref, cv_ref, c_ref, sg_ref, vst_ref, vwt_ref,
                   gtt_ref, *, tiles_per_seq):
    u = _rms(x_ref[...], g_ref[...]).astype(BF16)

    z_tok = _dot(u, wt_ref[...])
    z_feat = _dot_nt(wf_ref[...], u)

    def tok(lo, n):
        return z_tok[:, lo:lo + n]

    def feat(lo, n):
        return z_feat[lo:lo + n, :]

    ka = tok(_KA0, _KAN)
    s0 = (pl.program_id(0) % tiles_per_seq) * TM
    row = lax.broadcasted_iota(jnp.int32, ka.shape, 0) + s0
    col = lax.broadcasted_iota(jnp.int32, ka.shape, 1)
    onehot = ((col & HEAD_DIM) != 0) & ((col & (HEAD_DIM - 1)) == (row >> SEL_SHIFT))
    ka_ref[...] = jnp.where(onehot, 1.0, ka).astype(BF16)
    kw_ref[...] = tok(_KW0, _KWN).astype(BF16)
    cv_ref[...] = tok(_CV0, _CVN)
    zc = tok(_ZC0, _ZCN)
    half = _ZCN // 2
    c_ref[...] = zc[:, :half] * _sigmoid(zc[:, half:])
    sg_ref[...] = _sigmoid(tok(_ZM0, _ZMN))

    qt_ref[...] = feat(_Q0, _QN).astype(BF16)
    gtt_ref[...] = _sigmoid(feat(_ZG0, _ZGN))
    ones = jnp.ones((HEAD_DIM, VT), BF16)
    for src0, dst_ref in ((_VS0, vst_ref), (_VW0, vwt_ref)):
        v = feat(src0, NSA_GROUPS * HEAD_DIM).astype(BF16)
        for t in range(TM // VT):
            for g in range(NSA_GROUPS):
                r0 = g * 2 * HEAD_DIM
                dst_ref[t, r0:r0 + HEAD_DIM, :] = v[g * HEAD_DIM:(g + 1) * HEAD_DIM, t * VT:(t + 1) * VT]
                dst_ref[t, r0 + HEAD_DIM:r0 + 2 * HEAD_DIM, :] = ones


def _inproj(x2, g_mix, w_tok, w_feat, seq):
    m, d = x2.shape
    row = lambda n: pl.BlockSpec((TM, n), lambda i: (i, 0))
    colb = lambda n: pl.BlockSpec((n, TM), lambda i: (0, i))
    vt_spec = pl.BlockSpec((TM // VT, NSA_GROUPS * LANES, VT), lambda i: (i, 0, 0))
    vt_shape = jax.ShapeDtypeStruct((m // VT, NSA_GROUPS * LANES, VT), BF16)
    tok_outs = [(_KAN, BF16), (_KWN, BF16), (_CVN, F32), (_ZCN // 2, F32), (_ZMN, F32)]
    return pl.pallas_call(
        functools.partial(_inproj_kernel, tiles_per_seq=seq // TM),
        grid=(m // TM,),
        in_specs=[row(d), _resident((1, d)), _resident(w_tok.shape), _resident(w_feat.shape)],
        out_specs=[colb(_QN)] + [row(n) for n, _ in tok_outs] + [vt_spec, vt_spec, colb(_ZGN)],
        out_shape=([jax.ShapeDtypeStruct((_QN, m), BF16)]
                   + [jax.ShapeDtypeStruct((m, n), dt) for n, dt in tok_outs]
                   + [vt_shape, vt_shape, jax.ShapeDtypeStruct((_ZGN, m), F32)]),
        compiler_params=pltpu.CompilerParams(dimension_semantics=("parallel",), vmem_limit_bytes=VMEM_LIMIT),
        name="inproj",
    )(x2, g_mix, w_tok, w_feat)


N_CMP_PAD = 256
_CMP_HALF = CMP_LEN // 2


def _compress_kernel(ck_ref, cv_ref, pos_ref, w1_ref, w2_ref, w2t_ref, out_ref, outt_ref, *, n_cmp):
    hid = w1_ref.shape[-1]
    first = [jnp.zeros((N_CMP_PAD, hid), F32) for _ in range(4)]
    second = [jnp.zeros((N_CMP_PAD, hid), F32) for _ in range(4)]
    for l in range(_CMP_HALF):
        for kind, src_ref in enumerate((ck_ref, cv_ref)):
            xl = src_ref[pl.ds(l, N_CMP_PAD, stride=CMP_STRIDE), :]
            pcol = slice(kind * LANES, (kind + 1) * LANES)
            xa = (xl + pos_ref[l:l + 1, pcol]).astype(BF16)
            xb = (xl + pos_ref[_CMP_HALF + l:_CMP_HALF + l + 1, pcol]).astype(BF16)
            wa = w1_ref[kind, l * HEAD_DIM:(l + 1) * HEAD_DIM, :]
            wb = w1_ref[kind, (_CMP_HALF + l) * HEAD_DIM:(_CMP_HALF + l + 1) * HEAD_DIM, :]
            for g in range(NSA_GROUPS):
                kg = kind * NSA_GROUPS + g
                cols = slice(g * HEAD_DIM, (g + 1) * HEAD_DIM)
                first[kg] = first[kg] + _dot(xa[:, cols], wa)
                second[kg] = second[kg] + _dot(xb[:, cols], wb)
    rows = lax.broadcasted_iota(jnp.int32, (N_CMP_PAD, LANES), 0)
    lanes = lax.broadcasted_iota(jnp.int32, (LANES, N_CMP_PAD), 1)
    for g in range(NSA_GROUPS):
        acts = []
        for kind in range(2):
            kg = kind * NSA_GROUPS + g
            h1 = first[kg] + pltpu.roll(second[kg], N_CMP_PAD - 1, 0)
            acts.append(_gelu_tanh(h1).astype(BF16))
        kcv = _dot(acts[0], w2_ref[0]) + _dot(acts[1], w2_ref[1])
        kcvt = _dot_nt(w2t_ref[0], acts[0]) + _dot_nt(w2t_ref[1], acts[1])
        out_ref[0, g] = jnp.where(rows < n_cmp, kcv, 0.0).astype(BF16)
        outt_ref[0, g] = jnp.where(lanes < n_cmp, kcvt, 0.0).astype(BF16)


def _compress(cv, pos, w1, w2, w2t, batch, seq):
    n_cmp = (seq - CMP_LEN) // CMP_STRIDE + 1
    assert n_cmp < N_CMP_PAD and seq == N_CMP_PAD * CMP_STRIDE
    return pl.pallas_call(
        functools.partial(_compress_kernel, n_cmp=n_cmp),
        grid=(batch,),
        in_specs=[pl.BlockSpec((seq, LANES), lambda b: (b, 0)), pl.BlockSpec((seq, LANES), lambda b: (b, 1)),
                  _resident(pos.shape), _resident(w1.shape), _resident(w2.shape), _resident(w2t.shape)],
        out_specs=[pl.BlockSpec((1, NSA_GROUPS, N_CMP_PAD, LANES), lambda b: (b, 0, 0, 0)),
                   pl.BlockSpec((1, NSA_GROUPS, LANES, N_CMP_PAD), lambda b: (b, 0, 0, 0))],
        out_shape=[jax.ShapeDtypeStruct((batch, NSA_GROUPS, N_CMP_PAD, LANES), BF16),
                   jax.ShapeDtypeStruct((batch, NSA_GROUPS, LANES, N_CMP_PAD), BF16)],
        compiler_params=pltpu.CompilerParams(dimension_semantics=("parallel",), vmem_limit_bytes=VMEM_LIMIT),
        name="compress",
    )(cv, cv, pos, w1, w2, w2t)


N_SEL = 64


def _attn_kernel(qt_ref, kcv_ref, kcvt_ref, sel_ref, ka_ref, kw_ref, vst_ref, vwt_ref, gtt_ref, o_ref,
                 q0_ref, q1_ref, qs_ref, m_ref, acc_ref, s_ref, smax_ref, p_ref, alpha_ref, oc_ref, ow_ref):
    g = pl.program_id(1)
    qb = pl.program_id(2)
    t0 = qb * TQ
    cols = NSA_HPG * TQ

    zeros = jnp.zeros((HEAD_DIM, TQ), BF16)
    for h in range(NSA_HPG):
        qh = qt_ref[h * HEAD_DIM:(h + 1) * HEAD_DIM, :]
        cs = slice(h * TQ, (h + 1) * TQ)
        q0_ref[0:HEAD_DIM, cs] = qh
        q0_ref[HEAD_DIM:, cs] = zeros
        q1_ref[0:HEAD_DIM, cs] = zeros
        q1_ref[HEAD_DIM:, cs] = qh
        qs_ref[0:HEAD_DIM, cs] = qh
        qs_ref[HEAD_DIM:, cs] = zeros
    q0 = q0_ref[...]
    tq = t0 + (lax.broadcasted_iota(jnp.int32, (1, cols), 1) & (TQ - 1))

    sc = _dot(kcv_ref[0, 0], q0)
    wt0 = jnp.maximum(t0 - WINDOW, 0) // VT
    kstart = pl.multiple_of(wt0 * VT, VT)
    span = WIN_TILES * VT
    qwin = jnp.where(g == 0, q0, q1_ref[...])
    sw = _dot(kw_ref[pl.ds(kstart, span), :], qwin)

    tq1 = t0 + lax.broadcasted_iota(jnp.int32, (1, TQ), 1)

    def per_head(x):
        return jnp.concatenate([x] * NSA_HPG, axis=1)

    cend = lax.broadcasted_iota(jnp.int32, (N_CMP_PAD, 1), 0) * CMP_STRIDE + (CMP_LEN - 1)
    sc = sc + per_head(jnp.where(cend <= tq1, 0.0, NEG))
    ec = jnp.exp2(sc - jnp.max(sc, axis=0, keepdims=True))
    lc = jnp.sum(ec, axis=0, keepdims=True)
    visible = tq >= CMP_LEN - 1
    dist = tq1 - (kstart + lax.broadcasted_iota(jnp.int32, (span, 1), 0))
    sw = sw + per_head(jnp.where(dist.astype(jnp.uint32) < WINDOW, 0.0, NEG))
    wmax = jnp.max(sw, axis=0, keepdims=True)
    pc = ec * (jnp.where(visible, 1.0 / lc, 0.0) + _zero_after(wmax))
    oc_ref[...] = _dot(kcvt_ref[0, 0], pc.astype(BF16))[0:HEAD_DIM]

    psum = pc[:, 0:TQ]
    for h in range(1, NSA_HPG):
        psum = psum + pc[:, h * TQ:(h + 1) * TQ]
    sel = sel_ref[...]
    p_hi = psum.astype(BF16)
    rem = psum - p_hi.astype(F32)
    p_mid = rem.astype(BF16)
    p_lo = (rem - p_mid.astype(F32)).astype(BF16)
    imp = (_dot(sel, p_hi) + _dot(sel, p_mid)) + _dot(sel, p_lo)

    pw = jnp.exp2(sw - wmax).astype(BF16)
    ow = _dot(jnp.concatenate([vwt_ref[wt0 + c] for c in range(WIN_TILES)], axis=1), pw)
    ow_ref[...] = ow[0:HEAD_DIM] * (1.0 / ow[HEAD_DIM:HEAD_DIM + 1])

    j = lax.broadcasted_iota(jnp.int32, (N_SEL, TQ), 0)
    cur = (t0 + lax.broadcasted_iota(jnp.int32, (N_SEL, TQ), 1)) >> SEL_SHIFT

    forced = (j == 0) | (j == cur) | (j == cur - 1)
    score = jnp.where(j <= cur, jnp.where(forced, FORCE_SCORE, imp), NEG)
    jloc = lax.broadcasted_iota(jnp.int32, (SUBLANES, TQ), 0)

    def set_bias(row_blocks, score):
        ranks = []
        for r in row_blocks:
            blk = score[r * SUBLANES:(r + 1) * SUBLANES]
            rank = jnp.zeros((SUBLANES, TQ), jnp.int32)
            for i in range(N_SEL):
                vi = jnp.broadcast_to(score[i:i + 1, :], (SUBLANES, TQ))
                if r * SUBLANES > i:
                    before = vi >= blk
                elif (r + 1) * SUBLANES - 1 <= i:
                    before = vi > blk
                else:
                    before = (vi > blk) | ((vi == blk) & (jloc > i - r * SUBLANES))
                rank = rank + before.astype(jnp.int32)
            ranks.append(rank)
        bias = jnp.where(jnp.concatenate(ranks, axis=0) < SEL_TOP, 0.0, MASK_BIAS).astype(BF16)
        lo = HEAD_DIM + row_blocks[0] * SUBLANES
        for h in range(NSA_HPG):
            qs_ref[lo:lo + bias.shape[0], h * TQ:(h + 1) * TQ] = bias

    m_ref[...] = jnp.full(m_ref.shape, NEG, F32)
    acc_ref[...] = jnp.zeros(acc_ref.shape, F32)
    ksub = lax.broadcasted_iota(jnp.int32, (TK, 1), 0)
    even, odd = 0, 1

    def logits(kt, slot):
        k0 = pl.multiple_of(kt * TK, TK)
        s = _dot(ka_ref[pl.ds(k0, TK), :], qs_ref[...])
        s_ref[slot] = s
        smax_ref[slot] = jnp.max(s, axis=0, keepdims=True)
        return s[0:SUBLANES, 0:TQ]

    def softmax(kt, slot, causal, after=None):
        s = s_ref[slot]
        if causal:
            s = jnp.where(kt * TK + ksub <= tq, s, NEG)
            smax = jnp.max(s, axis=0, keepdims=True)
        else:
            smax = smax_ref[slot]
        m_prev = m_ref[...]
        m_new = jnp.maximum(m_prev, smax)
        p_ref[slot] = jnp.exp2(s - m_new).astype(BF16)
        alpha = jnp.exp2(m_prev - m_new)
        if after is not None:
            alpha = alpha + _zero_after(per_head(after[0:1, :]))
        alpha_ref[slot] = alpha
        m_ref[...] = m_new

    def accumulate(kt, slot):
        p = p_ref[slot]
        v0 = jnp.maximum(kt, 0) * (TK // VT)
        v = jnp.concatenate([vst_ref[v0 + c] for c in range(TK // VT)], axis=1)
        acc_ref[...] = alpha_ref[slot] * acc_ref[...] + _dot(v, p)

    p_ref[odd] = jnp.zeros(p_ref.shape[1:], BF16)
    alpha_ref[odd] = jnp.ones(alpha_ref.shape[1:], F32)
    last = qb // (TK // TQ)
    pairs = last // 2
    n_row_blocks = N_SEL // SUBLANES
    first_blocks = max(TK // SEL_LEN, BF16_ROWS) // SUBLANES
    set_bias(list(range(first_blocks)), score)
    started = logits(0, even)
    set_bias(list(range(first_blocks, n_row_blocks)), score + _zero_after(jnp.concatenate([started] * n_row_blocks, axis=0)))

    def body(jp, carry):
        a = 2 * jp
        accumulate(a - 1, odd)
        softmax(a, even, False, after=logits(a + 1, odd))
        accumulate(a, even)
        softmax(a + 1, odd, False, after=logits(a + 2, even))
        return carry

    lax.fori_loop(0, pairs, body, 0)
    a = 2 * pairs

    @pl.when(last == a)
    def _():
        accumulate(a - 1, odd)
        softmax(a, even, True)
        accumulate(a, even)

    @pl.when(last != a)
    def _():
        accumulate(a - 1, odd)
        softmax(a, even, False, after=logits(a + 1, odd))
        accumulate(a, even)
        softmax(a + 1, odd, True)
        accumulate(a + 1, odd)

    acc = acc_ref[...]
    o_s = acc[0:HEAD_DIM] * (1.0 / acc[HEAD_DIM:HEAD_DIM + 1])

    o_c = oc_ref[...]
    o_w = ow_ref[...]
    heads = []
    for h in range(NSA_HPG):
        cs = slice(h * TQ, (h + 1) * TQ)

        def gate(br):
            return gtt_ref[pl.ds(br * NSA_HEADS + g * NSA_HPG + h, 1), :]

        heads.append(gate(0) * o_c[:, cs] + gate(1) * o_s[:, cs] + gate(2) * o_w[:, cs])
    o_ref[...] = jnp.concatenate(heads, axis=0).T.astype(BF16)


def _attention(qt, kcv, kcvt, sel, ka, kw, vst, vwt, gtt, batch, seq):
    nq = seq // TQ
    cols = NSA_HPG * TQ
    assert seq // SEL_LEN == N_SEL and WIN_TILES * VT <= seq and TK % VT == 0 and TK % TQ == 0
    assert WINDOW % VT == 0 and (TQ % VT == 0 or VT % TQ == 0)
    vt_spec = pl.BlockSpec((seq // VT, LANES, VT), lambda b, g, i: (b, g, 0))
    return pl.pallas_call(
        _attn_kernel,
        grid=(batch, NSA_GROUPS, nq),
        in_specs=[
            pl.BlockSpec((NSA_HPG * HEAD_DIM, TQ), lambda b, g, i: (g, b * nq + i)),
            pl.BlockSpec((1, 1, N_CMP_PAD, LANES), lambda b, g, i: (b, g, 0, 0)),
            pl.BlockSpec((1, 1, LANES, N_CMP_PAD), lambda b, g, i: (b, g, 0, 0)),
            pl.BlockSpec(sel.shape, lambda b, g, i: (0, 0)),
            pl.BlockSpec((seq, LANES), lambda b, g, i: (b, g)),
            pl.BlockSpec((seq, LANES), lambda b, g, i: (b, 0)),
            vt_spec, vt_spec,
            pl.BlockSpec((_ZGN, TQ), lambda b, g, i: (0, b * nq + i)),
        ],
        out_specs=pl.BlockSpec((TQ, NSA_HPG * HEAD_DIM), lambda b, g, i: (b * nq + i, g)),
        out_shape=jax.ShapeDtypeStruct((batch * seq, NSA_HEADS * HEAD_DIM), BF16),
        scratch_shapes=[
            pltpu.VMEM((LANES, cols), BF16),
            pltpu.VMEM((LANES, cols), BF16),
            pltpu.VMEM((LANES, cols), BF16),
            pltpu.VMEM((1, cols), F32),
            pltpu.VMEM((LANES, cols), F32),
            pltpu.VMEM((2, TK, cols), F32),
            pltpu.VMEM((2, 1, cols), F32),
            pltpu.VMEM((2, TK, cols), BF16),
            pltpu.VMEM((2, 1, cols), F32),
            pltpu.VMEM((HEAD_DIM, cols), F32),
            pltpu.VMEM((HEAD_DIM, cols), F32),
        ],
        compiler_params=pltpu.CompilerParams(dimension_semantics=("parallel", "parallel", "arbitrary"),
                                             vmem_limit_bytes=VMEM_LIMIT),
        name="nsa_attention",
    )(qt, kcv, kcvt, sel, ka, kw, vst, vwt, gtt)


def _merge_kernel(o_ref, c_ref, halo_ref, sg_ref, x_ref, wo_ref, cw_ref, cb_ref, lg_ref, lb_ref, wco_ref,
                  bco_ref, wout_ref, h_ref, cext_ref, *, tiles_per_seq):
    first = (pl.program_id(0) % tiles_per_seq) == 0
    cext_ref[0:CONV_HALO, :] = jnp.where(first, 0.0, halo_ref[...])
    cext_ref[CONV_HALO:, :] = c_ref[...]
    shift = CONV_HALO - (CONV_WIDTH - 1)
    acc = jnp.broadcast_to(cb_ref[...], c_ref.shape)
    for res in range(SUBLANES):
        taps = [k for k in range(CONV_WIDTH) if (shift + k) % SUBLANES == res]
        n_rows = TM + (SUBLANES if res else 0)
        part = None
        for k in taps:
            base = shift + k - res
            term = cw_ref[k:k + 1, :] * cext_ref[base:base + n_rows, :]
            part = term if part is None else part + term
        acc = acc + part[res:res + TM]
    xc = acc - jnp.mean(acc, axis=-1, keepdims=True)
    y = xc * lax.rsqrt(jnp.mean(xc * xc, axis=-1, keepdims=True) + EPS) * lg_ref[...] + lb_ref[...]
    y_b = _dot((y * _sigmoid(y)).astype(BF16), wco_ref[...]) + bco_ref[...]
    y_a = _dot(o_ref[...], wo_ref[...])
    d = y_a.shape[-1]
    mix = sg_ref[:, :d] * y_a + sg_ref[:, d:] * y_b
    h_ref[...] = x_ref[...] + _dot(mix.astype(BF16), wout_ref[...])


def _merge(o, c, sg, x2, wo, cw, cb, lg, lb, wco, bco, wout, seq):
    m, d = x2.shape
    ch = c.shape[-1]
    row = lambda n: pl.BlockSpec((TM, n), lambda i: (i, 0))
    halo = pl.BlockSpec((CONV_HALO, ch), lambda i: (jnp.maximum(i * (TM // CONV_HALO) - 1, 0), 0))
    weights = [wo, cw, cb, lg, lb, wco, bco, wout]
    return pl.pallas_call(
        functools.partial(_merge_kernel, tiles_per_seq=seq // TM),
        grid=(m // TM,),
        in_specs=[row(o.shape[-1]), row(ch), halo, row(sg.shape[-1]), row(d)] + [_resident(w.shape) for w in weights],
        out_specs=row(d),
        out_shape=jax.ShapeDtypeStruct((m, d), F32),
        scratch_shapes=[pltpu.VMEM((CONV_HALO + TM, ch), F32)],
        compiler_params=pltpu.CompilerParams(dimension_semantics=("parallel",), vmem_limit_bytes=VMEM_LIMIT),
        name="merge",
    )(o, c, c, sg, x2, *weights)


def _ffn_kernel(h_ref, halo_ref, p_ref, gf_ref, wup_ref, fw_ref, fb_ref, wdn_ref, gp_ref, wpg_ref, wple_ref,
                gfin_ref, out_ref, u_ref, v_ref, *, tiles_per_seq, final_norm):
    first = (pl.program_id(0) % tiles_per_seq) == 0
    h = h_ref[...]
    u_ref[0:FFN_HALO, :] = _rms(jnp.where(first, 0.0, halo_ref[...]), gf_ref[...]).astype(BF16)
    u_ref[FFN_HALO:, :] = _rms(h, gf_ref[...]).astype(BF16)
    v_ref[...] = _dot(u_ref[...], wup_ref[...])
    shift = FFN_HALO - (FFN_CONV_WIDTH - 1)
    v = fb_ref[...]
    for k in range(FFN_CONV_WIDTH):
        v = v + fw_ref[k:k + 1, :] * v_ref[shift + k:shift + k + TM, :]
    d_ff = wdn_ref.shape[0]
    a = _gelu_tanh(v[:, :d_ff]) * v[:, d_ff:]
    h = h + _dot(a.astype(BF16), wdn_ref[...])
    gate = _sigmoid(_dot(_rms(h, gp_ref[...]).astype(BF16), wpg_ref[...]))
    h = h + gate * _dot(p_ref[...].astype(BF16), wple_ref[...])
    out_ref[...] = _rms(h, gfin_ref[...]) if final_norm else h


def _ffn(h1, p3, layer, gf, wup, fw, fb, wdn, gp, wpg, wple, gfin, seq, final_norm):
    m, d = h1.shape
    assert wdn.shape[0] % LANES == 0
    row = lambda n: pl.BlockSpec((TM, n), lambda i: (i, 0))
    halo = pl.BlockSpec((FFN_HALO, d), lambda i: (jnp.maximum(i * (TM // FFN_HALO) - 1, 0), 0))
    weights = [gf, wup, fw, fb, wdn, gp, wpg, wple, gfin]
    return pl.pallas_call(
        functools.partial(_ffn_kernel, tiles_per_seq=seq // TM, final_norm=final_norm),
        grid=(m // TM,),
        in_specs=[row(d), halo, pl.BlockSpec((None, TM, p3.shape[-1]), lambda i: (layer, i, 0))]
        + [_resident(w.shape) for w in weights],
        out_specs=row(d),
        out_shape=jax.ShapeDtypeStruct((m, d), F32),
        scratch_shapes=[pltpu.VMEM((FFN_HALO + TM, d), BF16), pltpu.VMEM((FFN_HALO + TM, wup.shape[1]), F32)],
        compiler_params=pltpu.CompilerParams(dimension_semantics=("parallel",), vmem_limit_bytes=VMEM_LIMIT),
        name="ffn_ple",
    )(h1, h1, p3, *weights)


def _sel_map_t(seq):
    n_cmp = (seq - CMP_LEN) // CMP_STRIDE + 1
    n_sel = seq // SEL_LEN
    c0 = np.arange(n_cmp) * CMP_STRIDE
    j0 = np.arange(n_sel) * SEL_LEN
    lo = np.maximum(c0[None, :], j0[:, None])
    hi = np.minimum(c0[None, :] + CMP_LEN, j0[:, None] + SEL_LEN)
    out = np.zeros((N_SEL, N_CMP_PAD), np.float32)
    out[:n_sel, :n_cmp] = np.maximum(hi - lo, 0) / CMP_LEN
    return jnp.asarray(out, BF16)


def _inproj_weights(w):
    d = w.shape[0]
    nq = NSA_HEADS * HEAD_DIM
    kv0 = nq
    gate0 = kv0 + 3 * 2 * NSA_GROUPS * HEAD_DIM
    conv0 = gate0 + 3 * NSA_HEADS
    merge0 = conv0 + _ZCN

    def kv(br, g):
        lo = kv0 + (br * NSA_GROUPS + g) * HEAD_DIM
        return w[:, lo:lo + HEAD_DIM]

    z = jnp.zeros((d, HEAD_DIM), w.dtype)
    tok = [kv(2, 0), z, kv(2, 1), z, kv(4, 0), kv(4, 1), w[:, kv0:kv0 + _CVN], w[:, conv0:merge0],
           w[:, merge0:merge0 + _ZMN]]
    feat = [w[:, :nq] * Q_SCALE, kv(3, 0), kv(3, 1), kv(5, 0), kv(5, 1), w[:, gate0:conv0],
            jnp.zeros((d, _ZGN - 3 * NSA_HEADS), w.dtype)]
    w_tok = jnp.concatenate(tok, axis=1).astype(BF16)
    w_feat = jnp.concatenate(feat, axis=1).T.astype(BF16)
    assert w_tok.shape[1] == _N_TOK and w_feat.shape[0] == _N_FEAT
    return w_tok, w_feat


def kernel(x, p, g_mix, w_in, cmp_pos_k, cmp_pos_v, w_cmp_k1, w_cmp_k2, w_cmp_v1, w_cmp_v2, w_o_nsa, conv_w,
           conv_b, conv_ln_g, conv_ln_b, w_conv_out, b_conv_out, w_out, g_ffn, w_up, ffn_conv_w, ffn_conv_b,
           w_down, g_ple, w_ple_gate, w_ple, g_final):
    batch, seq, d = x.shape
    depth = w_in.shape[0]
    m = batch * seq
    assert seq % TM == 0 and seq % TK == 0
    sel = _sel_map_t(seq)
    row = lambda v: v.reshape(1, -1)
    h = x.reshape(m, d)
    for i in range(depth):
        w_tok, w_feat = _inproj_weights(w_in[i])
        pos = jnp.concatenate([cmp_pos_k[i]] * NSA_GROUPS + [cmp_pos_v[i]] * NSA_GROUPS, axis=1)
        w1 = jnp.stack([w_cmp_k1[i], w_cmp_v1[i]]).astype(BF16)
        zpad = jnp.zeros_like(w_cmp_k2[i])
        w2 = jnp.stack([jnp.concatenate([w_cmp_k2[i], zpad], axis=1),
                        jnp.concatenate([zpad, w_cmp_v2[i]], axis=1)]).astype(BF16)
        w2t = jnp.stack([jnp.concatenate([zpad, w_cmp_k2[i]], axis=1).T,
                         jnp.concatenate([w_cmp_v2[i], zpad], axis=1).T]).astype(BF16)

        qt, ka, kw, cv, c, sg, vst, vwt, gtt = _inproj(h, row(g_mix[i]), w_tok, w_feat, seq)
        kcv, kcvt = _compress(cv, pos, w1, w2, w2t, batch, seq)
        o = _attention(qt, kcv, kcvt, sel, ka, kw, vst, vwt, gtt, batch, seq)
        h = _merge(o, c, sg, h, w_o_nsa[i].astype(BF16), conv_w[i], row(conv_b[i]), row(conv_ln_g[i]),
                   row(conv_ln_b[i]), w_conv_out[i].astype(BF16), row(b_conv_out[i]), w_out[i].astype(BF16), seq)
        h = _ffn(h, p.reshape(depth, m, -1), i, row(g_ffn[i]), w_up[i].astype(BF16), ffn_conv_w[i], row(ffn_conv_b[i]),
                 w_down[i].astype(BF16), row(g_ple[i]), w_ple_gate[i].astype(BF16), w_ple[i].astype(BF16),
                 row(g_final), seq, final_norm=(i == depth - 1))
    return h.reshape(batch, seq, d)
```

```python
import functools

import numpy as np
import jax
import jax.numpy as jnp
from jax import lax
from jax.experimental import pallas as pl
from jax.experimental.pallas import tpu as pltpu

F32 = jnp.float32
BF16 = jnp.bfloat16

NSA_HEADS = 8
NSA_GROUPS = 2
NSA_HPG = NSA_HEADS // NSA_GROUPS
HEAD_DIM = 64
CMP_LEN = 32
CMP_STRIDE = 16
SEL_LEN = 64
SEL_SHIFT = 6
SEL_TOP = 16
WINDOW = 512
FORCE_SCORE = 1e4
CONV_WIDTH = 31
FFN_CONV_WIDTH = 3
EPS = 1e-6
NEG = -1e30
MASK_BIAS = -1e9
Q_SCALE = HEAD_DIM ** -0.5 * float(np.log2(np.e))

LANES = 128
SUBLANES = 8
BF16_ROWS = 16
VMEM_LIMIT = 56 * 1024 * 1024

TM = 1024
TM_FFN = 512
TQ = 256
TK = 256
VT = 256
WIN_TILES = (WINDOW + TQ) // VT + (1 if TQ % VT else 0)
CONV_HALO = 32
FFN_HALO = BF16_ROWS


def _sigmoid(x):
    return 0.5 * (jnp.tanh(0.5 * x) + 1.0)


def _gelu_tanh(x):
    return 0.5 * x * (1.0 + jnp.tanh(np.sqrt(2.0 / np.pi).astype(np.float32) * (x + 0.044715 * (x * x * x))))


def _rms(x, g):
    return x * lax.rsqrt(jnp.mean(x * x, axis=-1, keepdims=True) + EPS) * g


def _zero_after(x):
    bits = lax.bitcast_convert_type(x, jnp.uint32)
    bits = lax.shift_right_logical(lax.shift_right_logical(bits, jnp.uint32(16)), jnp.uint32(16))
    return lax.bitcast_convert_type(bits, F32)


def _dot(a, b):
    return jnp.dot(a, b, preferred_element_type=F32)


def _dot_nt(a, b):
    return lax.dot_general(a, b, (((1,), (1,)), ((), ())), preferred_element_type=F32)


def _resident(shape):
    nd = len(shape)
    return pl.BlockSpec(shape, lambda *_: (0,) * nd, pipeline_mode=pl.Buffered(1))


_KA0, _KAN = 0, 256
_KW0, _KWN = 256, 128
_CV0, _CVN = 384, 256
_ZC0, _ZCN = 640, 1024
_ZM0, _ZMN = 1664, 2048
_N_TOK = 3712
_Q0, _QN = 0, 512
_VS0, _VSN = 512, 128
_VW0, _VWN = 640, 128
_ZG0, _ZGN = 768, 32
_N_FEAT = 800


def _inproj_kernel(x_ref, g_ref, wt_ref, wf_ref, qt_ref, ka_ref, kw_ref, cv_ref, c_ref, sg_ref, vst_ref, vwt_ref,
                   gtt_ref, *, tiles_per_seq):
    u = _rms(x_ref[...], g_ref[...]).astype(BF16)

    z_tok = _dot(u, wt_ref[...])
    z_feat = _dot_nt(wf_ref[...], u)

    def tok(lo, n):
        return z_tok[:, lo:lo + n]

    def feat(lo, n):
        return z_feat[lo:lo + n, :]

    ka = tok(_KA0, _KAN)
    s0 = (pl.program_id(0) % tiles_per_seq) * TM
    row = lax.broadcasted_iota(jnp.int32, ka.shape, 0) + s0
    col = lax.broadcasted_iota(jnp.int32, ka.shape, 1)
    onehot = ((col & HEAD_DIM) != 0) & ((col & (HEAD_DIM - 1)) == (row >> SEL_SHIFT))
    ka_ref[...] = jnp.where(onehot, 1.0, ka).astype(BF16)
    kw_ref[...] = tok(_KW0, _KWN).astype(BF16)
    cv_ref[...] = tok(_CV0, _CVN)
    zc = tok(_ZC0, _ZCN)
    half = _ZCN // 2
    c_ref[...] = zc[:, :half] * _sigmoid(zc[:, half:])
    sg_ref[...] = _sigmoid(tok(_ZM0, _ZMN))

    qt_ref[...] = feat(_Q0, _QN).astype(BF16)
    gtt_ref[...] = _sigmoid(feat(_ZG0, _ZGN))
    ones = jnp.ones((HEAD_DIM, VT), BF16)
    for src0, dst_ref in ((_VS0, vst_ref), (_VW0, vwt_ref)):
        v = feat(src0, NSA_GROUPS * HEAD_DIM).astype(BF16)
        for t in range(TM // VT):
            for g in range(NSA_GROUPS):
                r0 = g * 2 * HEAD_DIM
                dst_ref[t, r0:r0 + HEAD_DIM, :] = v[g * HEAD_DIM:(g + 1) * HEAD_DIM, t * VT:(t + 1) * VT]
                dst_ref[t, r0 + HEAD_DIM:r0 + 2 * HEAD_DIM, :] = ones


def _inproj(x2, g_mix, w_tok, w_feat, seq):
    m, d = x2.shape
    row = lambda n: pl.BlockSpec((TM, n), lambda i: (i, 0))
    colb = lambda n: pl.BlockSpec((n, TM), lambda i: (0, i))
    vt_spec = pl.BlockSpec((TM // VT, NSA_GROUPS * LANES, VT), lambda i: (i, 0, 0))
    vt_shape = jax.ShapeDtypeStruct((m // VT, NSA_GROUPS * LANES, VT), BF16)
    tok_outs = [(_KAN, BF16), (_KWN, BF16), (_CVN, F32), (_ZCN // 2, F32), (_ZMN, F32)]
    return pl.pallas_call(
        functools.partial(_inproj_kernel, tiles_per_seq=seq // TM),
        grid=(m // TM,),
        in_specs=[row(d), _resident((1, d)), _resident(w_tok.shape), _resident(w_feat.shape)],
        out_specs=[colb(_QN)] + [row(n) for n, _ in tok_outs] + [vt_spec, vt_spec, colb(_ZGN)],
        out_shape=([jax.ShapeDtypeStruct((_QN, m), BF16)]
                   + [jax.ShapeDtypeStruct((m, n), dt) for n, dt in tok_outs]
                   + [vt_shape, vt_shape, jax.ShapeDtypeStruct((_ZGN, m), F32)]),
        compiler_params=pltpu.CompilerParams(dimension_semantics=("parallel",), vmem_limit_bytes=VMEM_LIMIT),
        name="inproj",
    )(x2, g_mix, w_tok, w_feat)


N_CMP_PAD = 256
_CMP_HALF = CMP_LEN // 2


def _compress_kernel(ck_ref, cv_ref, pos_ref, w1_ref, w2_ref, w2t_ref, out_ref, outt_ref, *, n_cmp):
    hid = w1_ref.shape[-1]
    first = [jnp.zeros((N_CMP_PAD, hid), F32) for _ in range(4)]
    second = [jnp.zeros((N_CMP_PAD, hid), F32) for _ in range(4)]
    for l in range(_CMP_HALF):
        for kind, src_ref in enumerate((ck_ref, cv_ref)):
            xl = src_ref[pl.ds(l, N_CMP_PAD, stride=CMP_STRIDE), :]
            pcol = slice(kind * LANES, (kind + 1) * LANES)
            xa = (xl + pos_ref[l:l + 1, pcol]).astype(BF16)
            xb = (xl + pos_ref[_CMP_HALF + l:_CMP_HALF + l + 1, pcol]).astype(BF16)
            wa = w1_ref[kind, l * HEAD_DIM:(l + 1) * HEAD_DIM, :]
            wb = w1_ref[kind, (_CMP_HALF + l) * HEAD_DIM:(_CMP_HALF + l + 1) * HEAD_DIM, :]
            for g in range(NSA_GROUPS):
                kg = kind * NSA_GROUPS + g
                cols = slice(g * HEAD_DIM, (g + 1) * HEAD_DIM)
                first[kg] = first[kg] + _dot(xa[:, cols], wa)
                second[kg] = second[kg] + _dot(xb[:, cols], wb)
    rows = lax.broadcasted_iota(jnp.int32, (N_CMP_PAD, LANES), 0)
    lanes = lax.broadcasted_iota(jnp.int32, (LANES, N_CMP_PAD), 1)
    for g in range(NSA_GROUPS):
        acts = []
        for kind in range(2):
            kg = kind * NSA_GROUPS + g
            h1 = first[kg] + pltpu.roll(second[kg], N_CMP_PAD - 1, 0)
            acts.append(_gelu_tanh(h1).astype(BF16))
        kcv = _dot(acts[0], w2_ref[0]) + _dot(acts[1], w2_ref[1])
        kcvt = _dot_nt(w2t_ref[0], acts[0]) + _dot_nt(w2t_ref[1], acts[1])
        out_ref[0, g] = jnp.where(rows < n_cmp, kcv, 0.0).astype(BF16)
        outt_ref[0, g] = jnp.where(lanes < n_cmp, kcvt, 0.0).astype(BF16)


def _compress(cv, pos, w1, w2, w2t, batch, seq):
    n_cmp = (seq - CMP_LEN) // CMP_STRIDE + 1
    assert n_cmp < N_CMP_PAD and seq == N_CMP_PAD * CMP_STRIDE
    return pl.pallas_call(
        functools.partial(_compress_kernel, n_cmp=n_cmp),
        grid=(batch,),
        in_specs=[pl.BlockSpec((seq, LANES), lambda b: (b, 0)), pl.BlockSpec((seq, LANES), lambda b: (b, 1)),
                  _resident(pos.shape), _resident(w1.shape), _resident(w2.shape), _resident(w2t.shape)],
        out_specs=[pl.BlockSpec((1, NSA_GROUPS, N_CMP_PAD, LANES), lambda b: (b, 0, 0, 0)),
                   pl.BlockSpec((1, NSA_GROUPS, LANES, N_CMP_PAD), lambda b: (b, 0, 0, 0))],
        out_shape=[jax.ShapeDtypeStruct((batch, NSA_GROUPS, N_CMP_PAD, LANES), BF16),
                   jax.ShapeDtypeStruct((batch, NSA_GROUPS, LANES, N_CMP_PAD), BF16)],
        compiler_params=pltpu.CompilerParams(dimension_semantics=("parallel",), vmem_limit_bytes=VMEM_LIMIT),
        name="compress",
    )(cv, cv, pos, w1, w2, w2t)


N_SEL = 64


def _attn_kernel(qt_ref, kcv_ref, kcvt_ref, sel_ref, ka_ref, kw_ref, vst_ref, vwt_ref, gtt_ref, o_ref,
                 q0_ref, q1_ref, qs_ref, m_ref, acc_ref, s_ref, smax_ref, p_ref, alpha_ref, oc_ref, ow_ref):
    g = pl.program_id(1)
    qb = pl.program_id(2)
    t0 = qb * TQ
    cols = NSA_HPG * TQ

    zeros = jnp.zeros((HEAD_DIM, TQ), BF16)
    for h in range(NSA_HPG):
        qh = qt_ref[h * HEAD_DIM:(h + 1) * HEAD_DIM, :]
        cs = slice(h * TQ, (h + 1) * TQ)
        q0_ref[0:HEAD_DIM, cs] = qh
        q0_ref[HEAD_DIM:, cs] = zeros
        q1_ref[0:HEAD_DIM, cs] = zeros
        q1_ref[HEAD_DIM:, cs] = qh
        qs_ref[0:HEAD_DIM, cs] = qh
        qs_ref[HEAD_DIM:, cs] = zeros
    q0 = q0_ref[...]
    tq = t0 + (lax.broadcasted_iota(jnp.int32, (1, cols), 1) & (TQ - 1))

    sc = _dot(kcv_ref[0, 0], q0)
    wt0 = jnp.maximum(t0 - WINDOW, 0) // VT
    kstart = pl.multiple_of(wt0 * VT, VT)
    span = WIN_TILES * VT
    qwin = jnp.where(g == 0, q0, q1_ref[...])
    sw = _dot(kw_ref[pl.ds(kstart, span), :], qwin)

    tq1 = t0 + lax.broadcasted_iota(jnp.int32, (1, TQ), 1)

    def per_head(x):
        return jnp.concatenate([x] * NSA_HPG, axis=1)

    cend = lax.broadcasted_iota(jnp.int32, (N_CMP_PAD, 1), 0) * CMP_STRIDE + (CMP_LEN - 1)
    sc = sc + per_head(jnp.where(cend <= tq1, 0.0, NEG))
    ec = jnp.exp2(sc - jnp.max(sc, axis=0, keepdims=True))
    lc = jnp.sum(ec, axis=0, keepdims=True)
    visible = tq >= CMP_LEN - 1
    dist = tq1 - (kstart + lax.broadcasted_iota(jnp.int32, (span, 1), 0))
    sw = sw + per_head(jnp.where(dist.astype(jnp.uint32) < WINDOW, 0.0, NEG))
    wmax = jnp.max(sw, axis=0, keepdims=True)
    pc = ec * (jnp.where(visible, 1.0 / lc, 0.0) + _zero_after(wmax))
    oc_ref[...] = _dot(kcvt_ref[0, 0], pc.astype(BF16))[0:HEAD_DIM]

    psum = pc[:, 0:TQ]
    for h in range(1, NSA_HPG):
        psum = psum + pc[:, h * TQ:(h + 1) * TQ]
    sel = sel_ref[...]
    p_hi = psum.astype(BF16)
    rem = psum - p_hi.astype(F32)
    p_mid = rem.astype(BF16)
    p_lo = (rem - p_mid.astype(F32)).astype(BF16)
    imp = (_dot(sel, p_hi) + _dot(sel, p_mid)) + _dot(sel, p_lo)

    pw = jnp.exp2(sw - wmax).astype(BF16)
    ow = _dot(jnp.concatenate([vwt_ref[wt0 + c] for c in range(WIN_TILES)], axis=1), pw)
    ow_ref[...] = ow[0:HEAD_DIM] * (1.0 / ow[HEAD_DIM:HEAD_DIM + 1])

    j = lax.broadcasted_iota(jnp.int32, (N_SEL, TQ), 0)
    cur = (t0 + lax.broadcasted_iota(jnp.int32, (N_SEL, TQ), 1)) >> SEL_SHIFT

    forced = (j == 0) | (j == cur) | (j == cur - 1)
    score = jnp.where(j <= cur, jnp.where(forced, FORCE_SCORE, imp), NEG)
    jloc = lax.broadcasted_iota(jnp.int32, (SUBLANES, TQ), 0)

    def set_bias(row_blocks, score):
        ranks = []
        for r in row_blocks:
            blk = score[r * SUBLANES:(r + 1) * SUBLANES]
            rank = jnp.zeros((SUBLANES, TQ), jnp.int32)
            for i in range(N_SEL):
                vi = jnp.broadcast_to(score[i:i + 1, :], (SUBLANES, TQ))
                if r * SUBLANES > i:
                    before = vi >= blk
                elif (r + 1) * SUBLANES - 1 <= i:
                    before = vi > blk
                else:
                    before = (vi > blk) | ((vi == blk) & (jloc > i - r * SUBLANES))
                rank = rank + before.astype(jnp.int32)
            ranks.append(rank)
        bias = jnp.where(jnp.concatenate(ranks, axis=0) < SEL_TOP, 0.0, MASK_BIAS).astype(BF16)
        lo = HEAD_DIM + row_blocks[0] * SUBLANES
        for h in range(NSA_HPG):
            qs_ref[lo:lo + bias.shape[0], h * TQ:(h + 1) * TQ] = bias

    m_ref[...] = jnp.full(m_ref.shape, NEG, F32)
    acc_ref[...] = jnp.zeros(acc_ref.shape, F32)
    ksub = lax.broadcasted_iota(jnp.int32, (TK, 1), 0)
    even, odd = 0, 1

    def logits(kt, slot):
        k0 = pl.multiple_of(kt * TK, TK)
        s = _dot(ka_ref[pl.ds(k0, TK), :], qs_ref[...])
        s_ref[slot] = s
        smax_ref[slot] = jnp.max(s, axis=0, keepdims=True)
        return s[0:SUBLANES, 0:TQ]

    def softmax(kt, slot, causal, after=None):
        s = s_ref[slot]
        if causal:
            s = jnp.where(kt * TK + ksub <= tq, s, NEG)
            smax = jnp.max(s, axis=0, keepdims=True)
        else:
            smax = smax_ref[slot]
        m_prev = m_ref[...]
        m_new = jnp.maximum(m_prev, smax)
        p_ref[slot] = jnp.exp2(s - m_new).astype(BF16)
        alpha = jnp.exp2(m_prev - m_new)
        if after is not None:
            alpha = alpha + _zero_after(per_head(after[0:1, :]))
        alpha_ref[slot] = alpha
        m_ref[...] = m_new

    def accumulate(kt, slot):
        p = p_ref[slot]
        v0 = jnp.maximum(kt, 0) * (TK // VT)
        v = jnp.concatenate([vst_ref[v0 + c] for c in range(TK // VT)], axis=1)
        acc_ref[...] = alpha_ref[slot] * acc_ref[...] + _dot(v, p)

    p_ref[odd] = jnp.zeros(p_ref.shape[1:], BF16)
    alpha_ref[odd] = jnp.ones(alpha_ref.shape[1:], F32)
    last = qb // (TK // TQ)
    pairs = last // 2
    n_row_blocks = N_SEL // SUBLANES
    first_blocks = max(TK // SEL_LEN, BF16_ROWS) // SUBLANES
    set_bias(list(range(first_blocks)), score)
    started = logits(0, even)
    set_bias(list(range(first_blocks, n_row_blocks)), score + _zero_after(jnp.concatenate([started] * n_row_blocks, axis=0)))

    def body(jp, carry):
        a = 2 * jp
        accumulate(a - 1, odd)
        softmax(a, even, False, after=logits(a + 1, odd))
        accumulate(a, even)
        softmax(a + 1, odd, False, after=logits(a + 2, even))
        return carry

    lax.fori_loop(0, pairs, body, 0)
    a = 2 * pairs

    @pl.when(last == a)
    def _():
        accumulate(a - 1, odd)
        softmax(a, even, True)
        accumulate(a, even)

    @pl.when(last != a)
    def _():
        accumulate(a - 1, odd)
        softmax(a, even, False, after=logits(a + 1, odd))
        accumulate(a, even)
        softmax(a + 1, odd, True)
        accumulate(a + 1, odd)

    acc = acc_ref[...]
    o_s = acc[0:HEAD_DIM] * (1.0 / acc[HEAD_DIM:HEAD_DIM + 1])

    o_c = oc_ref[...]
    o_w = ow_ref[...]
    heads = []
    for h in range(NSA_HPG):
        cs = slice(h * TQ, (h + 1) * TQ)

        def gate(br):
            return gtt_ref[pl.ds(br * NSA_HEADS + g * NSA_HPG + h, 1), :]

        heads.append(gate(0) * o_c[:, cs] + gate(1) * o_s[:, cs] + gate(2) * o_w[:, cs])
    o_ref[...] = jnp.concatenate(heads, axis=0).T.astype(BF16)


def _attention(qt, kcv, kcvt, sel, ka, kw, vst, vwt, gtt, batch, seq):
    nq = seq // TQ
    cols = NSA_HPG * TQ
    assert seq // SEL_LEN == N_SEL and WIN_TILES * VT <= seq and TK % VT == 0 and TK % TQ == 0
    assert WINDOW % VT == 0 and (TQ % VT == 0 or VT % TQ == 0)
    vt_spec = pl.BlockSpec((seq // VT, LANES, VT), lambda b, g, i: (b, g, 0))
    return pl.pallas_call(
        _attn_kernel,
        grid=(batch, NSA_GROUPS, nq),
        in_specs=[
            pl.BlockSpec((NSA_HPG * HEAD_DIM, TQ), lambda b, g, i: (g, b * nq + i)),
            pl.BlockSpec((1, 1, N_CMP_PAD, LANES), lambda b, g, i: (b, g, 0, 0)),
            pl.BlockSpec((1, 1, LANES, N_CMP_PAD), lambda b, g, i: (b, g, 0, 0)),
            pl.BlockSpec(sel.shape, lambda b, g, i: (0, 0)),
            pl.BlockSpec((seq, LANES), lambda b, g, i: (b, g)),
            pl.BlockSpec((seq, LANES), lambda b, g, i: (b, 0)),
            vt_spec, vt_spec,
            pl.BlockSpec((_ZGN, TQ), lambda b, g, i: (0, b * nq + i)),
        ],
        out_specs=pl.BlockSpec((TQ, NSA_HPG * HEAD_DIM), lambda b, g, i: (b * nq + i, g)),
        out_shape=jax.ShapeDtypeStruct((batch * seq, NSA_HEADS * HEAD_DIM), BF16),
        scratch_shapes=[
            pltpu.VMEM((LANES, cols), BF16),
            pltpu.VMEM((LANES, cols), BF16),
            pltpu.VMEM((LANES, cols), BF16),
            pltpu.VMEM((1, cols), F32),
            pltpu.VMEM((LANES, cols), F32),
            pltpu.VMEM((2, TK, cols), F32),
            pltpu.VMEM((2, 1, cols), F32),
            pltpu.VMEM((2, TK, cols), BF16),
            pltpu.VMEM((2, 1, cols), F32),
            pltpu.VMEM((HEAD_DIM, cols), F32),
            pltpu.VMEM((HEAD_DIM, cols), F32),
        ],
        compiler_params=pltpu.CompilerParams(dimension_semantics=("parallel", "parallel", "arbitrary"),
                                             vmem_limit_bytes=VMEM_LIMIT),
        name="nsa_attention",
    )(qt, kcv, kcvt, sel, ka, kw, vst, vwt, gtt)


def _merge_kernel(o_ref, c_ref, halo_ref, sg_ref, x_ref, wo_ref, cw_ref, cb_ref, lg_ref, lb_ref, wco_ref,
                  bco_ref, wout_ref, h_ref, cext_ref, *, tiles_per_seq):
    first = (pl.program_id(0) % tiles_per_seq) == 0
    cext_ref[0:CONV_HALO, :] = jnp.where(first, 0.0, halo_ref[...])
    cext_ref[CONV_HALO:, :] = c_ref[...]
    shift = CONV_HALO - (CONV_WIDTH - 1)
    acc = jnp.broadcast_to(cb_ref[...], c_ref.shape)
    for res in range(SUBLANES):
        taps = [k for k in range(CONV_WIDTH) if (shift + k) % SUBLANES == res]
        n_rows = TM + (SUBLANES if res else 0)
        part = None
        for k in taps:
            base = shift + k - res
            term = cw_ref[k:k + 1, :] * cext_ref[base:base + n_rows, :]
            part = term if part is None else part + term
        acc = acc + part[res:res + TM]
    xc = acc - jnp.mean(acc, axis=-1, keepdims=True)
    y = xc * lax.rsqrt(jnp.mean(xc * xc, axis=-1, keepdims=True) + EPS) * lg_ref[...] + lb_ref[...]
    y_b = _dot((y * _sigmoid(y)).astype(BF16), wco_ref[...]) + bco_ref[...]
    y_a = _dot(o_ref[...], wo_ref[...])
    d = y_a.shape[-1]
    mix = sg_ref[:, :d] * y_a + sg_ref[:, d:] * y_b
    h_ref[...] = x_ref[...] + _dot(mix.astype(BF16), wout_ref[...])


def _merge(o, c, sg, x2, wo, cw, cb, lg, lb, wco, bco, wout, seq):
    m, d = x2.shape
    ch = c.shape[-1]
    row = lambda n: pl.BlockSpec((TM, n), lambda i: (i, 0))
    halo = pl.BlockSpec((CONV_HALO, ch), lambda i: (jnp.maximum(i * (TM // CONV_HALO) - 1, 0), 0))
    weights = [wo, cw, cb, lg, lb, wco, bco, wout]
    return pl.pallas_call(
        functools.partial(_merge_kernel, tiles_per_seq=seq // TM),
        grid=(m // TM,),
        in_specs=[row(o.shape[-1]), row(ch), halo, row(sg.shape[-1]), row(d)] + [_resident(w.shape) for w in weights],
        out_specs=row(d),
        out_shape=jax.ShapeDtypeStruct((m, d), F32),
        scratch_shapes=[pltpu.VMEM((CONV_HALO + TM, ch), F32)],
        compiler_params=pltpu.CompilerParams(dimension_semantics=("parallel",), vmem_limit_bytes=VMEM_LIMIT),
        name="merge",
    )(o, c, c, sg, x2, *weights)


def _ffn_kernel(h_ref, halo_ref, p_ref, gf_ref, wup_ref, fw_ref, fb_ref, wdn_ref, gp_ref, wpg_ref, wple_ref,
                gfin_ref, out_ref, u_ref, v_ref, *, tiles_per_seq, final_norm):
    first = (pl.program_id(0) % tiles_per_seq) == 0
    h = h_ref[...]
    u_ref[0:FFN_HALO, :] = _rms(jnp.where(first, 0.0, halo_ref[...]), gf_ref[...]).astype(BF16)
    u_ref[FFN_HALO:, :] = _rms(h, gf_ref[...]).astype(BF16)
    v_ref[...] = _dot(u_ref[...], wup_ref[...])
    shift = FFN_HALO - (FFN_CONV_WIDTH - 1)
    rows = h.shape[0]
    v = fb_ref[...]
    for k in range(FFN_CONV_WIDTH):
        v = v + fw_ref[k:k + 1, :] * v_ref[shift + k:shift + k + rows, :]
    d_ff = wdn_ref.shape[0]
    a = _gelu_tanh(v[:, :d_ff]) * v[:, d_ff:]
    h = h + _dot(a.astype(BF16), wdn_ref[...])
    gate = _sigmoid(_dot(_rms(h, gp_ref[...]).astype(BF16), wpg_ref[...]))
    h = h + gate * _dot(p_ref[...].astype(BF16), wple_ref[...])
    out_ref[...] = _rms(h, gfin_ref[...]) if final_norm else h


def _ffn(h1, p3, layer, gf, wup, fw, fb, wdn, gp, wpg, wple, gfin, seq, final_norm):
    m, d = h1.shape
    assert wdn.shape[0] % LANES == 0 and seq % TM_FFN == 0
    tm = TM_FFN
    row = lambda n: pl.BlockSpec((tm, n), lambda i: (i, 0))
    halo = pl.BlockSpec((FFN_HALO, d), lambda i: (jnp.maximum(i * (tm // FFN_HALO) - 1, 0), 0))
    weights = [gf, wup, fw, fb, wdn, gp, wpg, wple, gfin]
    return pl.pallas_call(
        functools.partial(_ffn_kernel, tiles_per_seq=seq // tm, final_norm=final_norm),
        grid=(m // tm,),
        in_specs=[row(d), halo, pl.BlockSpec((None, tm, p3.shape[-1]), lambda i: (layer, i, 0))]
        + [_resident(w.shape) for w in weights],
        out_specs=row(d),
        out_shape=jax.ShapeDtypeStruct((m, d), F32),
        scratch_shapes=[pltpu.VMEM((FFN_HALO + tm, d), BF16), pltpu.VMEM((FFN_HALO + tm, wup.shape[1]), F32)],
        compiler_params=pltpu.CompilerParams(dimension_semantics=("parallel",), vmem_limit_bytes=VMEM_LIMIT),
        name="ffn_ple",
    )(h1, h1, p3, *weights)


def _sel_map_t(seq):
    n_cmp = (seq - CMP_LEN) // CMP_STRIDE + 1
    n_sel = seq // SEL_LEN
    c0 = np.arange(n_cmp) * CMP_STRIDE
    j0 = np.arange(n_sel) * SEL_LEN
    lo = np.maximum(c0[None, :], j0[:, None])
    hi = np.minimum(c0[None, :] + CMP_LEN, j0[:, None] + SEL_LEN)
    out = np.zeros((N_SEL, N_CMP_PAD), np.float32)
    out[:n_sel, :n_cmp] = np.maximum(hi - lo, 0) / CMP_LEN
    return jnp.asarray(out, BF16)


def _inproj_weights(w):
    d = w.shape[0]
    nq = NSA_HEADS * HEAD_DIM
    kv0 = nq
    gate0 = kv0 + 3 * 2 * NSA_GROUPS * HEAD_DIM
    conv0 = gate0 + 3 * NSA_HEADS
    merge0 = conv0 + _ZCN

    def kv(br, g):
        lo = kv0 + (br * NSA_GROUPS + g) * HEAD_DIM
        return w[:, lo:lo + HEAD_DIM]

    z = jnp.zeros((d, HEAD_DIM), w.dtype)
    tok = [kv(2, 0), z, kv(2, 1), z, kv(4, 0), kv(4, 1), w[:, kv0:kv0 + _CVN], w[:, conv0:merge0],
           w[:, merge0:merge0 + _ZMN]]
    feat = [w[:, :nq] * Q_SCALE, kv(3, 0), kv(3, 1), kv(5, 0), kv(5, 1), w[:, gate0:conv0],
            jnp.zeros((d, _ZGN - 3 * NSA_HEADS), w.dtype)]
    w_tok = jnp.concatenate(tok, axis=1).astype(BF16)
    w_feat = jnp.concatenate(feat, axis=1).T.astype(BF16)
    assert w_tok.shape[1] == _N_TOK and w_feat.shape[0] == _N_FEAT
    return w_tok, w_feat


def kernel(x, p, g_mix, w_in, cmp_pos_k, cmp_pos_v, w_cmp_k1, w_cmp_k2, w_cmp_v1, w_cmp_v2, w_o_nsa, conv_w,
           conv_b, conv_ln_g, conv_ln_b, w_conv_out, b_conv_out, w_out, g_ffn, w_up, ffn_conv_w, ffn_conv_b,
           w_down, g_ple, w_ple_gate, w_ple, g_final):
    batch, seq, d = x.shape
    depth = w_in.shape[0]
    m = batch * seq
    assert seq % TM == 0 and seq % TK == 0
    sel = _sel_map_t(seq)
    row = lambda v: v.reshape(1, -1)
    h = x.reshape(m, d)
    for i in range(depth):
        w_tok, w_feat = _inproj_weights(w_in[i])
        pos = jnp.concatenate([cmp_pos_k[i]] * NSA_GROUPS + [cmp_pos_v[i]] * NSA_GROUPS, axis=1)
        w1 = jnp.stack([w_cmp_k1[i], w_cmp_v1[i]]).astype(BF16)
        zpad = jnp.zeros_like(w_cmp_k2[i])
        w2 = jnp.stack([jnp.concatenate([w_cmp_k2[i], zpad], axis=1),
                        jnp.concatenate([zpad, w_cmp_v2[i]], axis=1)]).astype(BF16)
        w2t = jnp.stack([jnp.concatenate([zpad, w_cmp_k2[i]], axis=1).T,
                         jnp.concatenate([w_cmp_v2[i], zpad], axis=1).T]).astype(BF16)

        qt, ka, kw, cv, c, sg, vst, vwt, gtt = _inproj(h, row(g_mix[i]), w_tok, w_feat, seq)
        kcv, kcvt = _compress(cv, pos, w1, w2, w2t, batch, seq)
        o = _attention(qt, kcv, kcvt, sel, ka, kw, vst, vwt, gtt, batch, seq)
        h = _merge(o, c, sg, h, w_o_nsa[i].astype(BF16), conv_w[i], row(conv_b[i]), row(conv_ln_g[i]),
                   row(conv_ln_b[i]), w_conv_out[i].astype(BF16), row(b_conv_out[i]), w_out[i].astype(BF16), seq)
        h = _ffn(h, p.reshape(depth, m, -1), i, row(g_ffn[i]), w_up[i].astype(BF16), ffn_conv_w[i], row(ffn_conv_b[i]),
                 w_down[i].astype(BF16), row(g_ple[i]), w_ple_gate[i].astype(BF16), w_ple[i].astype(BF16),
                 row(g_final), seq, final_norm=(i == depth - 1))
    return h.reshape(batch, seq, d)
```

```python
import functools

import numpy as np
import jax
import jax.numpy as jnp
from jax import lax
from jax.experimental import pallas as pl
from jax.experimental.pallas import tpu as pltpu

F32 = jnp.float32
BF16 = jnp.bfloat16

NSA_HEADS = 8
NSA_GROUPS = 2
NSA_HPG = NSA_HEADS // NSA_GROUPS
HEAD_DIM = 64
CMP_LEN = 32
CMP_STRIDE = 16
SEL_LEN = 64
SEL_SHIFT = 6
SEL_TOP = 16
WINDOW = 512
FORCE_SCORE = 1e4
CONV_WIDTH = 31
FFN_CONV_WIDTH = 3
EPS = 1e-6
NEG = -1e30
MASK_BIAS = NEG
Q_SCALE = HEAD_DIM ** -0.5 * float(np.log2(np.e))

LANES = 128
SUBLANES = 8
BF16_ROWS = 16
VMEM_LIMIT = 56 * 1024 * 1024

TM = 1024
TM_ROWS = 512
TM_FFN = 512
TQ = 256
TK = 256
VT = 256
WIN_TILES = (WINDOW + TQ) // VT + (1 if TQ % VT else 0)
CONV_HALO = 32
FFN_HALO = BF16_ROWS


def _sigmoid(x):
    return 0.5 * (jnp.tanh(0.5 * x) + 1.0)


def _gelu_tanh(x):
    return 0.5 * x * (1.0 + jnp.tanh(np.sqrt(2.0 / np.pi).astype(np.float32) * (x + 0.044715 * (x * x * x))))


def _rms(x, g):
    return x * lax.rsqrt(jnp.mean(x * x, axis=-1, keepdims=True) + EPS) * g


def _zero_after(x):
    bits = lax.bitcast_convert_type(x, jnp.uint32)
    bits = lax.shift_right_logical(lax.shift_right_logical(bits, jnp.uint32(16)), jnp.uint32(16))
    return lax.bitcast_convert_type(bits, F32)


def _dot(a, b):
    return jnp.dot(a, b, preferred_element_type=F32)


def _dot_nt(a, b):
    return lax.dot_general(a, b, (((1,), (1,)), ((), ())), preferred_element_type=F32)


def _resident(shape):
    nd = len(shape)
    return pl.BlockSpec(shape, lambda *_: (0,) * nd, pipeline_mode=pl.Buffered(1))


_KA0, _KAN = 0, 256
_KW0, _KWN = 256, 128
_CV0, _CVN = 384, 256
_ZC0, _ZCN = 640, 1024
_ZM0, _ZMN = 1664, 2048
_N_TOK = 3712
_Q0, _QN = 0, 512
_VS0, _VSN = 512, 128
_VW0, _VWN = 640, 128
_ZG0, _ZGN = 768, 32
_N_FEAT = 800


def _inproj_kernel(x_ref, g_ref, wt_ref, wf_ref, qt_ref, ka_ref, kw_ref, cv_ref, c_ref, sg_ref, vst_ref, vwt_ref,
                   gtt_ref, *, tiles_per_seq):
    u = _rms(x_ref[...], g_ref[...]).astype(BF16)

    z_tok = _dot(u, wt_ref[...])
    z_feat = _dot_nt(wf_ref[...], u)

    def tok(lo, n):
        return z_tok[:, lo:lo + n]

    def feat(lo, n):
        return z_feat[lo:lo + n, :]

    ka = tok(_KA0, _KAN)
    s0 = (pl.program_id(0) % tiles_per_seq) * TM
    row = lax.broadcasted_iota(jnp.int32, ka.shape, 0) + s0
    col = lax.broadcasted_iota(jnp.int32, ka.shape, 1)
    onehot = ((col & HEAD_DIM) != 0) & ((col & (HEAD_DIM - 1)) == (row >> SEL_SHIFT))
    ka_ref[...] = jnp.where(onehot, 1.0, ka).astype(BF16)
    kw_ref[...] = tok(_KW0, _KWN).astype(BF16)
    cv_ref[...] = tok(_CV0, _CVN)
    zc = tok(_ZC0, _ZCN)
    half = _ZCN // 2
    c_ref[...] = zc[:, :half] * _sigmoid(zc[:, half:])
    sg_ref[...] = _sigmoid(tok(_ZM0, _ZMN))

    qt_ref[...] = feat(_Q0, _QN).astype(BF16)
    gtt_ref[...] = _sigmoid(feat(_ZG0, _ZGN))
    ones = jnp.ones((HEAD_DIM, VT), BF16)
    for src0, dst_ref in ((_VS0, vst_ref), (_VW0, vwt_ref)):
        v = feat(src0, NSA_GROUPS * HEAD_DIM).astype(BF16)
        for t in range(TM // VT):
            for g in range(NSA_GROUPS):
                r0 = g * 2 * HEAD_DIM
                dst_ref[t, r0:r0 + HEAD_DIM, :] = v[g * HEAD_DIM:(g + 1) * HEAD_DIM, t * VT:(t + 1) * VT]
                dst_ref[t, r0 + HEAD_DIM:r0 + 2 * HEAD_DIM, :] = ones


def _inproj(x2, g_mix, w_tok, w_feat, seq):
    m, d = x2.shape
    row = lambda n: pl.BlockSpec((TM, n), lambda i: (i, 0))
    colb = lambda n: pl.BlockSpec((n, TM), lambda i: (0, i))
    vt_spec = pl.BlockSpec((TM // VT, NSA_GROUPS * LANES, VT), lambda i: (i, 0, 0))
    vt_shape = jax.ShapeDtypeStruct((m // VT, NSA_GROUPS * LANES, VT), BF16)
    tok_outs = [(_KAN, BF16), (_KWN, BF16), (_CVN, F32), (_ZCN // 2, F32), (_ZMN, F32)]
    return pl.pallas_call(
        functools.partial(_inproj_kernel, tiles_per_seq=seq // TM),
        grid=(m // TM,),
        in_specs=[row(d), _resident((1, d)), _resident(w_tok.shape), _resident(w_feat.shape)],
        out_specs=[colb(_QN)] + [row(n) for n, _ in tok_outs] + [vt_spec, vt_spec, colb(_ZGN)],
        out_shape=([jax.ShapeDtypeStruct((_QN, m), BF16)]
                   + [jax.ShapeDtypeStruct((m, n), dt) for n, dt in tok_outs]
                   + [vt_shape, vt_shape, jax.ShapeDtypeStruct((_ZGN, m), F32)]),
        compiler_params=pltpu.CompilerParams(dimension_semantics=("parallel",), vmem_limit_bytes=VMEM_LIMIT),
        name="inproj",
    )(x2, g_mix, w_tok, w_feat)


N_CMP_PAD = 256
_CMP_HALF = CMP_LEN // 2


def _compress_kernel(ck_ref, cv_ref, pos_ref, w1_ref, w2_ref, w2t_ref, out_ref, outt_ref, *, n_cmp):
    hid = w1_ref.shape[-1]
    first = [jnp.zeros((N_CMP_PAD, hid), F32) for _ in range(4)]
    second = [jnp.zeros((N_CMP_PAD, hid), F32) for _ in range(4)]
    for l in range(_CMP_HALF):
        for kind, src_ref in enumerate((ck_ref, cv_ref)):
            xl = src_ref[pl.ds(l, N_CMP_PAD, stride=CMP_STRIDE), :]
            pcol = slice(kind * LANES, (kind + 1) * LANES)
            xa = (xl + pos_ref[l:l + 1, pcol]).astype(BF16)
            xb = (xl + pos_ref[_CMP_HALF + l:_CMP_HALF + l + 1, pcol]).astype(BF16)
            wa = w1_ref[kind, l * HEAD_DIM:(l + 1) * HEAD_DIM, :]
            wb = w1_ref[kind, (_CMP_HALF + l) * HEAD_DIM:(_CMP_HALF + l + 1) * HEAD_DIM, :]
            for g in range(NSA_GROUPS):
                kg = kind * NSA_GROUPS + g
                cols = slice(g * HEAD_DIM, (g + 1) * HEAD_DIM)
                first[kg] = first[kg] + _dot(xa[:, cols], wa)
                second[kg] = second[kg] + _dot(xb[:, cols], wb)
    rows = lax.broadcasted_iota(jnp.int32, (N_CMP_PAD, LANES), 0)
    lanes = lax.broadcasted_iota(jnp.int32, (LANES, N_CMP_PAD), 1)
    for g in range(NSA_GROUPS):
        acts = []
        for kind in range(2):
            kg = kind * NSA_GROUPS + g
            h1 = first[kg] + pltpu.roll(second[kg], N_CMP_PAD - 1, 0)
            acts.append(_gelu_tanh(h1).astype(BF16))
        kcv = _dot(acts[0], w2_ref[0]) + _dot(acts[1], w2_ref[1])
        kcvt = _dot_nt(w2t_ref[0], acts[0]) + _dot_nt(w2t_ref[1], acts[1])
        out_ref[0, g] = jnp.where(rows < n_cmp, kcv, 0.0).astype(BF16)
        outt_ref[0, g] = jnp.where(lanes < n_cmp, kcvt, 0.0).astype(BF16)


def _compress(cv, pos, w1, w2, w2t, batch, seq):
    n_cmp = (seq - CMP_LEN) // CMP_STRIDE + 1
    assert n_cmp < N_CMP_PAD and seq == N_CMP_PAD * CMP_STRIDE
    return pl.pallas_call(
        functools.partial(_compress_kernel, n_cmp=n_cmp),
        grid=(batch,),
        in_specs=[pl.BlockSpec((seq, LANES), lambda b: (b, 0)), pl.BlockSpec((seq, LANES), lambda b: (b, 1)),
                  _resident(pos.shape), _resident(w1.shape), _resident(w2.shape), _resident(w2t.shape)],
        out_specs=[pl.BlockSpec((1, NSA_GROUPS, N_CMP_PAD, LANES), lambda b: (b, 0, 0, 0)),
                   pl.BlockSpec((1, NSA_GROUPS, LANES, N_CMP_PAD), lambda b: (b, 0, 0, 0))],
        out_shape=[jax.ShapeDtypeStruct((batch, NSA_GROUPS, N_CMP_PAD, LANES), BF16),
                   jax.ShapeDtypeStruct((batch, NSA_GROUPS, LANES, N_CMP_PAD), BF16)],
        compiler_params=pltpu.CompilerParams(dimension_semantics=("parallel",), vmem_limit_bytes=VMEM_LIMIT),
        name="compress",
    )(cv, cv, pos, w1, w2, w2t)


N_SEL = 64


def _attn_kernel(qt_ref, kcv_ref, kcvt_ref, sel_ref, ka_ref, kw_ref, vst_ref, vwt_ref, gtt_ref, o_ref,
                 q0_ref, q1_ref, qs_ref, m_ref, acc_ref, s_ref, smax_ref, p_ref, alpha_ref, oc_ref, ow_ref):
    g = pl.program_id(1)
    qb = pl.program_id(2)
    t0 = qb * TQ
    cols = NSA_HPG * TQ

    zeros = jnp.zeros((HEAD_DIM, TQ), BF16)
    for h in range(NSA_HPG):
        qh = qt_ref[h * HEAD_DIM:(h + 1) * HEAD_DIM, :]
        cs = slice(h * TQ, (h + 1) * TQ)
        q0_ref[0:HEAD_DIM, cs] = qh
        q0_ref[HEAD_DIM:, cs] = zeros
        q1_ref[0:HEAD_DIM, cs] = zeros
        q1_ref[HEAD_DIM:, cs] = qh
        qs_ref[0:HEAD_DIM, cs] = qh
        qs_ref[HEAD_DIM:, cs] = zeros
    q0 = q0_ref[...]
    tq = t0 + (lax.broadcasted_iota(jnp.int32, (1, cols), 1) & (TQ - 1))

    sc = _dot(kcv_ref[0, 0], q0)
    wt0 = jnp.maximum(t0 - WINDOW, 0) // VT
    kstart = pl.multiple_of(wt0 * VT, VT)
    span = WIN_TILES * VT
    qwin = jnp.where(g == 0, q0, q1_ref[...])
    sw = _dot(kw_ref[pl.ds(kstart, span), :], qwin)

    tq1 = t0 + lax.broadcasted_iota(jnp.int32, (1, TQ), 1)

    def per_head(x):
        return jnp.concatenate([x] * NSA_HPG, axis=1)

    cend = lax.broadcasted_iota(jnp.int32, (N_CMP_PAD, 1), 0) * CMP_STRIDE + (CMP_LEN - 1)
    sc = sc + per_head(jnp.where(cend <= tq1, 0.0, NEG))
    ec = jnp.exp2(sc - jnp.max(sc, axis=0, keepdims=True))
    lc = jnp.sum(ec, axis=0, keepdims=True)
    visible = tq >= CMP_LEN - 1
    dist = tq1 - (kstart + lax.broadcasted_iota(jnp.int32, (span, 1), 0))
    sw = sw + per_head(jnp.where(dist.astype(jnp.uint32) < WINDOW, 0.0, NEG))
    wmax = jnp.max(sw, axis=0, keepdims=True)
    pc = ec * (jnp.where(visible, 1.0 / lc, 0.0) + _zero_after(wmax))
    oc_ref[...] = _dot(kcvt_ref[0, 0], pc.astype(BF16))[0:HEAD_DIM]

    psum = pc[:, 0:TQ]
    for h in range(1, NSA_HPG):
        psum = psum + pc[:, h * TQ:(h + 1) * TQ]
    sel = sel_ref[...]
    p_hi = psum.astype(BF16)
    rem = psum - p_hi.astype(F32)
    p_mid = rem.astype(BF16)
    p_lo = (rem - p_mid.astype(F32)).astype(BF16)
    imp = (_dot(sel, p_hi) + _dot(sel, p_mid)) + _dot(sel, p_lo)

    pw = jnp.exp2(sw - wmax).astype(BF16)
    ow = _dot(jnp.concatenate([vwt_ref[wt0 + c] for c in range(WIN_TILES)], axis=1), pw)
    ow_ref[...] = ow[0:HEAD_DIM] * (1.0 / ow[HEAD_DIM:HEAD_DIM + 1])

    j = lax.broadcasted_iota(jnp.int32, (N_SEL, TQ), 0)
    cur = (t0 + lax.broadcasted_iota(jnp.int32, (N_SEL, TQ), 1)) >> SEL_SHIFT

    forced = (j == 0) | (j == cur) | (j == cur - 1)
    score = jnp.where(j <= cur, jnp.where(forced, FORCE_SCORE, imp), NEG)
    jloc = lax.broadcasted_iota(jnp.int32, (SUBLANES, TQ), 0)

    def set_bias(row_blocks, score):
        ranks = []
        for r in row_blocks:
            blk = score[r * SUBLANES:(r + 1) * SUBLANES]
            rank = jnp.zeros((SUBLANES, TQ), jnp.int32)
            for i in range(N_SEL):
                vi = jnp.broadcast_to(score[i:i + 1, :], (SUBLANES, TQ))
                if r * SUBLANES > i:
                    before = vi >= blk
                elif (r + 1) * SUBLANES - 1 <= i:
                    before = vi > blk
                else:
                    before = (vi > blk) | ((vi == blk) & (jloc > i - r * SUBLANES))
                rank = rank + before.astype(jnp.int32)
            ranks.append(rank)
        bias = jnp.where(jnp.concatenate(ranks, axis=0) < SEL_TOP, 0.0, MASK_BIAS).astype(BF16)
        lo = HEAD_DIM + row_blocks[0] * SUBLANES
        for h in range(NSA_HPG):
            qs_ref[lo:lo + bias.shape[0], h * TQ:(h + 1) * TQ] = bias

    m_ref[...] = jnp.full(m_ref.shape, NEG, F32)
    acc_ref[...] = jnp.zeros(acc_ref.shape, F32)
    ksub = lax.broadcasted_iota(jnp.int32, (TK, 1), 0)
    even, odd = 0, 1

    def logits(kt, slot):
        k0 = pl.multiple_of(kt * TK, TK)
        s = _dot(ka_ref[pl.ds(k0, TK), :], qs_ref[...])
        s_ref[slot] = s
        smax_ref[slot] = jnp.max(s, axis=0, keepdims=True)
        return s[0:SUBLANES, 0:TQ]

    def softmax(kt, slot, causal, after=None):
        s = s_ref[slot]
        if causal:
            s = jnp.where(kt * TK + ksub <= tq, s, NEG)
            smax = jnp.max(s, axis=0, keepdims=True)
        else:
            smax = smax_ref[slot]
        m_prev = m_ref[...]
        m_new = jnp.maximum(m_prev, smax)
        p_ref[slot] = jnp.exp2(s - m_new).astype(BF16)
        alpha = jnp.exp2(m_prev - m_new)
        if after is not None:
            alpha = alpha + _zero_after(per_head(after[0:1, :]))
        alpha_ref[slot] = alpha
        m_ref[...] = m_new

    def accumulate(kt, slot):
        p = p_ref[slot]
        v0 = jnp.maximum(kt, 0) * (TK // VT)
        v = jnp.concatenate([vst_ref[v0 + c] for c in range(TK // VT)], axis=1)
        acc_ref[...] = alpha_ref[slot] * acc_ref[...] + _dot(v, p)

    p_ref[odd] = jnp.zeros(p_ref.shape[1:], BF16)
    alpha_ref[odd] = jnp.ones(alpha_ref.shape[1:], F32)
    last = qb // (TK // TQ)
    pairs = last // 2
    n_row_blocks = N_SEL // SUBLANES
    first_blocks = max(TK // SEL_LEN, BF16_ROWS) // SUBLANES
    set_bias(list(range(first_blocks)), score)
    started = logits(0, even)
    set_bias(list(range(first_blocks, n_row_blocks)), score + _zero_after(jnp.concatenate([started] * n_row_blocks, axis=0)))

    def body(jp, carry):
        a = 2 * jp
        accumulate(a - 1, odd)
        softmax(a, even, False, after=logits(a + 1, odd))
        accumulate(a, even)
        softmax(a + 1, odd, False, after=logits(a + 2, even))
        return carry

    lax.fori_loop(0, pairs, body, 0)
    a = 2 * pairs

    @pl.when(last == a)
    def _():
        accumulate(a - 1, odd)
        softmax(a, even, True)
        accumulate(a, even)

    @pl.when(last != a)
    def _():
        accumulate(a - 1, odd)
        softmax(a, even, False, after=logits(a + 1, odd))
        accumulate(a, even)
        softmax(a + 1, odd, True)
        accumulate(a + 1, odd)

    acc = acc_ref[...]
    o_s = acc[0:HEAD_DIM] * (1.0 / acc[HEAD_DIM:HEAD_DIM + 1])

    o_c = oc_ref[...]
    o_w = ow_ref[...]
    heads = []
    for h in range(NSA_HPG):
        cs = slice(h * TQ, (h + 1) * TQ)

        def gate(br):
            return gtt_ref[pl.ds(br * NSA_HEADS + g * NSA_HPG + h, 1), :]

        heads.append(gate(0) * o_c[:, cs] + gate(1) * o_s[:, cs] + gate(2) * o_w[:, cs])
    o_ref[...] = jnp.concatenate(heads, axis=0).T.astype(BF16)


def _attention(qt, kcv, kcvt, sel, ka, kw, vst, vwt, gtt, batch, seq):
    nq = seq // TQ
    cols = NSA_HPG * TQ
    assert seq // SEL_LEN == N_SEL and WIN_TILES * VT <= seq and TK % VT == 0 and TK % TQ == 0
    assert WINDOW % VT == 0 and (TQ % VT == 0 or VT % TQ == 0)
    vt_spec = pl.BlockSpec((seq // VT, LANES, VT), lambda b, g, i: (b, g, 0))
    return pl.pallas_call(
        _attn_kernel,
        grid=(batch, NSA_GROUPS, nq),
        in_specs=[
            pl.BlockSpec((NSA_HPG * HEAD_DIM, TQ), lambda b, g, i: (g, b * nq + i)),
            pl.BlockSpec((1, 1, N_CMP_PAD, LANES), lambda b, g, i: (b, g, 0, 0)),
            pl.BlockSpec((1, 1, LANES, N_CMP_PAD), lambda b, g, i: (b, g, 0, 0)),
            pl.BlockSpec(sel.shape, lambda b, g, i: (0, 0)),
            pl.BlockSpec((seq, LANES), lambda b, g, i: (b, g)),
            pl.BlockSpec((seq, LANES), lambda b, g, i: (b, 0)),
            vt_spec, vt_spec,
            pl.BlockSpec((_ZGN, TQ), lambda b, g, i: (0, b * nq + i)),
        ],
        out_specs=pl.BlockSpec((TQ, NSA_HPG * HEAD_DIM), lambda b, g, i: (b * nq + i, g)),
        out_shape=jax.ShapeDtypeStruct((batch * seq, NSA_HEADS * HEAD_DIM), BF16),
        scratch_shapes=[
            pltpu.VMEM((LANES, cols), BF16),
            pltpu.VMEM((LANES, cols), BF16),
            pltpu.VMEM((LANES, cols), BF16),
            pltpu.VMEM((1, cols), F32),
            pltpu.VMEM((LANES, cols), F32),
            pltpu.VMEM((2, TK, cols), F32),
            pltpu.VMEM((2, 1, cols), F32),
            pltpu.VMEM((2, TK, cols), BF16),
            pltpu.VMEM((2, 1, cols), F32),
            pltpu.VMEM((HEAD_DIM, cols), F32),
            pltpu.VMEM((HEAD_DIM, cols), F32),
        ],
        compiler_params=pltpu.CompilerParams(dimension_semantics=("parallel", "parallel", "arbitrary"),
                                             vmem_limit_bytes=VMEM_LIMIT),
        name="nsa_attention",
    )(qt, kcv, kcvt, sel, ka, kw, vst, vwt, gtt)


def _merge_kernel(o_ref, c_ref, halo_ref, sg_ref, x_ref, wo_ref, cw_ref, cb_ref, lg_ref, lb_ref, wco_ref,
                  bco_ref, wout_ref, h_ref, cext_ref, *, tiles_per_seq):
    first = (pl.program_id(0) % tiles_per_seq) == 0
    cext_ref[0:CONV_HALO, :] = jnp.where(first, 0.0, halo_ref[...])
    cext_ref[CONV_HALO:, :] = c_ref[...]
    shift = CONV_HALO - (CONV_WIDTH - 1)
    rows = c_ref.shape[0]
    acc = jnp.broadcast_to(cb_ref[...], c_ref.shape)
    for res in range(SUBLANES):
        taps = [k for k in range(CONV_WIDTH) if (shift + k) % SUBLANES == res]
        n_rows = rows + (SUBLANES if res else 0)
        part = None
        for k in taps:
            base = shift + k - res
            term = cw_ref[k:k + 1, :] * cext_ref[base:base + n_rows, :]
            part = term if part is None else part + term
        acc = acc + part[res:res + rows]
    xc = acc - jnp.mean(acc, axis=-1, keepdims=True)
    y = xc * lax.rsqrt(jnp.mean(xc * xc, axis=-1, keepdims=True) + EPS) * lg_ref[...] + lb_ref[...]
    y_b = _dot((y * _sigmoid(y)).astype(BF16), wco_ref[...]) + bco_ref[...]
    y_a = _dot(o_ref[...], wo_ref[...])
    d = y_a.shape[-1]
    mix = sg_ref[:, :d] * y_a + sg_ref[:, d:] * y_b
    h_ref[...] = x_ref[...] + _dot(mix.astype(BF16), wout_ref[...])


def _merge(o, c, sg, x2, wo, cw, cb, lg, lb, wco, bco, wout, seq):
    m, d = x2.shape
    ch = c.shape[-1]
    tm = TM_ROWS
    assert seq % tm == 0
    row = lambda n: pl.BlockSpec((tm, n), lambda i: (i, 0))
    halo = pl.BlockSpec((CONV_HALO, ch), lambda i: (jnp.maximum(i * (tm // CONV_HALO) - 1, 0), 0))
    weights = [wo, cw, cb, lg, lb, wco, bco, wout]
    return pl.pallas_call(
        functools.partial(_merge_kernel, tiles_per_seq=seq // tm),
        grid=(m // tm,),
        in_specs=[row(o.shape[-1]), row(ch), halo, row(sg.shape[-1]), row(d)] + [_resident(w.shape) for w in weights],
        out_specs=row(d),
        out_shape=jax.ShapeDtypeStruct((m, d), F32),
        scratch_shapes=[pltpu.VMEM((CONV_HALO + tm, ch), F32)],
        compiler_params=pltpu.CompilerParams(dimension_semantics=("parallel",), vmem_limit_bytes=VMEM_LIMIT),
        name="merge",
    )(o, c, c, sg, x2, *weights)


def _ffn_kernel(h_ref, halo_ref, p_ref, gf_ref, wup_ref, fw_ref, fb_ref, wdn_ref, gp_ref, wpg_ref, wple_ref,
                gfin_ref, out_ref, u_ref, v_ref, *, tiles_per_seq, final_norm):
    first = (pl.program_id(0) % tiles_per_seq) == 0
    h = h_ref[...]
    u_ref[0:FFN_HALO, :] = _rms(jnp.where(first, 0.0, halo_ref[...]), gf_ref[...]).astype(BF16)
    u_ref[FFN_HALO:, :] = _rms(h, gf_ref[...]).astype(BF16)
    v_ref[...] = _dot(u_ref[...], wup_ref[...])
    shift = FFN_HALO - (FFN_CONV_WIDTH - 1)
    rows = h.shape[0]
    v = fb_ref[...]
    for k in range(FFN_CONV_WIDTH):
        v = v + fw_ref[k:k + 1, :] * v_ref[shift + k:shift + k + rows, :]
    d_ff = wdn_ref.shape[0]
    a = _gelu_tanh(v[:, :d_ff]) * v[:, d_ff:]
    h = h + _dot(a.astype(BF16), wdn_ref[...])
    gate = _sigmoid(_dot(_rms(h, gp_ref[...]).astype(BF16), wpg_ref[...]))
    h = h + gate * _dot(p_ref[...].astype(BF16), wple_ref[...])
    out_ref[...] = _rms(h, gfin_ref[...]) if final_norm else h


def _ffn(h1, p3, layer, gf, wup, fw, fb, wdn, gp, wpg, wple, gfin, seq, final_norm):
    m, d = h1.shape
    assert wdn.shape[0] % LANES == 0 and seq % TM_FFN == 0
    tm = TM_FFN
    row = lambda n: pl.BlockSpec((tm, n), lambda i: (i, 0))
    halo = pl.BlockSpec((FFN_HALO, d), lambda i: (jnp.maximum(i * (tm // FFN_HALO) - 1, 0), 0))
    weights = [gf, wup, fw, fb, wdn, gp, wpg, wple, gfin]
    return pl.pallas_call(
        functools.partial(_ffn_kernel, tiles_per_seq=seq // tm, final_norm=final_norm),
        grid=(m // tm,),
        in_specs=[row(d), halo, pl.BlockSpec((None, tm, p3.shape[-1]), lambda i: (layer, i, 0))]
        + [_resident(w.shape) for w in weights],
        out_specs=row(d),
        out_shape=jax.ShapeDtypeStruct((m, d), F32),
        scratch_shapes=[pltpu.VMEM((FFN_HALO + tm, d), BF16), pltpu.VMEM((FFN_HALO + tm, wup.shape[1]), F32)],
        compiler_params=pltpu.CompilerParams(dimension_semantics=("parallel",), vmem_limit_bytes=VMEM_LIMIT),
        name="ffn_ple",
    )(h1, h1, p3, *weights)


def _sel_map_t(seq):
    n_cmp = (seq - CMP_LEN) // CMP_STRIDE + 1
    n_sel = seq // SEL_LEN
    c0 = np.arange(n_cmp) * CMP_STRIDE
    j0 = np.arange(n_sel) * SEL_LEN
    lo = np.maximum(c0[None, :], j0[:, None])
    hi = np.minimum(c0[None, :] + CMP_LEN, j0[:, None] + SEL_LEN)
    out = np.zeros((N_SEL, N_CMP_PAD), np.float32)
    out[:n_sel, :n_cmp] = np.maximum(hi - lo, 0) / CMP_LEN
    return jnp.asarray(out, BF16)


def _inproj_weights(w):
    d = w.shape[0]
    nq = NSA_HEADS * HEAD_DIM
    kv0 = nq
    gate0 = kv0 + 3 * 2 * NSA_GROUPS * HEAD_DIM
    conv0 = gate0 + 3 * NSA_HEADS
    merge0 = conv0 + _ZCN

    def kv(br, g):
        lo = kv0 + (br * NSA_GROUPS + g) * HEAD_DIM
        return w[:, lo:lo + HEAD_DIM]

    z = jnp.zeros((d, HEAD_DIM), w.dtype)
    tok = [kv(2, 0), z, kv(2, 1), z, kv(4, 0), kv(4, 1), w[:, kv0:kv0 + _CVN], w[:, conv0:merge0],
           w[:, merge0:merge0 + _ZMN]]
    feat = [w[:, :nq] * Q_SCALE, kv(3, 0), kv(3, 1), kv(5, 0), kv(5, 1), w[:, gate0:conv0],
            jnp.zeros((d, _ZGN - 3 * NSA_HEADS), w.dtype)]
    w_tok = jnp.concatenate(tok, axis=1).astype(BF16)
    w_feat = jnp.concatenate(feat, axis=1).T.astype(BF16)
    assert w_tok.shape[1] == _N_TOK and w_feat.shape[0] == _N_FEAT
    return w_tok, w_feat


def kernel(x, p, g_mix, w_in, cmp_pos_k, cmp_pos_v, w_cmp_k1, w_cmp_k2, w_cmp_v1, w_cmp_v2, w_o_nsa, conv_w,
           conv_b, conv_ln_g, conv_ln_b, w_conv_out, b_conv_out, w_out, g_ffn, w_up, ffn_conv_w, ffn_conv_b,
           w_down, g_ple, w_ple_gate, w_ple, g_final):
    batch, seq, d = x.shape
    depth = w_in.shape[0]
    m = batch * seq
    assert seq % TM == 0 and seq % TK == 0
    sel = _sel_map_t(seq)
    row = lambda v: v.reshape(1, -1)
    h = x.reshape(m, d)
    for i in range(depth):
        w_tok, w_feat = _inproj_weights(w_in[i])
        pos = jnp.concatenate([cmp_pos_k[i]] * NSA_GROUPS + [cmp_pos_v[i]] * NSA_GROUPS, axis=1)
        w1 = jnp.stack([w_cmp_k1[i], w_cmp_v1[i]]).astype(BF16)
        zpad = jnp.zeros_like(w_cmp_k2[i])
        w2 = jnp.stack([jnp.concatenate([w_cmp_k2[i], zpad], axis=1),
                        jnp.concatenate([zpad, w_cmp_v2[i]], axis=1)]).astype(BF16)
        w2t = jnp.stack([jnp.concatenate([zpad, w_cmp_k2[i]], axis=1).T,
                         jnp.concatenate([w_cmp_v2[i], zpad], axis=1).T]).astype(BF16)

        qt, ka, kw, cv, c, sg, vst, vwt, gtt = _inproj(h, row(g_mix[i]), w_tok, w_feat, seq)
        kcv, kcvt = _compress(cv, pos, w1, w2, w2t, batch, seq)
        o = _attention(qt, kcv, kcvt, sel, ka, kw, vst, vwt, gtt, batch, seq)
        h = _merge(o, c, sg, h, w_o_nsa[i].astype(BF16), conv_w[i], row(conv_b[i]), row(conv_ln_g[i]),
                   row(conv_ln_b[i]), w_conv_out[i].astype(BF16), row(b_conv_out[i]), w_out[i].astype(BF16), seq)
        h = _ffn(h, p.reshape(depth, m, -1), i, row(g_ffn[i]), w_up[i].astype(BF16), ffn_conv_w[i], row(ffn_conv_b[i]),
                 w_down[i].astype(BF16), row(g_ple[i]), w_ple_gate[i].astype(BF16), w_ple[i].astype(BF16),
                 row(g_final), seq, final_norm=(i == depth - 1))
    return h.reshape(batch, seq, d)
```

```python
import functools

import numpy as np
import jax
import jax.numpy as jnp
from jax import lax
from jax.experimental import pallas as pl
from jax.experimental.pallas import tpu as pltpu

F32 = jnp.float32
BF16 = jnp.bfloat16

NSA_HEADS = 8
NSA_GROUPS = 2
NSA_HPG = NSA_HEADS // NSA_GROUPS
HEAD_DIM = 64
CMP_LEN = 32
CMP_STRIDE = 16
SEL_LEN = 64
SEL_SHIFT = 6
SEL_TOP = 16
WINDOW = 512
FORCE_SCORE = 1e4
CONV_WIDTH = 31
FFN_CONV_WIDTH = 3
EPS = 1e-6
NEG = -1e30
MASK_BIAS = NEG
Q_SCALE = HEAD_DIM ** -0.5 * float(np.log2(np.e))

LANES = 128
SUBLANES = 8
BF16_ROWS = 16
VMEM_LIMIT = 56 * 1024 * 1024

TM = 1024
TM_ROWS = 512
TM_FFN = 512
TQ = 256
TK = 256
VT = 256
WIN_TILES = (WINDOW + TQ) // VT + (1 if TQ % VT else 0)
CONV_HALO = 32
FFN_HALO = BF16_ROWS


def _sigmoid(x):
    return 0.5 * (jnp.tanh(0.5 * x) + 1.0)


def _gelu_tanh(x):
    return 0.5 * x * (1.0 + jnp.tanh(np.sqrt(2.0 / np.pi).astype(np.float32) * (x + 0.044715 * (x * x * x))))


def _rms(x, g):
    return x * lax.rsqrt(jnp.mean(x * x, axis=-1, keepdims=True) + EPS) * g


def _zero_after(x):
    bits = lax.bitcast_convert_type(x, jnp.uint32)
    bits = lax.shift_right_logical(lax.shift_right_logical(bits, jnp.uint32(16)), jnp.uint32(16))
    return lax.bitcast_convert_type(bits, F32)


def _dot(a, b):
    return jnp.dot(a, b, preferred_element_type=F32)


def _dot_nt(a, b):
    return lax.dot_general(a, b, (((1,), (1,)), ((), ())), preferred_element_type=F32)


def _resident(shape):
    nd = len(shape)
    return pl.BlockSpec(shape, lambda *_: (0,) * nd, pipeline_mode=pl.Buffered(1))


_KA0, _KAN = 0, 256
_KW0, _KWN = 256, 128
_CV0, _CVN = 384, 256
_ZC0, _ZCN = 640, 1024
_ZM0, _ZMN = 1664, 2048
_N_TOK = 3712
_Q0, _QN = 0, 512
_VS0, _VSN = 512, 128
_VW0, _VWN = 640, 128
_ZG0, _ZGN = 768, 32
_N_FEAT = 800


def _inproj_kernel(x_ref, g_ref, wt_ref, wf_ref, qt_ref, ka_ref, kw_ref, cv_ref, c_ref, sg_ref, vst_ref, vwt_ref,
                   gtt_ref, *, tiles_per_seq):
    u = _rms(x_ref[...], g_ref[...]).astype(BF16)

    z_tok = _dot(u, wt_ref[...])
    z_feat = _dot_nt(wf_ref[...], u)

    def tok(lo, n):
        return z_tok[:, lo:lo + n]

    def feat(lo, n):
        return z_feat[lo:lo + n, :]

    ka = tok(_KA0, _KAN)
    s0 = (pl.program_id(0) % tiles_per_seq) * TM
    row = lax.broadcasted_iota(jnp.int32, ka.shape, 0) + s0
    col = lax.broadcasted_iota(jnp.int32, ka.shape, 1)
    onehot = ((col & HEAD_DIM) != 0) & ((col & (HEAD_DIM - 1)) == (row >> SEL_SHIFT))
    ka_ref[...] = jnp.where(onehot, 1.0, ka).astype(BF16)
    kw_ref[...] = tok(_KW0, _KWN).astype(BF16)
    cv_ref[...] = tok(_CV0, _CVN)
    zc = tok(_ZC0, _ZCN)
    half = _ZCN // 2
    c_ref[...] = zc[:, :half] * _sigmoid(zc[:, half:])
    sg_ref[...] = _sigmoid(tok(_ZM0, _ZMN))

    qt_ref[...] = feat(_Q0, _QN).astype(BF16)
    gtt_ref[...] = _sigmoid(feat(_ZG0, _ZGN))
    ones = jnp.ones((HEAD_DIM, VT), BF16)
    for src0, dst_ref in ((_VS0, vst_ref), (_VW0, vwt_ref)):
        v = feat(src0, NSA_GROUPS * HEAD_DIM).astype(BF16)
        for t in range(TM // VT):
            for g in range(NSA_GROUPS):
                r0 = g * 2 * HEAD_DIM
                dst_ref[t, r0:r0 + HEAD_DIM, :] = v[g * HEAD_DIM:(g + 1) * HEAD_DIM, t * VT:(t + 1) * VT]
                dst_ref[t, r0 + HEAD_DIM:r0 + 2 * HEAD_DIM, :] = ones


def _inproj(x2, g_mix, w_tok, w_feat, seq):
    m, d = x2.shape
    row = lambda n: pl.BlockSpec((TM, n), lambda i: (i, 0))
    colb = lambda n: pl.BlockSpec((n, TM), lambda i: (0, i))
    vt_spec = pl.BlockSpec((TM // VT, NSA_GROUPS * LANES, VT), lambda i: (i, 0, 0))
    vt_shape = jax.ShapeDtypeStruct((m // VT, NSA_GROUPS * LANES, VT), BF16)
    tok_outs = [(_KAN, BF16), (_KWN, BF16), (_CVN, F32), (_ZCN // 2, F32), (_ZMN, F32)]
    return pl.pallas_call(
        functools.partial(_inproj_kernel, tiles_per_seq=seq // TM),
        grid=(m // TM,),
        in_specs=[row(d), _resident((1, d)), _resident(w_tok.shape), _resident(w_feat.shape)],
        out_specs=[colb(_QN)] + [row(n) for n, _ in tok_outs] + [vt_spec, vt_spec, colb(_ZGN)],
        out_shape=([jax.ShapeDtypeStruct((_QN, m), BF16)]
                   + [jax.ShapeDtypeStruct((m, n), dt) for n, dt in tok_outs]
                   + [vt_shape, vt_shape, jax.ShapeDtypeStruct((_ZGN, m), F32)]),
        compiler_params=pltpu.CompilerParams(dimension_semantics=("parallel",), vmem_limit_bytes=VMEM_LIMIT),
        name="inproj",
    )(x2, g_mix, w_tok, w_feat)


N_CMP_PAD = 256
_CMP_HALF = CMP_LEN // 2


def _compress_kernel(ck_ref, cv_ref, pos_ref, w1_ref, w2_ref, w2t_ref, out_ref, outt_ref, *, n_cmp):
    hid = w1_ref.shape[-1]
    first = [jnp.zeros((N_CMP_PAD, hid), F32) for _ in range(4)]
    second = [jnp.zeros((N_CMP_PAD, hid), F32) for _ in range(4)]
    for l in range(_CMP_HALF):
        for kind, src_ref in enumerate((ck_ref, cv_ref)):
            xl = src_ref[pl.ds(l, N_CMP_PAD, stride=CMP_STRIDE), :]
            pcol = slice(kind * LANES, (kind + 1) * LANES)
            xa = (xl + pos_ref[l:l + 1, pcol]).astype(BF16)
            xb = (xl + pos_ref[_CMP_HALF + l:_CMP_HALF + l + 1, pcol]).astype(BF16)
            wa = w1_ref[kind, l * HEAD_DIM:(l + 1) * HEAD_DIM, :]
            wb = w1_ref[kind, (_CMP_HALF + l) * HEAD_DIM:(_CMP_HALF + l + 1) * HEAD_DIM, :]
            for g in range(NSA_GROUPS):
                kg = kind * NSA_GROUPS + g
                cols = slice(g * HEAD_DIM, (g + 1) * HEAD_DIM)
                first[kg] = first[kg] + _dot(xa[:, cols], wa)
                second[kg] = second[kg] + _dot(xb[:, cols], wb)
    rows = lax.broadcasted_iota(jnp.int32, (N_CMP_PAD, LANES), 0)
    lanes = lax.broadcasted_iota(jnp.int32, (LANES, N_CMP_PAD), 1)
    for g in range(NSA_GROUPS):
        acts = []
        for kind in range(2):
            kg = kind * NSA_GROUPS + g
            h1 = first[kg] + pltpu.roll(second[kg], N_CMP_PAD - 1, 0)
            acts.append(_gelu_tanh(h1).astype(BF16))
        kcv = _dot(acts[0], w2_ref[0]) + _dot(acts[1], w2_ref[1])
        kcvt = _dot_nt(w2t_ref[0], acts[0]) + _dot_nt(w2t_ref[1], acts[1])
        out_ref[0, g] = jnp.where(rows < n_cmp, kcv, 0.0).astype(BF16)
        outt_ref[0, g] = jnp.where(lanes < n_cmp, kcvt, 0.0).astype(BF16)


def _compress(cv, pos, w1, w2, w2t, batch, seq):
    n_cmp = (seq - CMP_LEN) // CMP_STRIDE + 1
    assert n_cmp < N_CMP_PAD and seq == N_CMP_PAD * CMP_STRIDE
    return pl.pallas_call(
        functools.partial(_compress_kernel, n_cmp=n_cmp),
        grid=(batch,),
        in_specs=[pl.BlockSpec((seq, LANES), lambda b: (b, 0)), pl.BlockSpec((seq, LANES), lambda b: (b, 1)),
                  _resident(pos.shape), _resident(w1.shape), _resident(w2.shape), _resident(w2t.shape)],
        out_specs=[pl.BlockSpec((1, NSA_GROUPS, N_CMP_PAD, LANES), lambda b: (b, 0, 0, 0)),
                   pl.BlockSpec((1, NSA_GROUPS, LANES, N_CMP_PAD), lambda b: (b, 0, 0, 0))],
        out_shape=[jax.ShapeDtypeStruct((batch, NSA_GROUPS, N_CMP_PAD, LANES), BF16),
                   jax.ShapeDtypeStruct((batch, NSA_GROUPS, LANES, N_CMP_PAD), BF16)],
        compiler_params=pltpu.CompilerParams(dimension_semantics=("parallel",), vmem_limit_bytes=VMEM_LIMIT),
        name="compress",
    )(cv, cv, pos, w1, w2, w2t)


N_SEL = 64


class _Group:
    def __init__(self, g, qb, refs):
        (self.qt_ref, self.kcv_ref, self.kcvt_ref, self.sel_ref, self.ka_ref, self.kw_ref, self.vst_ref,
         self.vwt_ref, self.gtt_ref, self.o_ref, q0_ref, q1_ref, qs_ref, m_ref, acc_ref, s_ref, smax_ref,
         p_ref, alpha_ref, oc_ref, ow_ref) = refs
        self.g = g
        self.t0 = qb * TQ
        self.cols = NSA_HPG * TQ
        self.rows = slice(g * LANES, (g + 1) * LANES)
        self.q0_ref, self.q1_ref, self.qs_ref = q0_ref.at[g], q1_ref.at[g], qs_ref.at[g]
        self.m_ref, self.acc_ref = m_ref.at[g], acc_ref.at[g]
        self.s_ref, self.smax_ref, self.p_ref, self.alpha_ref = s_ref.at[g], smax_ref.at[g], p_ref.at[g], alpha_ref.at[g]
        self.oc_ref, self.ow_ref = oc_ref.at[g], ow_ref.at[g]
        self.tq = self.t0 + (lax.broadcasted_iota(jnp.int32, (1, self.cols), 1) & (TQ - 1))
        self.ksub = lax.broadcasted_iota(jnp.int32, (TK, 1), 0)

    @staticmethod
    def per_head(x):
        return jnp.concatenate([x] * NSA_HPG, axis=1)

    def pre(self):
        g, t0, cols = self.g, self.t0, self.cols
        zeros = jnp.zeros((HEAD_DIM, TQ), BF16)
        for h in range(NSA_HPG):
            r0 = (g * NSA_HPG + h) * HEAD_DIM
            qh = self.qt_ref[r0:r0 + HEAD_DIM, :]
            cs = slice(h * TQ, (h + 1) * TQ)
            self.q0_ref[0:HEAD_DIM, cs] = qh
            self.q0_ref[HEAD_DIM:, cs] = zeros
            self.q1_ref[0:HEAD_DIM, cs] = zeros
            self.q1_ref[HEAD_DIM:, cs] = qh
            self.qs_ref[0:HEAD_DIM, cs] = qh
            self.qs_ref[HEAD_DIM:, cs] = zeros
        q0 = self.q0_ref[...]
        sc = _dot(self.kcv_ref[0, g], q0)
        wt0 = jnp.maximum(t0 - WINDOW, 0) // VT
        kstart = pl.multiple_of(wt0 * VT, VT)
        span = WIN_TILES * VT
        qwin = q0 if g == 0 else self.q1_ref[...]
        sw = _dot(self.kw_ref[pl.ds(kstart, span), :], qwin)

        tq1 = t0 + lax.broadcasted_iota(jnp.int32, (1, TQ), 1)
        cend = lax.broadcasted_iota(jnp.int32, (N_CMP_PAD, 1), 0) * CMP_STRIDE + (CMP_LEN - 1)
        sc = sc + self.per_head(jnp.where(cend <= tq1, 0.0, NEG))
        ec = jnp.exp2(sc - jnp.max(sc, axis=0, keepdims=True))
        lc = jnp.sum(ec, axis=0, keepdims=True)
        visible = self.tq >= CMP_LEN - 1
        dist = tq1 - (kstart + lax.broadcasted_iota(jnp.int32, (span, 1), 0))
        sw = sw + self.per_head(jnp.where(dist.astype(jnp.uint32) < WINDOW, 0.0, NEG))
        wmax = jnp.max(sw, axis=0, keepdims=True)
        pc = ec * (jnp.where(visible, 1.0 / lc, 0.0) + _zero_after(wmax))
        self.oc_ref[...] = _dot(self.kcvt_ref[0, g], pc.astype(BF16))[0:HEAD_DIM]

        psum = pc[:, 0:TQ]
        for h in range(1, NSA_HPG):
            psum = psum + pc[:, h * TQ:(h + 1) * TQ]
        sel = self.sel_ref[...]
        p_hi = psum.astype(BF16)
        rem = psum - p_hi.astype(F32)
        p_mid = rem.astype(BF16)
        p_lo = (rem - p_mid.astype(F32)).astype(BF16)
        imp = (_dot(sel, p_hi) + _dot(sel, p_mid)) + _dot(sel, p_lo)

        pw = jnp.exp2(sw - wmax).astype(BF16)
        vw = jnp.concatenate([self.vwt_ref[wt0 + c, self.rows, :] for c in range(WIN_TILES)], axis=1)
        ow = _dot(vw, pw)
        self.ow_ref[...] = ow[0:HEAD_DIM] * (1.0 / ow[HEAD_DIM:HEAD_DIM + 1])

        j = lax.broadcasted_iota(jnp.int32, (N_SEL, TQ), 0)
        cur = (t0 + lax.broadcasted_iota(jnp.int32, (N_SEL, TQ), 1)) >> SEL_SHIFT
        forced = (j == 0) | (j == cur) | (j == cur - 1)
        return jnp.where(j <= cur, jnp.where(forced, FORCE_SCORE, imp), NEG)

    def set_bias(self, row_blocks, score):
        jloc = lax.broadcasted_iota(jnp.int32, (SUBLANES, TQ), 0)
        ranks = []
        for r in row_blocks:
            blk = score[r * SUBLANES:(r + 1) * SUBLANES]
            rank = jnp.zeros((SUBLANES, TQ), jnp.int32)
            for i in range(N_SEL):
                vi = jnp.broadcast_to(score[i:i + 1, :], (SUBLANES, TQ))
                if r * SUBLANES > i:
                    before = vi >= blk
                elif (r + 1) * SUBLANES - 1 <= i:
                    before = vi > blk
                else:
                    before = (vi > blk) | ((vi == blk) & (jloc > i - r * SUBLANES))
                rank = rank + before.astype(jnp.int32)
            ranks.append(rank)
        bias = jnp.where(jnp.concatenate(ranks, axis=0) < SEL_TOP, 0.0, MASK_BIAS).astype(BF16)
        lo = HEAD_DIM + row_blocks[0] * SUBLANES
        for h in range(NSA_HPG):
            self.qs_ref[lo:lo + bias.shape[0], h * TQ:(h + 1) * TQ] = bias

    def start(self, score):
        self.m_ref[...] = jnp.full(self.m_ref.shape, NEG, F32)
        self.acc_ref[...] = jnp.zeros(self.acc_ref.shape, F32)
        self.p_ref[1] = jnp.zeros(self.p_ref.shape[1:], BF16)
        self.alpha_ref[1] = jnp.ones(self.alpha_ref.shape[1:], F32)
        n_row_blocks = N_SEL // SUBLANES
        first_blocks = max(TK // SEL_LEN, BF16_ROWS) // SUBLANES
        self.set_bias(list(range(first_blocks)), score)
        started = self.logits(0, 0)
        held = score + _zero_after(jnp.concatenate([started] * n_row_blocks, axis=0))
        self.set_bias(list(range(first_blocks, n_row_blocks)), held)

    def logits(self, kt, slot):
        k0 = pl.multiple_of(kt * TK, TK)
        s = _dot(self.ka_ref[pl.ds(k0, TK), self.rows], self.qs_ref[...])
        self.s_ref[slot] = s
        self.smax_ref[slot] = jnp.max(s, axis=0, keepdims=True)
        return s[0:SUBLANES, 0:TQ]

    def softmax(self, kt, slot, causal, after=None):
        s = self.s_ref[slot]
        if causal:
            s = jnp.where(kt * TK + self.ksub <= self.tq, s, NEG)
            smax = jnp.max(s, axis=0, keepdims=True)
        else:
            smax = self.smax_ref[slot]
        m_prev = self.m_ref[...]
        m_new = jnp.maximum(m_prev, smax)
        self.p_ref[slot] = jnp.exp2(s - m_new).astype(BF16)
        alpha = jnp.exp2(m_prev - m_new)
        if after is not None:
            alpha = alpha + _zero_after(self.per_head(after[0:1, :]))
        self.alpha_ref[slot] = alpha
        self.m_ref[...] = m_new

    def accumulate(self, kt, slot):
        p = self.p_ref[slot]
        v0 = jnp.maximum(kt, 0) * (TK // VT)
        v = jnp.concatenate([self.vst_ref[v0 + c, self.rows, :] for c in range(TK // VT)], axis=1)
        self.acc_ref[...] = self.alpha_ref[slot] * self.acc_ref[...] + _dot(v, p)

    def finish(self):
        g = self.g
        acc = self.acc_ref[...]
        o_s = acc[0:HEAD_DIM] * (1.0 / acc[HEAD_DIM:HEAD_DIM + 1])
        o_c = self.oc_ref[...]
        o_w = self.ow_ref[...]
        heads = []
        for h in range(NSA_HPG):
            cs = slice(h * TQ, (h + 1) * TQ)

            def gate(br):
                row = br * NSA_HEADS + g * NSA_HPG + h
                return self.gtt_ref[row:row + 1, :]

            heads.append(gate(0) * o_c[:, cs] + gate(1) * o_s[:, cs] + gate(2) * o_w[:, cs])
        width = NSA_HPG * HEAD_DIM
        self.o_ref[:, g * width:(g + 1) * width] = jnp.concatenate(heads, axis=0).T.astype(BF16)


def _attn_kernel(*refs):
    qb = pl.program_id(1)
    groups = [_Group(g, qb, refs) for g in range(NSA_GROUPS)]
    scores = [grp.pre() for grp in groups]
    for grp, score in zip(groups, scores):
        grp.start(score)
    even, odd = 0, 1
    last = qb // (TK // TQ)
    pairs = last // 2

    def body(jp, carry):
        a = 2 * jp
        for grp in groups:
            grp.accumulate(a - 1, odd)
            grp.softmax(a, even, False, after=grp.logits(a + 1, odd))
        for grp in groups:
            grp.accumulate(a, even)
            grp.softmax(a + 1, odd, False, after=grp.logits(a + 2, even))
        return carry

    lax.fori_loop(0, pairs, body, 0)
    a = 2 * pairs

    @pl.when(last == a)
    def _():
        for grp in groups:
            grp.accumulate(a - 1, odd)
            grp.softmax(a, even, True)
        for grp in groups:
            grp.accumulate(a, even)

    @pl.when(last != a)
    def _():
        for grp in groups:
            grp.accumulate(a - 1, odd)
            grp.softmax(a, even, False, after=grp.logits(a + 1, odd))
        for grp in groups:
            grp.accumulate(a, even)
            grp.softmax(a + 1, odd, True)
        for grp in groups:
            grp.accumulate(a + 1, odd)

    for grp in groups:
        grp.finish()


def _attention(qt, kcv, kcvt, sel, ka, kw, vst, vwt, gtt, batch, seq):
    nq = seq // TQ
    cols = NSA_HPG * TQ
    ng = NSA_GROUPS
    assert seq // SEL_LEN == N_SEL and WIN_TILES * VT <= seq and TK % VT == 0 and TK % TQ == 0
    assert WINDOW % VT == 0 and (TQ % VT == 0 or VT % TQ == 0)
    vt_spec = pl.BlockSpec((seq // VT, ng * LANES, VT), lambda b, i: (b, 0, 0))
    return pl.pallas_call(
        _attn_kernel,
        grid=(batch, nq),
        in_specs=[
            pl.BlockSpec((NSA_HEADS * HEAD_DIM, TQ), lambda b, i: (0, b * nq + i)),
            pl.BlockSpec((1, ng, N_CMP_PAD, LANES), lambda b, i: (b, 0, 0, 0)),
            pl.BlockSpec((1, ng, LANES, N_CMP_PAD), lambda b, i: (b, 0, 0, 0)),
            pl.BlockSpec(sel.shape, lambda b, i: (0, 0)),
            pl.BlockSpec((seq, ng * LANES), lambda b, i: (b, 0)),
            pl.BlockSpec((seq, LANES), lambda b, i: (b, 0)),
            vt_spec, vt_spec,
            pl.BlockSpec((_ZGN, TQ), lambda b, i: (0, b * nq + i)),
        ],
        out_specs=pl.BlockSpec((TQ, NSA_HEADS * HEAD_DIM), lambda b, i: (b * nq + i, 0)),
        out_shape=jax.ShapeDtypeStruct((batch * seq, NSA_HEADS * HEAD_DIM), BF16),
        scratch_shapes=[
            pltpu.VMEM((ng, LANES, cols), BF16),
            pltpu.VMEM((ng, LANES, cols), BF16),
            pltpu.VMEM((ng, LANES, cols), BF16),
            pltpu.VMEM((ng, 1, cols), F32),
            pltpu.VMEM((ng, LANES, cols), F32),
            pltpu.VMEM((ng, 2, TK, cols), F32),
            pltpu.VMEM((ng, 2, 1, cols), F32),
            pltpu.VMEM((ng, 2, TK, cols), BF16),
            pltpu.VMEM((ng, 2, 1, cols), F32),
            pltpu.VMEM((ng, HEAD_DIM, cols), F32),
            pltpu.VMEM((ng, HEAD_DIM, cols), F32),
        ],
        compiler_params=pltpu.CompilerParams(dimension_semantics=("parallel", "arbitrary"),
                                             vmem_limit_bytes=VMEM_LIMIT),
        name="nsa_attention",
    )(qt, kcv, kcvt, sel, ka, kw, vst, vwt, gtt)


def _merge_kernel(o_ref, c_ref, halo_ref, sg_ref, x_ref, wo_ref, cw_ref, cb_ref, lg_ref, lb_ref, wco_ref,
                  bco_ref, wout_ref, h_ref, cext_ref, *, tiles_per_seq):
    first = (pl.program_id(0) % tiles_per_seq) == 0
    cext_ref[0:CONV_HALO, :] = jnp.where(first, 0.0, halo_ref[...])
    cext_ref[CONV_HALO:, :] = c_ref[...]
    shift = CONV_HALO - (CONV_WIDTH - 1)
    rows = c_ref.shape[0]
    acc = jnp.broadcast_to(cb_ref[...], c_ref.shape)
    for res in range(SUBLANES):
        taps = [k for k in range(CONV_WIDTH) if (shift + k) % SUBLANES == res]
        n_rows = rows + (SUBLANES if res else 0)
        part = None
        for k in taps:
            base = shift + k - res
            term = cw_ref[k:k + 1, :] * cext_ref[base:base + n_rows, :]
            part = term if part is None else part + term
        acc = acc + part[res:res + rows]
    xc = acc - jnp.mean(acc, axis=-1, keepdims=True)
    y = xc * lax.rsqrt(jnp.mean(xc * xc, axis=-1, keepdims=True) + EPS) * lg_ref[...] + lb_ref[...]
    y_b = _dot((y * _sigmoid(y)).astype(BF16), wco_ref[...]) + bco_ref[...]
    y_a = _dot(o_ref[...], wo_ref[...])
    d = y_a.shape[-1]
    mix = sg_ref[:, :d] * y_a + sg_ref[:, d:] * y_b
    h_ref[...] = x_ref[...] + _dot(mix.astype(BF16), wout_ref[...])


def _merge(o, c, sg, x2, wo, cw, cb, lg, lb, wco, bco, wout, seq):
    m, d = x2.shape
    ch = c.shape[-1]
    tm = TM_ROWS
    assert seq % tm == 0
    row = lambda n: pl.BlockSpec((tm, n), lambda i: (i, 0))
    halo = pl.BlockSpec((CONV_HALO, ch), lambda i: (jnp.maximum(i * (tm // CONV_HALO) - 1, 0), 0))
    weights = [wo, cw, cb, lg, lb, wco, bco, wout]
    return pl.pallas_call(
        functools.partial(_merge_kernel, tiles_per_seq=seq // tm),
        grid=(m // tm,),
        in_specs=[row(o.shape[-1]), row(ch), halo, row(sg.shape[-1]), row(d)] + [_resident(w.shape) for w in weights],
        out_specs=row(d),
        out_shape=jax.ShapeDtypeStruct((m, d), F32),
        scratch_shapes=[pltpu.VMEM((CONV_HALO + tm, ch), F32)],
        compiler_params=pltpu.CompilerParams(dimension_semantics=("parallel",), vmem_limit_bytes=VMEM_LIMIT),
        name="merge",
    )(o, c, c, sg, x2, *weights)


def _ffn_kernel(h_ref, halo_ref, p_ref, gf_ref, wup_ref, fw_ref, fb_ref, wdn_ref, gp_ref, wpg_ref, wple_ref,
                gfin_ref, out_ref, u_ref, v_ref, *, tiles_per_seq, final_norm):
    first = (pl.program_id(0) % tiles_per_seq) == 0
    h = h_ref[...]
    u_ref[0:FFN_HALO, :] = _rms(jnp.where(first, 0.0, halo_ref[...]), gf_ref[...]).astype(BF16)
    u_ref[FFN_HALO:, :] = _rms(h, gf_ref[...]).astype(BF16)
    v_ref[...] = _dot(u_ref[...], wup_ref[...])
    shift = FFN_HALO - (FFN_CONV_WIDTH - 1)
    rows = h.shape[0]
    v = fb_ref[...]
    for k in range(FFN_CONV_WIDTH):
        v = v + fw_ref[k:k + 1, :] * v_ref[shift + k:shift + k + rows, :]
    d_ff = wdn_ref.shape[0]
    a = _gelu_tanh(v[:, :d_ff]) * v[:, d_ff:]
    h = h + _dot(a.astype(BF16), wdn_ref[...])
    gate = _sigmoid(_dot(_rms(h, gp_ref[...]).astype(BF16), wpg_ref[...]))
    h = h + gate * _dot(p_ref[...].astype(BF16), wple_ref[...])
    out_ref[...] = _rms(h, gfin_ref[...]) if final_norm else h


def _ffn(h1, p3, layer, gf, wup, fw, fb, wdn, gp, wpg, wple, gfin, seq, final_norm):
    m, d = h1.shape
    assert wdn.shape[0] % LANES == 0 and seq % TM_FFN == 0
    tm = TM_FFN
    row = lambda n: pl.BlockSpec((tm, n), lambda i: (i, 0))
    halo = pl.BlockSpec((FFN_HALO, d), lambda i: (jnp.maximum(i * (tm // FFN_HALO) - 1, 0), 0))
    weights = [gf, wup, fw, fb, wdn, gp, wpg, wple, gfin]
    return pl.pallas_call(
        functools.partial(_ffn_kernel, tiles_per_seq=seq // tm, final_norm=final_norm),
        grid=(m // tm,),
        in_specs=[row(d), halo, pl.BlockSpec((None, tm, p3.shape[-1]), lambda i: (layer, i, 0))]
        + [_resident(w.shape) for w in weights],
        out_specs=row(d),
        out_shape=jax.ShapeDtypeStruct((m, d), F32),
        scratch_shapes=[pltpu.VMEM((FFN_HALO + tm, d), BF16), pltpu.VMEM((FFN_HALO + tm, wup.shape[1]), F32)],
        compiler_params=pltpu.CompilerParams(dimension_semantics=("parallel",), vmem_limit_bytes=VMEM_LIMIT),
        name="ffn_ple",
    )(h1, h1, p3, *weights)


def _sel_map_t(seq):
    n_cmp = (seq - CMP_LEN) // CMP_STRIDE + 1
    n_sel = seq // SEL_LEN
    c0 = np.arange(n_cmp) * CMP_STRIDE
    j0 = np.arange(n_sel) * SEL_LEN
    lo = np.maximum(c0[None, :], j0[:, None])
    hi = np.minimum(c0[None, :] + CMP_LEN, j0[:, None] + SEL_LEN)
    out = np.zeros((N_SEL, N_CMP_PAD), np.float32)
    out[:n_sel, :n_cmp] = np.maximum(hi - lo, 0) / CMP_LEN
    return jnp.asarray(out, BF16)


def _inproj_weights(w):
    d = w.shape[0]
    nq = NSA_HEADS * HEAD_DIM
    kv0 = nq
    gate0 = kv0 + 3 * 2 * NSA_GROUPS * HEAD_DIM
    conv0 = gate0 + 3 * NSA_HEADS
    merge0 = conv0 + _ZCN

    def kv(br, g):
        lo = kv0 + (br * NSA_GROUPS + g) * HEAD_DIM
        return w[:, lo:lo + HEAD_DIM]

    z = jnp.zeros((d, HEAD_DIM), w.dtype)
    tok = [kv(2, 0), z, kv(2, 1), z, kv(4, 0), kv(4, 1), w[:, kv0:kv0 + _CVN], w[:, conv0:merge0],
           w[:, merge0:merge0 + _ZMN]]
    feat = [w[:, :nq] * Q_SCALE, kv(3, 0), kv(3, 1), kv(5, 0), kv(5, 1), w[:, gate0:conv0],
            jnp.zeros((d, _ZGN - 3 * NSA_HEADS), w.dtype)]
    w_tok = jnp.concatenate(tok, axis=1).astype(BF16)
    w_feat = jnp.concatenate(feat, axis=1).T.astype(BF16)
    assert w_tok.shape[1] == _N_TOK and w_feat.shape[0] == _N_FEAT
    return w_tok, w_feat


def kernel(x, p, g_mix, w_in, cmp_pos_k, cmp_pos_v, w_cmp_k1, w_cmp_k2, w_cmp_v1, w_cmp_v2, w_o_nsa, conv_w,
           conv_b, conv_ln_g, conv_ln_b, w_conv_out, b_conv_out, w_out, g_ffn, w_up, ffn_conv_w, ffn_conv_b,
           w_down, g_ple, w_ple_gate, w_ple, g_final):
    batch, seq, d = x.shape
    depth = w_in.shape[0]
    m = batch * seq
    assert seq % TM == 0 and seq % TK == 0
    sel = _sel_map_t(seq)
    row = lambda v: v.reshape(1, -1)
    h = x.reshape(m, d)
    for i in range(depth):
        w_tok, w_feat = _inproj_weights(w_in[i])
        pos = jnp.concatenate([cmp_pos_k[i]] * NSA_GROUPS + [cmp_pos_v[i]] * NSA_GROUPS, axis=1)
        w1 = jnp.stack([w_cmp_k1[i], w_cmp_v1[i]]).astype(BF16)
        zpad = jnp.zeros_like(w_cmp_k2[i])
        w2 = jnp.stack([jnp.concatenate([w_cmp_k2[i], zpad], axis=1),
                        jnp.concatenate([zpad, w_cmp_v2[i]], axis=1)]).astype(BF16)
        w2t = jnp.stack([jnp.concatenate([zpad, w_cmp_k2[i]], axis=1).T,
                         jnp.concatenate([w_cmp_v2[i], zpad], axis=1).T]).astype(BF16)

        qt, ka, kw, cv, c, sg, vst, vwt, gtt = _inproj(h, row(g_mix[i]), w_tok, w_feat, seq)
        kcv, kcvt = _compress(cv, pos, w1, w2, w2t, batch, seq)
        o = _attention(qt, kcv, kcvt, sel, ka, kw, vst, vwt, gtt, batch, seq)
        h = _merge(o, c, sg, h, w_o_nsa[i].astype(BF16), conv_w[i], row(conv_b[i]), row(conv_ln_g[i]),
                   row(conv_ln_b[i]), w_conv_out[i].astype(BF16), row(b_conv_out[i]), w_out[i].astype(BF16), seq)
        h = _ffn(h, p.reshape(depth, m, -1), i, row(g_ffn[i]), w_up[i].astype(BF16), ffn_conv_w[i], row(ffn_conv_b[i]),
                 w_down[i].astype(BF16), row(g_ple[i]), w_ple_gate[i].astype(BF16), w_ple[i].astype(BF16),
                 row(g_final), seq, final_norm=(i == depth - 1))
    return h.reshape(batch, seq, d)
```

```python
import functools

import numpy as np
import jax
import jax.numpy as jnp
from jax import lax
from jax.experimental import pallas as pl
from jax.experimental.pallas import tpu as pltpu

F32 = jnp.float32
BF16 = jnp.bfloat16

NSA_HEADS = 8
NSA_GROUPS = 2
NSA_HPG = NSA_HEADS // NSA_GROUPS
HEAD_DIM = 64
CMP_LEN = 32
CMP_STRIDE = 16
SEL_LEN = 64
SEL_SHIFT = 6
SEL_TOP = 16
WINDOW = 512
FORCE_SCORE = 1e4
CONV_WIDTH = 31
FFN_CONV_WIDTH = 3
EPS = 1e-6
NEG = -1e30
MASK_BIAS = NEG
Q_SCALE = HEAD_DIM ** -0.5 * float(np.log2(np.e))

LANES = 128
SUBLANES = 8
BF16_ROWS = 16
VMEM_LIMIT = 56 * 1024 * 1024

TM = 1024
TM_ROWS = 512
TM_FFN = 512
TQ = 256
TK = 256
VT = 256
WIN_TILES = (WINDOW + TQ) // VT + (1 if TQ % VT else 0)
CONV_HALO = 32
FFN_HALO = BF16_ROWS


def _sigmoid(x):
    return 0.5 * (jnp.tanh(0.5 * x) + 1.0)


def _gelu_tanh(x):
    return 0.5 * x * (1.0 + jnp.tanh(np.sqrt(2.0 / np.pi).astype(np.float32) * (x + 0.044715 * (x * x * x))))


def _rms(x, g):
    return x * lax.rsqrt(jnp.mean(x * x, axis=-1, keepdims=True) + EPS) * g


def _zero_after(x):
    bits = lax.bitcast_convert_type(x, jnp.uint32)
    bits = lax.shift_right_logical(lax.shift_right_logical(bits, jnp.uint32(16)), jnp.uint32(16))
    return lax.bitcast_convert_type(bits, F32)


def _dot(a, b):
    return jnp.dot(a, b, preferred_element_type=F32)


def _dot_nt(a, b):
    return lax.dot_general(a, b, (((1,), (1,)), ((), ())), preferred_element_type=F32)


def _resident(shape):
    nd = len(shape)
    return pl.BlockSpec(shape, lambda *_: (0,) * nd, pipeline_mode=pl.Buffered(1))


_KA0, _KAN = 0, 256
_KW0, _KWN = 256, 128
_CV0, _CVN = 384, 256
_ZC0, _ZCN = 640, 1024
_ZM0, _ZMN = 1664, 2048
_N_TOK = 3712
_Q0, _QN = 0, 512
_VS0, _VSN = 512, 128
_VW0, _VWN = 640, 128
_ZG0, _ZGN = 768, 32
_N_FEAT = 800


def _inproj_kernel(x_ref, g_ref, wt_ref, wf_ref, qt_ref, ka_ref, kw_ref, cv_ref, c_ref, sg_ref, vst_ref, vwt_ref,
                   gtt_ref, *, tiles_per_seq):
    u = _rms(x_ref[...], g_ref[...]).astype(BF16)

    z_tok = _dot(u, wt_ref[...])
    z_feat = _dot_nt(wf_ref[...], u)

    def tok(lo, n):
        return z_tok[:, lo:lo + n]

    def feat(lo, n):
        return z_feat[lo:lo + n, :]

    ka = tok(_KA0, _KAN)
    s0 = (pl.program_id(0) % tiles_per_seq) * TM
    row = lax.broadcasted_iota(jnp.int32, ka.shape, 0) + s0
    col = lax.broadcasted_iota(jnp.int32, ka.shape, 1)
    onehot = ((col & HEAD_DIM) != 0) & ((col & (HEAD_DIM - 1)) == (row >> SEL_SHIFT))
    ka_ref[...] = jnp.where(onehot, 1.0, ka).astype(BF16)
    kw_ref[...] = tok(_KW0, _KWN).astype(BF16)
    cv_ref[...] = tok(_CV0, _CVN)
    zc = tok(_ZC0, _ZCN)
    half = _ZCN // 2
    c_ref[...] = zc[:, :half] * _sigmoid(zc[:, half:])
    sg_ref[...] = _sigmoid(tok(_ZM0, _ZMN))

    qt_ref[...] = feat(_Q0, _QN).astype(BF16)
    gtt_ref[...] = _sigmoid(feat(_ZG0, _ZGN))
    ones = jnp.ones((HEAD_DIM, VT), BF16)
    for src0, dst_ref in ((_VS0, vst_ref), (_VW0, vwt_ref)):
        v = feat(src0, NSA_GROUPS * HEAD_DIM).astype(BF16)
        for t in range(TM // VT):
            for g in range(NSA_GROUPS):
                r0 = g * 2 * HEAD_DIM
                dst_ref[t, r0:r0 + HEAD_DIM, :] = v[g * HEAD_DIM:(g + 1) * HEAD_DIM, t * VT:(t + 1) * VT]
                dst_ref[t, r0 + HEAD_DIM:r0 + 2 * HEAD_DIM, :] = ones


def _inproj(x2, g_mix, w_tok, w_feat, seq):
    m, d = x2.shape
    row = lambda n: pl.BlockSpec((TM, n), lambda i: (i, 0))
    colb = lambda n: pl.BlockSpec((n, TM), lambda i: (0, i))
    vt_spec = pl.BlockSpec((TM // VT, NSA_GROUPS * LANES, VT), lambda i: (i, 0, 0))
    vt_shape = jax.ShapeDtypeStruct((m // VT, NSA_GROUPS * LANES, VT), BF16)
    tok_outs = [(_KAN, BF16), (_KWN, BF16), (_CVN, F32), (_ZCN // 2, F32), (_ZMN, F32)]
    return pl.pallas_call(
        functools.partial(_inproj_kernel, tiles_per_seq=seq // TM),
        grid=(m // TM,),
        in_specs=[row(d), _resident((1, d)), _resident(w_tok.shape), _resident(w_feat.shape)],
        out_specs=[colb(_QN)] + [row(n) for n, _ in tok_outs] + [vt_spec, vt_spec, colb(_ZGN)],
        out_shape=([jax.ShapeDtypeStruct((_QN, m), BF16)]
                   + [jax.ShapeDtypeStruct((m, n), dt) for n, dt in tok_outs]
                   + [vt_shape, vt_shape, jax.ShapeDtypeStruct((_ZGN, m), F32)]),
        compiler_params=pltpu.CompilerParams(dimension_semantics=("parallel",), vmem_limit_bytes=VMEM_LIMIT),
        name="inproj",
    )(x2, g_mix, w_tok, w_feat)


N_CMP_PAD = 256
_CMP_HALF = CMP_LEN // 2


def _compress_kernel(ck_ref, cv_ref, pos_ref, w1_ref, w2_ref, w2t_ref, out_ref, outt_ref, *, n_cmp):
    hid = w1_ref.shape[-1]
    first = [jnp.zeros((N_CMP_PAD, hid), F32) for _ in range(4)]
    second = [jnp.zeros((N_CMP_PAD, hid), F32) for _ in range(4)]
    for l in range(_CMP_HALF):
        for kind, src_ref in enumerate((ck_ref, cv_ref)):
            xl = src_ref[pl.ds(l, N_CMP_PAD, stride=CMP_STRIDE), :]
            pcol = slice(kind * LANES, (kind + 1) * LANES)
            xa = (xl + pos_ref[l:l + 1, pcol]).astype(BF16)
            xb = (xl + pos_ref[_CMP_HALF + l:_CMP_HALF + l + 1, pcol]).astype(BF16)
            wa = w1_ref[kind, l * HEAD_DIM:(l + 1) * HEAD_DIM, :]
            wb = w1_ref[kind, (_CMP_HALF + l) * HEAD_DIM:(_CMP_HALF + l + 1) * HEAD_DIM, :]
            for g in range(NSA_GROUPS):
                kg = kind * NSA_GROUPS + g
                cols = slice(g * HEAD_DIM, (g + 1) * HEAD_DIM)
                first[kg] = first[kg] + _dot(xa[:, cols], wa)
                second[kg] = second[kg] + _dot(xb[:, cols], wb)
    rows = lax.broadcasted_iota(jnp.int32, (N_CMP_PAD, LANES), 0)
    lanes = lax.broadcasted_iota(jnp.int32, (LANES, N_CMP_PAD), 1)
    for g in range(NSA_GROUPS):
        acts = []
        for kind in range(2):
            kg = kind * NSA_GROUPS + g
            h1 = first[kg] + pltpu.roll(second[kg], N_CMP_PAD - 1, 0)
            acts.append(_gelu_tanh(h1).astype(BF16))
        kcv = _dot(acts[0], w2_ref[0]) + _dot(acts[1], w2_ref[1])
        kcvt = _dot_nt(w2t_ref[0], acts[0]) + _dot_nt(w2t_ref[1], acts[1])
        out_ref[0, g] = jnp.where(rows < n_cmp, kcv, 0.0).astype(BF16)
        outt_ref[0, g] = jnp.where(lanes < n_cmp, kcvt, 0.0).astype(BF16)


def _compress(cv, pos, w1, w2, w2t, batch, seq):
    n_cmp = (seq - CMP_LEN) // CMP_STRIDE + 1
    assert n_cmp < N_CMP_PAD and seq == N_CMP_PAD * CMP_STRIDE
    return pl.pallas_call(
        functools.partial(_compress_kernel, n_cmp=n_cmp),
        grid=(batch,),
        in_specs=[pl.BlockSpec((seq, LANES), lambda b: (b, 0)), pl.BlockSpec((seq, LANES), lambda b: (b, 1)),
                  _resident(pos.shape), _resident(w1.shape), _resident(w2.shape), _resident(w2t.shape)],
        out_specs=[pl.BlockSpec((1, NSA_GROUPS, N_CMP_PAD, LANES), lambda b: (b, 0, 0, 0)),
                   pl.BlockSpec((1, NSA_GROUPS, LANES, N_CMP_PAD), lambda b: (b, 0, 0, 0))],
        out_shape=[jax.ShapeDtypeStruct((batch, NSA_GROUPS, N_CMP_PAD, LANES), BF16),
                   jax.ShapeDtypeStruct((batch, NSA_GROUPS, LANES, N_CMP_PAD), BF16)],
        compiler_params=pltpu.CompilerParams(dimension_semantics=("parallel",), vmem_limit_bytes=VMEM_LIMIT),
        name="compress",
    )(cv, cv, pos, w1, w2, w2t)


N_SEL = 64


class _Group:
    def __init__(self, stream, g, qb, col0, refs):
        (self.qt_ref, self.kcv_ref, self.kcvt_ref, self.sel_ref, self.ka_ref, self.kw_ref, self.vst_ref,
         self.vwt_ref, self.gtt_ref, self.o_ref, q0_ref, q1_ref, qs_ref, m_ref, acc_ref, s_ref, smax_ref,
         p_ref, alpha_ref, oc_ref, ow_ref) = refs
        self.g = g
        self.t0 = qb * TQ
        self.cols = NSA_HPG * TQ
        self.rows = slice(g * LANES, (g + 1) * LANES)
        self.qcols = slice(col0, col0 + TQ)
        k = stream
        self.q0_ref, self.q1_ref, self.qs_ref = q0_ref.at[k], q1_ref.at[k], qs_ref.at[k]
        self.m_ref, self.acc_ref = m_ref.at[k], acc_ref.at[k]
        self.s_ref, self.smax_ref, self.p_ref, self.alpha_ref = s_ref.at[k], smax_ref.at[k], p_ref.at[k], alpha_ref.at[k]
        self.oc_ref, self.ow_ref = oc_ref.at[k], ow_ref.at[k]
        self.tq = self.t0 + (lax.broadcasted_iota(jnp.int32, (1, self.cols), 1) & (TQ - 1))
        self.ksub = lax.broadcasted_iota(jnp.int32, (TK, 1), 0)

    @staticmethod
    def per_head(x):
        return jnp.concatenate([x] * NSA_HPG, axis=1)

    def pre(self):
        g, t0, cols = self.g, self.t0, self.cols
        zeros = jnp.zeros((HEAD_DIM, TQ), BF16)
        for h in range(NSA_HPG):
            r0 = (g * NSA_HPG + h) * HEAD_DIM
            qh = self.qt_ref[r0:r0 + HEAD_DIM, self.qcols]
            cs = slice(h * TQ, (h + 1) * TQ)
            self.q0_ref[0:HEAD_DIM, cs] = qh
            self.q0_ref[HEAD_DIM:, cs] = zeros
            self.q1_ref[0:HEAD_DIM, cs] = zeros
            self.q1_ref[HEAD_DIM:, cs] = qh
            self.qs_ref[0:HEAD_DIM, cs] = qh
            self.qs_ref[HEAD_DIM:, cs] = zeros
        q0 = self.q0_ref[...]
        sc = _dot(self.kcv_ref[0, g], q0)
        wt0 = jnp.maximum(t0 - WINDOW, 0) // VT
        kstart = pl.multiple_of(wt0 * VT, VT)
        span = WIN_TILES * VT
        qwin = q0 if g == 0 else self.q1_ref[...]
        sw = _dot(self.kw_ref[pl.ds(kstart, span), :], qwin)

        tq1 = t0 + lax.broadcasted_iota(jnp.int32, (1, TQ), 1)
        cend = lax.broadcasted_iota(jnp.int32, (N_CMP_PAD, 1), 0) * CMP_STRIDE + (CMP_LEN - 1)
        sc = sc + self.per_head(jnp.where(cend <= tq1, 0.0, NEG))
        ec = jnp.exp2(sc - jnp.max(sc, axis=0, keepdims=True))
        lc = jnp.sum(ec, axis=0, keepdims=True)
        visible = self.tq >= CMP_LEN - 1
        dist = tq1 - (kstart + lax.broadcasted_iota(jnp.int32, (span, 1), 0))
        sw = sw + self.per_head(jnp.where(dist.astype(jnp.uint32) < WINDOW, 0.0, NEG))
        wmax = jnp.max(sw, axis=0, keepdims=True)
        pc = ec * (jnp.where(visible, 1.0 / lc, 0.0) + _zero_after(wmax))
        self.oc_ref[...] = _dot(self.kcvt_ref[0, g], pc.astype(BF16))[0:HEAD_DIM]

        psum = pc[:, 0:TQ]
        for h in range(1, NSA_HPG):
            psum = psum + pc[:, h * TQ:(h + 1) * TQ]
        sel = self.sel_ref[...]
        p_hi = psum.astype(BF16)
        rem = psum - p_hi.astype(F32)
        p_mid = rem.astype(BF16)
        p_lo = (rem - p_mid.astype(F32)).astype(BF16)
        imp = (_dot(sel, p_hi) + _dot(sel, p_mid)) + _dot(sel, p_lo)

        pw = jnp.exp2(sw - wmax).astype(BF16)
        vw = jnp.concatenate([self.vwt_ref[wt0 + c, self.rows, :] for c in range(WIN_TILES)], axis=1)
        ow = _dot(vw, pw)
        self.ow_ref[...] = ow[0:HEAD_DIM] * (1.0 / ow[HEAD_DIM:HEAD_DIM + 1])

        j = lax.broadcasted_iota(jnp.int32, (N_SEL, TQ), 0)
        cur = (t0 + lax.broadcasted_iota(jnp.int32, (N_SEL, TQ), 1)) >> SEL_SHIFT
        forced = (j == 0) | (j == cur) | (j == cur - 1)
        return jnp.where(j <= cur, jnp.where(forced, FORCE_SCORE, imp), NEG)

    def set_bias(self, row_blocks, score):
        jloc = lax.broadcasted_iota(jnp.int32, (SUBLANES, TQ), 0)
        ranks = []
        for r in row_blocks:
            blk = score[r * SUBLANES:(r + 1) * SUBLANES]
            rank = jnp.zeros((SUBLANES, TQ), jnp.int32)
            for i in range(N_SEL):
                vi = jnp.broadcast_to(score[i:i + 1, :], (SUBLANES, TQ))
                if r * SUBLANES > i:
                    before = vi >= blk
                elif (r + 1) * SUBLANES - 1 <= i:
                    before = vi > blk
                else:
                    before = (vi > blk) | ((vi == blk) & (jloc > i - r * SUBLANES))
                rank = rank + before.astype(jnp.int32)
            ranks.append(rank)
        bias = jnp.where(jnp.concatenate(ranks, axis=0) < SEL_TOP, 0.0, MASK_BIAS).astype(BF16)
        lo = HEAD_DIM + row_blocks[0] * SUBLANES
        for h in range(NSA_HPG):
            self.qs_ref[lo:lo + bias.shape[0], h * TQ:(h + 1) * TQ] = bias

    def start(self, score):
        self.m_ref[...] = jnp.full(self.m_ref.shape, NEG, F32)
        self.acc_ref[...] = jnp.zeros(self.acc_ref.shape, F32)
        self.p_ref[1] = jnp.zeros(self.p_ref.shape[1:], BF16)
        self.alpha_ref[1] = jnp.ones(self.alpha_ref.shape[1:], F32)
        n_row_blocks = N_SEL // SUBLANES
        first_blocks = max(TK // SEL_LEN, BF16_ROWS) // SUBLANES
        self.set_bias(list(range(first_blocks)), score)
        started = self.logits(0, 0)
        held = score + _zero_after(jnp.concatenate([started] * n_row_blocks, axis=0))
        self.set_bias(list(range(first_blocks, n_row_blocks)), held)

    def logits(self, kt, slot):
        k0 = pl.multiple_of(kt * TK, TK)
        s = _dot(self.ka_ref[pl.ds(k0, TK), self.rows], self.qs_ref[...])
        self.s_ref[slot] = s
        self.smax_ref[slot] = jnp.max(s, axis=0, keepdims=True)
        return s[0:SUBLANES, 0:TQ]

    def softmax(self, kt, slot, causal, after=None):
        s = self.s_ref[slot]
        if causal:
            s = jnp.where(kt * TK + self.ksub <= self.tq, s, NEG)
            smax = jnp.max(s, axis=0, keepdims=True)
        else:
            smax = self.smax_ref[slot]
        m_prev = self.m_ref[...]
        m_new = jnp.maximum(m_prev, smax)
        self.p_ref[slot] = jnp.exp2(s - m_new).astype(BF16)
        alpha = jnp.exp2(m_prev - m_new)
        if after is not None:
            alpha = alpha + _zero_after(self.per_head(after[0:1, :]))
        self.alpha_ref[slot] = alpha
        self.m_ref[...] = m_new

    def accumulate(self, kt, slot):
        p = self.p_ref[slot]
        v0 = jnp.maximum(kt, 0) * (TK // VT)
        v = jnp.concatenate([self.vst_ref[v0 + c, self.rows, :] for c in range(TK // VT)], axis=1)
        self.acc_ref[...] = self.alpha_ref[slot] * self.acc_ref[...] + _dot(v, p)

    def finish(self):
        g = self.g
        acc = self.acc_ref[...]
        o_s = acc[0:HEAD_DIM] * (1.0 / acc[HEAD_DIM:HEAD_DIM + 1])
        o_c = self.oc_ref[...]
        o_w = self.ow_ref[...]
        heads = []
        for h in range(NSA_HPG):
            cs = slice(h * TQ, (h + 1) * TQ)

            def gate(br):
                row = br * NSA_HEADS + g * NSA_HPG + h
                return self.gtt_ref[row:row + 1, self.qcols]

            heads.append(gate(0) * o_c[:, cs] + gate(1) * o_s[:, cs] + gate(2) * o_w[:, cs])
        width = NSA_HPG * HEAD_DIM
        self.o_ref[self.qcols, g * width:(g + 1) * width] = jnp.concatenate(heads, axis=0).T.astype(BF16)


Q_BLOCKS = 2


def _attn_kernel(*refs):
    step = pl.program_id(1)
    streams = []
    for blk in range(Q_BLOCKS):
        for g in range(NSA_GROUPS):
            streams.append(_Group(len(streams), g, Q_BLOCKS * step + blk, blk * TQ, refs))
    ends_even = streams[:NSA_GROUPS]
    ends_odd = streams[NSA_GROUPS:]
    scores = [st.pre() for st in streams]
    for st, score in zip(streams, scores):
        st.start(score)
    even, odd = 0, 1

    def body(jp, carry):
        a = 2 * jp
        for st in streams:
            st.accumulate(a - 1, odd)
        for st in streams:
            st.softmax(a, even, False, after=st.logits(a + 1, odd))
        for st in streams:
            st.accumulate(a, even)
        for st in streams:
            st.softmax(a + 1, odd, False, after=st.logits(a + 2, even))
        return carry

    lax.fori_loop(0, step, body, 0)
    a = 2 * step
    for st in streams:
        st.accumulate(a - 1, odd)
    for st in ends_even:
        st.softmax(a, even, True)
    for st in ends_odd:
        st.softmax(a, even, False, after=st.logits(a + 1, odd))
    for st in streams:
        st.accumulate(a, even)
    for st in ends_odd:
        st.softmax(a + 1, odd, True)
    for st in ends_odd:
        st.accumulate(a + 1, odd)
    for st in streams:
        st.finish()


def _attention(qt, kcv, kcvt, sel, ka, kw, vst, vwt, gtt, batch, seq):
    tq = Q_BLOCKS * TQ
    nq = seq // tq
    cols = NSA_HPG * TQ
    ng = NSA_GROUPS
    ns = Q_BLOCKS * ng
    assert seq // SEL_LEN == N_SEL and WIN_TILES * VT <= seq and TK % VT == 0 and seq % tq == 0
    assert TK == TQ and Q_BLOCKS == 2, "the static pipeline drain relies on block qb ending on tile qb"
    assert WINDOW % VT == 0 and (TQ % VT == 0 or VT % TQ == 0)
    vt_spec = pl.BlockSpec((seq // VT, ng * LANES, VT), lambda b, i: (b, 0, 0))
    return pl.pallas_call(
        _attn_kernel,
        grid=(batch, nq),
        in_specs=[
            pl.BlockSpec((NSA_HEADS * HEAD_DIM, tq), lambda b, i: (0, b * nq + i)),
            pl.BlockSpec((1, ng, N_CMP_PAD, LANES), lambda b, i: (b, 0, 0, 0)),
            pl.BlockSpec((1, ng, LANES, N_CMP_PAD), lambda b, i: (b, 0, 0, 0)),
            pl.BlockSpec(sel.shape, lambda b, i: (0, 0)),
            pl.BlockSpec((seq, ng * LANES), lambda b, i: (b, 0)),
            pl.BlockSpec((seq, LANES), lambda b, i: (b, 0)),
            vt_spec, vt_spec,
            pl.BlockSpec((_ZGN, tq), lambda b, i: (0, b * nq + i)),
        ],
        out_specs=pl.BlockSpec((tq, NSA_HEADS * HEAD_DIM), lambda b, i: (b * nq + i, 0)),
        out_shape=jax.ShapeDtypeStruct((batch * seq, NSA_HEADS * HEAD_DIM), BF16),
        scratch_shapes=[
            pltpu.VMEM((ns, LANES, cols), BF16),
            pltpu.VMEM((ns, LANES, cols), BF16),
            pltpu.VMEM((ns, LANES, cols), BF16),
            pltpu.VMEM((ns, 1, cols), F32),
            pltpu.VMEM((ns, LANES, cols), F32),
            pltpu.VMEM((ns, 2, TK, cols), F32),
            pltpu.VMEM((ns, 2, 1, cols), F32),
            pltpu.VMEM((ns, 2, TK, cols), BF16),
            pltpu.VMEM((ns, 2, 1, cols), F32),
            pltpu.VMEM((ns, HEAD_DIM, cols), F32),
            pltpu.VMEM((ns, HEAD_DIM, cols), F32),
        ],
        compiler_params=pltpu.CompilerParams(dimension_semantics=("parallel", "arbitrary"),
                                             vmem_limit_bytes=VMEM_LIMIT),
        name="nsa_attention",
    )(qt, kcv, kcvt, sel, ka, kw, vst, vwt, gtt)


def _merge_kernel(o_ref, c_ref, halo_ref, sg_ref, x_ref, wo_ref, cw_ref, cb_ref, lg_ref, lb_ref, wco_ref,
                  bco_ref, wout_ref, h_ref, cext_ref, *, tiles_per_seq):
    first = (pl.program_id(0) % tiles_per_seq) == 0
    cext_ref[0:CONV_HALO, :] = jnp.where(first, 0.0, halo_ref[...])
    cext_ref[CONV_HALO:, :] = c_ref[...]
    shift = CONV_HALO - (CONV_WIDTH - 1)
    rows = c_ref.shape[0]
    acc = jnp.broadcast_to(cb_ref[...], c_ref.shape)
    for res in range(SUBLANES):
        taps = [k for k in range(CONV_WIDTH) if (shift + k) % SUBLANES == res]
        n_rows = rows + (SUBLANES if res else 0)
        part = None
        for k in taps:
            base = shift + k - res
            term = cw_ref[k:k + 1, :] * cext_ref[base:base + n_rows, :]
            part = term if part is None else part + term
        acc = acc + part[res:res + rows]
    xc = acc - jnp.mean(acc, axis=-1, keepdims=True)
    y = xc * lax.rsqrt(jnp.mean(xc * xc, axis=-1, keepdims=True) + EPS) * lg_ref[...] + lb_ref[...]
    y_b = _dot((y * _sigmoid(y)).astype(BF16), wco_ref[...]) + bco_ref[...]
    y_a = _dot(o_ref[...], wo_ref[...])
    d = y_a.shape[-1]
    mix = sg_ref[:, :d] * y_a + sg_ref[:, d:] * y_b
    h_ref[...] = x_ref[...] + _dot(mix.astype(BF16), wout_ref[...])


def _merge(o, c, sg, x2, wo, cw, cb, lg, lb, wco, bco, wout, seq):
    m, d = x2.shape
    ch = c.shape[-1]
    tm = TM_ROWS
    assert seq % tm == 0
    row = lambda n: pl.BlockSpec((tm, n), lambda i: (i, 0))
    halo = pl.BlockSpec((CONV_HALO, ch), lambda i: (jnp.maximum(i * (tm // CONV_HALO) - 1, 0), 0))
    weights = [wo, cw, cb, lg, lb, wco, bco, wout]
    return pl.pallas_call(
        functools.partial(_merge_kernel, tiles_per_seq=seq // tm),
        grid=(m // tm,),
        in_specs=[row(o.shape[-1]), row(ch), halo, row(sg.shape[-1]), row(d)] + [_resident(w.shape) for w in weights],
        out_specs=row(d),
        out_shape=jax.ShapeDtypeStruct((m, d), F32),
        scratch_shapes=[pltpu.VMEM((CONV_HALO + tm, ch), F32)],
        compiler_params=pltpu.CompilerParams(dimension_semantics=("parallel",), vmem_limit_bytes=VMEM_LIMIT),
        name="merge",
    )(o, c, c, sg, x2, *weights)


def _ffn_kernel(h_ref, halo_ref, p_ref, gf_ref, wup_ref, fw_ref, fb_ref, wdn_ref, gp_ref, wpg_ref, wple_ref,
                gfin_ref, out_ref, u_ref, v_ref, *, tiles_per_seq, final_norm):
    first = (pl.program_id(0) % tiles_per_seq) == 0
    h = h_ref[...]
    u_ref[0:FFN_HALO, :] = _rms(jnp.where(first, 0.0, halo_ref[...]), gf_ref[...]).astype(BF16)
    u_ref[FFN_HALO:, :] = _rms(h, gf_ref[...]).astype(BF16)
    v_ref[...] = _dot(u_ref[...], wup_ref[...])
    shift = FFN_HALO - (FFN_CONV_WIDTH - 1)
    rows = h.shape[0]
    v = fb_ref[...]
    for k in range(FFN_CONV_WIDTH):
        v = v + fw_ref[k:k + 1, :] * v_ref[shift + k:shift + k + rows, :]
    d_ff = wdn_ref.shape[0]
    a = _gelu_tanh(v[:, :d_ff]) * v[:, d_ff:]
    h = h + _dot(a.astype(BF16), wdn_ref[...])
    gate = _sigmoid(_dot(_rms(h, gp_ref[...]).astype(BF16), wpg_ref[...]))
    h = h + gate * _dot(p_ref[...].astype(BF16), wple_ref[...])
    out_ref[...] = _rms(h, gfin_ref[...]) if final_norm else h


def _ffn(h1, p3, layer, gf, wup, fw, fb, wdn, gp, wpg, wple, gfin, seq, final_norm):
    m, d = h1.shape
    assert wdn.shape[0] % LANES == 0 and seq % TM_FFN == 0
    tm = TM_FFN
    row = lambda n: pl.BlockSpec((tm, n), lambda i: (i, 0))
    halo = pl.BlockSpec((FFN_HALO, d), lambda i: (jnp.maximum(i * (tm // FFN_HALO) - 1, 0), 0))
    weights = [gf, wup, fw, fb, wdn, gp, wpg, wple, gfin]
    return pl.pallas_call(
        functools.partial(_ffn_kernel, tiles_per_seq=seq // tm, final_norm=final_norm),
        grid=(m // tm,),
        in_specs=[row(d), halo, pl.BlockSpec((None, tm, p3.shape[-1]), lambda i: (layer, i, 0))]
        + [_resident(w.shape) for w in weights],
        out_specs=row(d),
        out_shape=jax.ShapeDtypeStruct((m, d), F32),
        scratch_shapes=[pltpu.VMEM((FFN_HALO + tm, d), BF16), pltpu.VMEM((FFN_HALO + tm, wup.shape[1]), F32)],
        compiler_params=pltpu.CompilerParams(dimension_semantics=("parallel",), vmem_limit_bytes=VMEM_LIMIT),
        name="ffn_ple",
    )(h1, h1, p3, *weights)


def _sel_map_t(seq):
    n_cmp = (seq - CMP_LEN) // CMP_STRIDE + 1
    n_sel = seq // SEL_LEN
    c0 = np.arange(n_cmp) * CMP_STRIDE
    j0 = np.arange(n_sel) * SEL_LEN
    lo = np.maximum(c0[None, :], j0[:, None])
    hi = np.minimum(c0[None, :] + CMP_LEN, j0[:, None] + SEL_LEN)
    out = np.zeros((N_SEL, N_CMP_PAD), np.float32)
    out[:n_sel, :n_cmp] = np.maximum(hi - lo, 0) / CMP_LEN
    return jnp.asarray(out, BF16)


def _inproj_weights(w):
    d = w.shape[0]
    nq = NSA_HEADS * HEAD_DIM
    kv0 = nq
    gate0 = kv0 + 3 * 2 * NSA_GROUPS * HEAD_DIM
    conv0 = gate0 + 3 * NSA_HEADS
    merge0 = conv0 + _ZCN

    def kv(br, g):
        lo = kv0 + (br * NSA_GROUPS + g) * HEAD_DIM
        return w[:, lo:lo + HEAD_DIM]

    z = jnp.zeros((d, HEAD_DIM), w.dtype)
    tok = [kv(2, 0), z, kv(2, 1), z, kv(4, 0), kv(4, 1), w[:, kv0:kv0 + _CVN], w[:, conv0:merge0],
           w[:, merge0:merge0 + _ZMN]]
    feat = [w[:, :nq] * Q_SCALE, kv(3, 0), kv(3, 1), kv(5, 0), kv(5, 1), w[:, gate0:conv0],
            jnp.zeros((d, _ZGN - 3 * NSA_HEADS), w.dtype)]
    w_tok = jnp.concatenate(tok, axis=1).astype(BF16)
    w_feat = jnp.concatenate(feat, axis=1).T.astype(BF16)
    assert w_tok.shape[1] == _N_TOK and w_feat.shape[0] == _N_FEAT
    return w_tok, w_feat


def kernel(x, p, g_mix, w_in, cmp_pos_k, cmp_pos_v, w_cmp_k1, w_cmp_k2, w_cmp_v1, w_cmp_v2, w_o_nsa, conv_w,
           conv_b, conv_ln_g, conv_ln_b, w_conv_out, b_conv_out, w_out, g_ffn, w_up, ffn_conv_w, ffn_conv_b,
           w_down, g_ple, w_ple_gate, w_ple, g_final):
    batch, seq, d = x.shape
    depth = w_in.shape[0]
    m = batch * seq
    assert seq % TM == 0 and seq % TK == 0
    sel = _sel_map_t(seq)
    row = lambda v: v.reshape(1, -1)
    h = x.reshape(m, d)
    for i in range(depth):
        w_tok, w_feat = _inproj_weights(w_in[i])
        pos = jnp.concatenate([cmp_pos_k[i]] * NSA_GROUPS + [cmp_pos_v[i]] * NSA_GROUPS, axis=1)
        w1 = jnp.stack([w_cmp_k1[i], w_cmp_v1[i]]).astype(BF16)
        zpad = jnp.zeros_like(w_cmp_k2[i])
        w2 = jnp.stack([jnp.concatenate([w_cmp_k2[i], zpad], axis=1),
                        jnp.concatenate([zpad, w_cmp_v2[i]], axis=1)]).astype(BF16)
        w2t = jnp.stack([jnp.concatenate([zpad, w_cmp_k2[i]], axis=1).T,
                         jnp.concatenate([w_cmp_v2[i], zpad], axis=1).T]).astype(BF16)

        qt, ka, kw, cv, c, sg, vst, vwt, gtt = _inproj(h, row(g_mix[i]), w_tok, w_feat, seq)
        kcv, kcvt = _compress(cv, pos, w1, w2, w2t, batch, seq)
        o = _attention(qt, kcv, kcvt, sel, ka, kw, vst, vwt, gtt, batch, seq)
        h = _merge(o, c, sg, h, w_o_nsa[i].astype(BF16), conv_w[i], row(conv_b[i]), row(conv_ln_g[i]),
                   row(conv_ln_b[i]), w_conv_out[i].astype(BF16), row(b_conv_out[i]), w_out[i].astype(BF16), seq)
        h = _ffn(h, p.reshape(depth, m, -1), i, row(g_ffn[i]), w_up[i].astype(BF16), ffn_conv_w[i], row(ffn_conv_b[i]),
                 w_down[i].astype(BF16), row(g_ple[i]), w_ple_gate[i].astype(BF16), w_ple[i].astype(BF16),
                 row(g_final), seq, final_norm=(i == depth - 1))
    return h.reshape(batch, seq, d)
```

```python
import functools

import numpy as np
import jax
import jax.numpy as jnp
from jax import lax
from jax.experimental import pallas as pl
from jax.experimental.pallas import tpu as pltpu

F32 = jnp.float32
BF16 = jnp.bfloat16

NSA_HEADS = 8
NSA_GROUPS = 2
NSA_HPG = NSA_HEADS // NSA_GROUPS
HEAD_DIM = 64
CMP_LEN = 32
CMP_STRIDE = 16
SEL_LEN = 64
SEL_SHIFT = 6
SEL_TOP = 16
WINDOW = 512
FORCE_SCORE = 1e4
CONV_WIDTH = 31
FFN_CONV_WIDTH = 3
EPS = 1e-6
NEG = -1e30
MASK_BIAS = NEG
Q_SCALE = HEAD_DIM ** -0.5 * float(np.log2(np.e))

LANES = 128
SUBLANES = 8
BF16_ROWS = 16
VMEM_LIMIT = 56 * 1024 * 1024

TM = 1024
TM_ROWS = 512
TM_FFN = 512
TQ = 256
TK = 256
VT = 256
WIN_TILES = (WINDOW + TQ) // VT + (1 if TQ % VT else 0)
CONV_HALO = 32
FFN_HALO = BF16_ROWS


def _sigmoid(x):
    return 0.5 * (jnp.tanh(0.5 * x) + 1.0)


def _gelu_tanh(x):
    return 0.5 * x * (1.0 + jnp.tanh(np.sqrt(2.0 / np.pi).astype(np.float32) * (x + 0.044715 * (x * x * x))))


def _rms(x, g):
    return x * lax.rsqrt(jnp.mean(x * x, axis=-1, keepdims=True) + EPS) * g


def _zero_after(x):
    bits = lax.bitcast_convert_type(x, jnp.uint32)
    bits = lax.shift_right_logical(lax.shift_right_logical(bits, jnp.uint32(16)), jnp.uint32(16))
    return lax.bitcast_convert_type(bits, F32)


def _dot(a, b):
    return jnp.dot(a, b, preferred_element_type=F32)


def _dot_nt(a, b):
    return lax.dot_general(a, b, (((1,), (1,)), ((), ())), preferred_element_type=F32)


def _resident(shape):
    nd = len(shape)
    return pl.BlockSpec(shape, lambda *_: (0,) * nd, pipeline_mode=pl.Buffered(1))


_KA0, _KAN = 0, 256
_KW0, _KWN = 256, 128
_CV0, _CVN = 384, 256
_ZC0, _ZCN = 640, 1024
_ZM0, _ZMN = 1664, 2048
_N_TOK = 3712
_Q0, _QN = 0, 512
_VS0, _VSN = 512, 128
_VW0, _VWN = 640, 128
_ZG0, _ZGN = 768, 32
_N_FEAT = 800


def _inproj_kernel(x_ref, g_ref, wt_ref, wf_ref, qt_ref, ka_ref, kw_ref, cv_ref, c_ref, sg_ref, vst_ref, vwt_ref,
                   gtt_ref, *, tiles_per_seq):
    u = _rms(x_ref[...], g_ref[...]).astype(BF16)

    z_tok = _dot_nt(u, wt_ref[...])
    z_feat = _dot_nt(wf_ref[...], u)

    def tok(lo, n):
        return z_tok[:, lo:lo + n]

    def feat(lo, n):
        return z_feat[lo:lo + n, :]

    ka = tok(_KA0, _KAN)
    s0 = (pl.program_id(0) % tiles_per_seq) * TM
    row = lax.broadcasted_iota(jnp.int32, ka.shape, 0) + s0
    col = lax.broadcasted_iota(jnp.int32, ka.shape, 1)
    onehot = ((col & HEAD_DIM) != 0) & ((col & (HEAD_DIM - 1)) == (row >> SEL_SHIFT))
    ka_ref[...] = jnp.where(onehot, 1.0, ka).astype(BF16)
    kw_ref[...] = tok(_KW0, _KWN).astype(BF16)
    cv_ref[...] = tok(_CV0, _CVN)
    zc = tok(_ZC0, _ZCN)
    half = _ZCN // 2
    c_ref[...] = zc[:, :half] * _sigmoid(zc[:, half:])
    sg_ref[...] = _sigmoid(tok(_ZM0, _ZMN))

    qt_ref[...] = feat(_Q0, _QN).astype(BF16)
    gtt_ref[...] = _sigmoid(feat(_ZG0, _ZGN))
    ones = jnp.ones((HEAD_DIM, VT), BF16)
    for src0, dst_ref in ((_VS0, vst_ref), (_VW0, vwt_ref)):
        v = feat(src0, NSA_GROUPS * HEAD_DIM).astype(BF16)
        for t in range(TM // VT):
            for g in range(NSA_GROUPS):
                r0 = g * 2 * HEAD_DIM
                dst_ref[t, r0:r0 + HEAD_DIM, :] = v[g * HEAD_DIM:(g + 1) * HEAD_DIM, t * VT:(t + 1) * VT]
                dst_ref[t, r0 + HEAD_DIM:r0 + 2 * HEAD_DIM, :] = ones


def _inproj(x2, g_mix, w_tok, w_feat, seq):
    m, d = x2.shape
    row = lambda n: pl.BlockSpec((TM, n), lambda i: (i, 0))
    colb = lambda n: pl.BlockSpec((n, TM), lambda i: (0, i))
    vt_spec = pl.BlockSpec((TM // VT, NSA_GROUPS * LANES, VT), lambda i: (i, 0, 0))
    vt_shape = jax.ShapeDtypeStruct((m // VT, NSA_GROUPS * LANES, VT), BF16)
    tok_outs = [(_KAN, BF16), (_KWN, BF16), (_CVN, F32), (_ZCN // 2, F32), (_ZMN, F32)]
    return pl.pallas_call(
        functools.partial(_inproj_kernel, tiles_per_seq=seq // TM),
        grid=(m // TM,),
        in_specs=[row(d), _resident((1, d)), _resident(w_tok.shape), _resident(w_feat.shape)],
        out_specs=[colb(_QN)] + [row(n) for n, _ in tok_outs] + [vt_spec, vt_spec, colb(_ZGN)],
        out_shape=([jax.ShapeDtypeStruct((_QN, m), BF16)]
                   + [jax.ShapeDtypeStruct((m, n), dt) for n, dt in tok_outs]
                   + [vt_shape, vt_shape, jax.ShapeDtypeStruct((_ZGN, m), F32)]),
        compiler_params=pltpu.CompilerParams(dimension_semantics=("parallel",), vmem_limit_bytes=VMEM_LIMIT),
        name="inproj",
    )(x2, g_mix, w_tok, w_feat)


N_CMP_PAD = 256
_CMP_HALF = CMP_LEN // 2


def _compress_kernel(ck_ref, cv_ref, pos_ref, w1_ref, w2_ref, w2t_ref, out_ref, outt_ref, *, n_cmp):
    hid = w1_ref.shape[-1]
    first = [jnp.zeros((N_CMP_PAD, hid), F32) for _ in range(4)]
    second = [jnp.zeros((N_CMP_PAD, hid), F32) for _ in range(4)]
    for l in range(_CMP_HALF):
        for kind, src_ref in enumerate((ck_ref, cv_ref)):
            xl = src_ref[pl.ds(l, N_CMP_PAD, stride=CMP_STRIDE), :]
            pcol = slice(kind * LANES, (kind + 1) * LANES)
            xa = (xl + pos_ref[l:l + 1, pcol]).astype(BF16)
            xb = (xl + pos_ref[_CMP_HALF + l:_CMP_HALF + l + 1, pcol]).astype(BF16)
            wa = w1_ref[kind, l * HEAD_DIM:(l + 1) * HEAD_DIM, :]
            wb = w1_ref[kind, (_CMP_HALF + l) * HEAD_DIM:(_CMP_HALF + l + 1) * HEAD_DIM, :]
            for g in range(NSA_GROUPS):
                kg = kind * NSA_GROUPS + g
                cols = slice(g * HEAD_DIM, (g + 1) * HEAD_DIM)
                first[kg] = first[kg] + _dot(xa[:, cols], wa)
                second[kg] = second[kg] + _dot(xb[:, cols], wb)
    rows = lax.broadcasted_iota(jnp.int32, (N_CMP_PAD, LANES), 0)
    lanes = lax.broadcasted_iota(jnp.int32, (LANES, N_CMP_PAD), 1)
    for g in range(NSA_GROUPS):
        acts = []
        for kind in range(2):
            kg = kind * NSA_GROUPS + g
            h1 = first[kg] + pltpu.roll(second[kg], N_CMP_PAD - 1, 0)
            acts.append(_gelu_tanh(h1).astype(BF16))
        kcv = _dot(acts[0], w2_ref[0]) + _dot(acts[1], w2_ref[1])
        kcvt = _dot_nt(w2t_ref[0], acts[0]) + _dot_nt(w2t_ref[1], acts[1])
        out_ref[0, g] = jnp.where(rows < n_cmp, kcv, 0.0).astype(BF16)
        outt_ref[0, g] = jnp.where(lanes < n_cmp, kcvt, 0.0).astype(BF16)


def _compress(cv, pos, w1, w2, w2t, batch, seq):
    n_cmp = (seq - CMP_LEN) // CMP_STRIDE + 1
    assert n_cmp < N_CMP_PAD and seq == N_CMP_PAD * CMP_STRIDE
    return pl.pallas_call(
        functools.partial(_compress_kernel, n_cmp=n_cmp),
        grid=(batch,),
        in_specs=[pl.BlockSpec((seq, LANES), lambda b: (b, 0)), pl.BlockSpec((seq, LANES), lambda b: (b, 1)),
                  _resident(pos.shape), _resident(w1.shape), _resident(w2.shape), _resident(w2t.shape)],
        out_specs=[pl.BlockSpec((1, NSA_GROUPS, N_CMP_PAD, LANES), lambda b: (b, 0, 0, 0)),
                   pl.BlockSpec((1, NSA_GROUPS, LANES, N_CMP_PAD), lambda b: (b, 0, 0, 0))],
        out_shape=[jax.ShapeDtypeStruct((batch, NSA_GROUPS, N_CMP_PAD, LANES), BF16),
                   jax.ShapeDtypeStruct((batch, NSA_GROUPS, LANES, N_CMP_PAD), BF16)],
        compiler_params=pltpu.CompilerParams(dimension_semantics=("parallel",), vmem_limit_bytes=VMEM_LIMIT),
        name="compress",
    )(cv, cv, pos, w1, w2, w2t)


N_SEL = 64


class _Group:
    def __init__(self, stream, g, qb, col0, refs):
        (self.qt_ref, self.kcv_ref, self.kcvt_ref, self.sel_ref, self.ka_ref, self.kw_ref, self.vst_ref,
         self.vwt_ref, self.gtt_ref, self.o_ref, q0_ref, q1_ref, qs_ref, m_ref, acc_ref, s_ref, smax_ref,
         p_ref, alpha_ref, oc_ref, ow_ref) = refs
        self.g = g
        self.t0 = qb * TQ
        self.cols = NSA_HPG * TQ
        self.rows = slice(g * LANES, (g + 1) * LANES)
        self.qcols = slice(col0, col0 + TQ)
        k = stream
        self.q0_ref, self.q1_ref, self.qs_ref = q0_ref.at[k], q1_ref.at[k], qs_ref.at[k]
        self.m_ref, self.acc_ref = m_ref.at[k], acc_ref.at[k]
        self.s_ref, self.smax_ref, self.p_ref, self.alpha_ref = s_ref.at[k], smax_ref.at[k], p_ref.at[k], alpha_ref.at[k]
        self.oc_ref, self.ow_ref = oc_ref.at[k], ow_ref.at[k]
        self.tq = self.t0 + (lax.broadcasted_iota(jnp.int32, (1, self.cols), 1) & (TQ - 1))
        self.ksub = lax.broadcasted_iota(jnp.int32, (TK, 1), 0)

    @staticmethod
    def per_head(x):
        return jnp.concatenate([x] * NSA_HPG, axis=1)

    def pre(self):
        g, t0, cols = self.g, self.t0, self.cols
        zeros = jnp.zeros((HEAD_DIM, TQ), BF16)
        for h in range(NSA_HPG):
            r0 = (g * NSA_HPG + h) * HEAD_DIM
            qh = self.qt_ref[r0:r0 + HEAD_DIM, self.qcols]
            cs = slice(h * TQ, (h + 1) * TQ)
            self.q0_ref[0:HEAD_DIM, cs] = qh
            self.q0_ref[HEAD_DIM:, cs] = zeros
            self.q1_ref[0:HEAD_DIM, cs] = zeros
            self.q1_ref[HEAD_DIM:, cs] = qh
            self.qs_ref[0:HEAD_DIM, cs] = qh
            self.qs_ref[HEAD_DIM:, cs] = zeros
        q0 = self.q0_ref[...]
        sc = _dot(self.kcv_ref[0, g], q0)
        wt0 = jnp.maximum(t0 - WINDOW, 0) // VT
        kstart = pl.multiple_of(wt0 * VT, VT)
        span = WIN_TILES * VT
        qwin = q0 if g == 0 else self.q1_ref[...]
        sw = _dot(self.kw_ref[pl.ds(kstart, span), :], qwin)

        tq1 = t0 + lax.broadcasted_iota(jnp.int32, (1, TQ), 1)
        cend = lax.broadcasted_iota(jnp.int32, (N_CMP_PAD, 1), 0) * CMP_STRIDE + (CMP_LEN - 1)
        sc = sc + self.per_head(jnp.where(cend <= tq1, 0.0, NEG))
        ec = jnp.exp2(sc - jnp.max(sc, axis=0, keepdims=True))
        lc = jnp.sum(ec, axis=0, keepdims=True)
        visible = self.tq >= CMP_LEN - 1
        dist = tq1 - (kstart + lax.broadcasted_iota(jnp.int32, (span, 1), 0))
        sw = sw + self.per_head(jnp.where(dist.astype(jnp.uint32) < WINDOW, 0.0, NEG))
        wmax = jnp.max(sw, axis=0, keepdims=True)
        pc = ec * (jnp.where(visible, 1.0 / lc, 0.0) + _zero_after(wmax))
        self.oc_ref[...] = _dot(self.kcvt_ref[0, g], pc.astype(BF16))[0:HEAD_DIM]

        psum = pc[:, 0:TQ]
        for h in range(1, NSA_HPG):
            psum = psum + pc[:, h * TQ:(h + 1) * TQ]
        sel = self.sel_ref[...]
        p_hi = psum.astype(BF16)
        rem = psum - p_hi.astype(F32)
        p_mid = rem.astype(BF16)
        p_lo = (rem - p_mid.astype(F32)).astype(BF16)
        imp = (_dot(sel, p_hi) + _dot(sel, p_mid)) + _dot(sel, p_lo)

        pw = jnp.exp2(sw - wmax).astype(BF16)
        vw = jnp.concatenate([self.vwt_ref[wt0 + c, self.rows, :] for c in range(WIN_TILES)], axis=1)
        ow = _dot(vw, pw)
        self.ow_ref[...] = ow[0:HEAD_DIM] * (1.0 / ow[HEAD_DIM:HEAD_DIM + 1])

        j = lax.broadcasted_iota(jnp.int32, (N_SEL, TQ), 0)
        cur = (t0 + lax.broadcasted_iota(jnp.int32, (N_SEL, TQ), 1)) >> SEL_SHIFT
        forced = (j == 0) | (j == cur) | (j == cur - 1)
        return jnp.where(j <= cur, jnp.where(forced, FORCE_SCORE, imp), NEG)

    def set_bias(self, row_blocks, score):
        jloc = lax.broadcasted_iota(jnp.int32, (SUBLANES, TQ), 0)
        ranks = []
        for r in row_blocks:
            blk = score[r * SUBLANES:(r + 1) * SUBLANES]
            rank = jnp.zeros((SUBLANES, TQ), jnp.int32)
            for i in range(N_SEL):
                vi = jnp.broadcast_to(score[i:i + 1, :], (SUBLANES, TQ))
                if r * SUBLANES > i:
                    before = vi >= blk
                elif (r + 1) * SUBLANES - 1 <= i:
                    before = vi > blk
                else:
                    before = (vi > blk) | ((vi == blk) & (jloc > i - r * SUBLANES))
                rank = rank + before.astype(jnp.int32)
            ranks.append(rank)
        bias = jnp.where(jnp.concatenate(ranks, axis=0) < SEL_TOP, 0.0, MASK_BIAS).astype(BF16)
        lo = HEAD_DIM + row_blocks[0] * SUBLANES
        for h in range(NSA_HPG):
            self.qs_ref[lo:lo + bias.shape[0], h * TQ:(h + 1) * TQ] = bias

    def start(self, score):
        self.m_ref[...] = jnp.full(self.m_ref.shape, NEG, F32)
        self.acc_ref[...] = jnp.zeros(self.acc_ref.shape, F32)
        self.p_ref[1] = jnp.zeros(self.p_ref.shape[1:], BF16)
        self.alpha_ref[1] = jnp.ones(self.alpha_ref.shape[1:], F32)
        n_row_blocks = N_SEL // SUBLANES
        first_blocks = max(TK // SEL_LEN, BF16_ROWS) // SUBLANES
        self.set_bias(list(range(first_blocks)), score)
        started = self.logits(0, 0)
        held = score + _zero_after(jnp.concatenate([started] * n_row_blocks, axis=0))
        self.set_bias(list(range(first_blocks, n_row_blocks)), held)

    def logits(self, kt, slot):
        k0 = pl.multiple_of(kt * TK, TK)
        s = _dot(self.ka_ref[pl.ds(k0, TK), self.rows], self.qs_ref[...])
        self.s_ref[slot] = s
        self.smax_ref[slot] = jnp.max(s, axis=0, keepdims=True)
        return s[0:SUBLANES, 0:TQ]

    def softmax(self, kt, slot, causal, after=None):
        s = self.s_ref[slot]
        if causal:
            s = jnp.where(kt * TK + self.ksub <= self.tq, s, NEG)
            smax = jnp.max(s, axis=0, keepdims=True)
        else:
            smax = self.smax_ref[slot]
        m_prev = self.m_ref[...]
        m_new = jnp.maximum(m_prev, smax)
        self.p_ref[slot] = jnp.exp2(s - m_new).astype(BF16)
        alpha = jnp.exp2(m_prev - m_new)
        if after is not None:
            alpha = alpha + _zero_after(self.per_head(after[0:1, :]))
        self.alpha_ref[slot] = alpha
        self.m_ref[...] = m_new

    def accumulate(self, kt, slot):
        p = self.p_ref[slot]
        v0 = jnp.maximum(kt, 0) * (TK // VT)
        v = jnp.concatenate([self.vst_ref[v0 + c, self.rows, :] for c in range(TK // VT)], axis=1)
        self.acc_ref[...] = self.alpha_ref[slot] * self.acc_ref[...] + _dot(v, p)

    def finish(self):
        g = self.g
        acc = self.acc_ref[...]
        o_s = acc[0:HEAD_DIM] * (1.0 / acc[HEAD_DIM:HEAD_DIM + 1])
        o_c = self.oc_ref[...]
        o_w = self.ow_ref[...]
        heads = []
        for h in range(NSA_HPG):
            cs = slice(h * TQ, (h + 1) * TQ)

            def gate(br):
                row = br * NSA_HEADS + g * NSA_HPG + h
                return self.gtt_ref[row:row + 1, self.qcols]

            heads.append(gate(0) * o_c[:, cs] + gate(1) * o_s[:, cs] + gate(2) * o_w[:, cs])
        width = NSA_HPG * HEAD_DIM
        self.o_ref[self.qcols, g * width:(g + 1) * width] = jnp.concatenate(heads, axis=0).T.astype(BF16)


Q_BLOCKS = 2


def _attn_kernel(*refs):
    step = pl.program_id(1)
    streams = []
    for blk in range(Q_BLOCKS):
        for g in range(NSA_GROUPS):
            streams.append(_Group(len(streams), g, Q_BLOCKS * step + blk, blk * TQ, refs))
    ends_even = streams[:NSA_GROUPS]
    ends_odd = streams[NSA_GROUPS:]
    scores = [st.pre() for st in streams]
    for st, score in zip(streams, scores):
        st.start(score)
    even, odd = 0, 1

    def body(jp, carry):
        a = 2 * jp
        for st in streams:
            st.accumulate(a - 1, odd)
        for st in streams:
            st.softmax(a, even, False, after=st.logits(a + 1, odd))
        for st in streams:
            st.accumulate(a, even)
        for st in streams:
            st.softmax(a + 1, odd, False, after=st.logits(a + 2, even))
        return carry

    lax.fori_loop(0, step, body, 0)
    a = 2 * step
    for st in streams:
        st.accumulate(a - 1, odd)
    for st in ends_even:
        st.softmax(a, even, True)
    for st in ends_odd:
        st.softmax(a, even, False, after=st.logits(a + 1, odd))
    for st in streams:
        st.accumulate(a, even)
    for st in ends_odd:
        st.softmax(a + 1, odd, True)
    for st in ends_odd:
        st.accumulate(a + 1, odd)
    for st in streams:
        st.finish()


def _attention(qt, kcv, kcvt, sel, ka, kw, vst, vwt, gtt, batch, seq):
    tq = Q_BLOCKS * TQ
    nq = seq // tq
    cols = NSA_HPG * TQ
    ng = NSA_GROUPS
    ns = Q_BLOCKS * ng
    assert seq // SEL_LEN == N_SEL and WIN_TILES * VT <= seq and TK % VT == 0 and seq % tq == 0
    assert TK == TQ and Q_BLOCKS == 2, "the static pipeline drain relies on block qb ending on tile qb"
    assert WINDOW % VT == 0 and (TQ % VT == 0 or VT % TQ == 0)
    vt_spec = pl.BlockSpec((seq // VT, ng * LANES, VT), lambda b, i: (b, 0, 0))
    return pl.pallas_call(
        _attn_kernel,
        grid=(batch, nq),
        in_specs=[
            pl.BlockSpec((NSA_HEADS * HEAD_DIM, tq), lambda b, i: (0, b * nq + i)),
            pl.BlockSpec((1, ng, N_CMP_PAD, LANES), lambda b, i: (b, 0, 0, 0)),
            pl.BlockSpec((1, ng, LANES, N_CMP_PAD), lambda b, i: (b, 0, 0, 0)),
            pl.BlockSpec(sel.shape, lambda b, i: (0, 0)),
            pl.BlockSpec((seq, ng * LANES), lambda b, i: (b, 0)),
            pl.BlockSpec((seq, LANES), lambda b, i: (b, 0)),
            vt_spec, vt_spec,
            pl.BlockSpec((_ZGN, tq), lambda b, i: (0, b * nq + i)),
        ],
        out_specs=pl.BlockSpec((tq, NSA_HEADS * HEAD_DIM), lambda b, i: (b * nq + i, 0)),
        out_shape=jax.ShapeDtypeStruct((batch * seq, NSA_HEADS * HEAD_DIM), BF16),
        scratch_shapes=[
            pltpu.VMEM((ns, LANES, cols), BF16),
            pltpu.VMEM((ns, LANES, cols), BF16),
            pltpu.VMEM((ns, LANES, cols), BF16),
            pltpu.VMEM((ns, 1, cols), F32),
            pltpu.VMEM((ns, LANES, cols), F32),
            pltpu.VMEM((ns, 2, TK, cols), F32),
            pltpu.VMEM((ns, 2, 1, cols), F32),
            pltpu.VMEM((ns, 2, TK, cols), BF16),
            pltpu.VMEM((ns, 2, 1, cols), F32),
            pltpu.VMEM((ns, HEAD_DIM, cols), F32),
            pltpu.VMEM((ns, HEAD_DIM, cols), F32),
        ],
        compiler_params=pltpu.CompilerParams(dimension_semantics=("parallel", "arbitrary"),
                                             vmem_limit_bytes=VMEM_LIMIT),
        name="nsa_attention",
    )(qt, kcv, kcvt, sel, ka, kw, vst, vwt, gtt)


def _merge_kernel(o_ref, c_ref, halo_ref, sg_ref, x_ref, wo_ref, cw_ref, cb_ref, lg_ref, lb_ref, wco_ref,
                  bco_ref, wout_ref, h_ref, cext_ref, *, tiles_per_seq):
    first = (pl.program_id(0) % tiles_per_seq) == 0
    cext_ref[0:CONV_HALO, :] = jnp.where(first, 0.0, halo_ref[...])
    cext_ref[CONV_HALO:, :] = c_ref[...]
    shift = CONV_HALO - (CONV_WIDTH - 1)
    rows = c_ref.shape[0]
    acc = jnp.broadcast_to(cb_ref[...], c_ref.shape)
    for res in range(SUBLANES):
        taps = [k for k in range(CONV_WIDTH) if (shift + k) % SUBLANES == res]
        n_rows = rows + (SUBLANES if res else 0)
        part = None
        for k in taps:
            base = shift + k - res
            term = cw_ref[k:k + 1, :] * cext_ref[base:base + n_rows, :]
            part = term if part is None else part + term
        acc = acc + part[res:res + rows]
    xc = acc - jnp.mean(acc, axis=-1, keepdims=True)
    y = xc * lax.rsqrt(jnp.mean(xc * xc, axis=-1, keepdims=True) + EPS) * lg_ref[...] + lb_ref[...]
    y_b = _dot((y * _sigmoid(y)).astype(BF16), wco_ref[...]) + bco_ref[...]
    y_a = _dot(o_ref[...], wo_ref[...])
    d = y_a.shape[-1]
    mix = sg_ref[:, :d] * y_a + sg_ref[:, d:] * y_b
    h_ref[...] = x_ref[...] + _dot(mix.astype(BF16), wout_ref[...])


def _merge(o, c, sg, x2, wo, cw, cb, lg, lb, wco, bco, wout, seq):
    m, d = x2.shape
    ch = c.shape[-1]
    tm = TM_ROWS
    assert seq % tm == 0
    row = lambda n: pl.BlockSpec((tm, n), lambda i: (i, 0))
    halo = pl.BlockSpec((CONV_HALO, ch), lambda i: (jnp.maximum(i * (tm // CONV_HALO) - 1, 0), 0))
    weights = [wo, cw, cb, lg, lb, wco, bco, wout]
    return pl.pallas_call(
        functools.partial(_merge_kernel, tiles_per_seq=seq // tm),
        grid=(m // tm,),
        in_specs=[row(o.shape[-1]), row(ch), halo, row(sg.shape[-1]), row(d)] + [_resident(w.shape) for w in weights],
        out_specs=row(d),
        out_shape=jax.ShapeDtypeStruct((m, d), F32),
        scratch_shapes=[pltpu.VMEM((CONV_HALO + tm, ch), F32)],
        compiler_params=pltpu.CompilerParams(dimension_semantics=("parallel",), vmem_limit_bytes=VMEM_LIMIT),
        name="merge",
    )(o, c, c, sg, x2, *weights)


def _ffn_kernel(h_ref, halo_ref, p_ref, gf_ref, wup_ref, fw_ref, fb_ref, wdn_ref, gp_ref, wpg_ref, wple_ref,
                gfin_ref, out_ref, u_ref, v_ref, *, tiles_per_seq, final_norm):
    first = (pl.program_id(0) % tiles_per_seq) == 0
    h = h_ref[...]
    u_ref[0:FFN_HALO, :] = _rms(jnp.where(first, 0.0, halo_ref[...]), gf_ref[...]).astype(BF16)
    u_ref[FFN_HALO:, :] = _rms(h, gf_ref[...]).astype(BF16)
    v_ref[...] = _dot(u_ref[...], wup_ref[...])
    shift = FFN_HALO - (FFN_CONV_WIDTH - 1)
    rows = h.shape[0]
    v = fb_ref[...]
    for k in range(FFN_CONV_WIDTH):
        v = v + fw_ref[k:k + 1, :] * v_ref[shift + k:shift + k + rows, :]
    d_ff = wdn_ref.shape[0]
    a = _gelu_tanh(v[:, :d_ff]) * v[:, d_ff:]
    h = h + _dot(a.astype(BF16), wdn_ref[...])
    gate = _sigmoid(_dot(_rms(h, gp_ref[...]).astype(BF16), wpg_ref[...]))
    h = h + gate * _dot(p_ref[...].astype(BF16), wple_ref[...])
    out_ref[...] = _rms(h, gfin_ref[...]) if final_norm else h


def _ffn(h1, p3, layer, gf, wup, fw, fb, wdn, gp, wpg, wple, gfin, seq, final_norm):
    m, d = h1.shape
    assert wdn.shape[0] % LANES == 0 and seq % TM_FFN == 0
    tm = TM_FFN
    row = lambda n: pl.BlockSpec((tm, n), lambda i: (i, 0))
    halo = pl.BlockSpec((FFN_HALO, d), lambda i: (jnp.maximum(i * (tm // FFN_HALO) - 1, 0), 0))
    weights = [gf, wup, fw, fb, wdn, gp, wpg, wple, gfin]
    return pl.pallas_call(
        functools.partial(_ffn_kernel, tiles_per_seq=seq // tm, final_norm=final_norm),
        grid=(m // tm,),
        in_specs=[row(d), halo, pl.BlockSpec((None, tm, p3.shape[-1]), lambda i: (layer, i, 0))]
        + [_resident(w.shape) for w in weights],
        out_specs=row(d),
        out_shape=jax.ShapeDtypeStruct((m, d), F32),
        scratch_shapes=[pltpu.VMEM((FFN_HALO + tm, d), BF16), pltpu.VMEM((FFN_HALO + tm, wup.shape[1]), F32)],
        compiler_params=pltpu.CompilerParams(dimension_semantics=("parallel",), vmem_limit_bytes=VMEM_LIMIT),
        name="ffn_ple",
    )(h1, h1, p3, *weights)


def _sel_map_t(seq):
    n_cmp = (seq - CMP_LEN) // CMP_STRIDE + 1
    n_sel = seq // SEL_LEN
    c0 = np.arange(n_cmp) * CMP_STRIDE
    j0 = np.arange(n_sel) * SEL_LEN
    lo = np.maximum(c0[None, :], j0[:, None])
    hi = np.minimum(c0[None, :] + CMP_LEN, j0[:, None] + SEL_LEN)
    out = np.zeros((N_SEL, N_CMP_PAD), np.float32)
    out[:n_sel, :n_cmp] = np.maximum(hi - lo, 0) / CMP_LEN
    return jnp.asarray(out, BF16)


def _inproj_weights(w):
    d = w.shape[0]
    nq = NSA_HEADS * HEAD_DIM
    kv0 = nq
    gate0 = kv0 + 3 * 2 * NSA_GROUPS * HEAD_DIM
    conv0 = gate0 + 3 * NSA_HEADS
    merge0 = conv0 + _ZCN

    def kv(br, g):
        lo = kv0 + (br * NSA_GROUPS + g) * HEAD_DIM
        return w[:, lo:lo + HEAD_DIM]

    z = jnp.zeros((d, HEAD_DIM), w.dtype)
    tok = [kv(2, 0), z, kv(2, 1), z, kv(4, 0), kv(4, 1), w[:, kv0:kv0 + _CVN], w[:, conv0:merge0],
           w[:, merge0:merge0 + _ZMN]]
    feat = [w[:, :nq] * Q_SCALE, kv(3, 0), kv(3, 1), kv(5, 0), kv(5, 1), w[:, gate0:conv0],
            jnp.zeros((d, _ZGN - 3 * NSA_HEADS), w.dtype)]
    w_tok = jnp.concatenate(tok, axis=1).T.astype(BF16)
    w_feat = jnp.concatenate(feat, axis=1).T.astype(BF16)
    assert w_tok.shape[0] == _N_TOK and w_feat.shape[0] == _N_FEAT
    return w_tok, w_feat


def kernel(x, p, g_mix, w_in, cmp_pos_k, cmp_pos_v, w_cmp_k1, w_cmp_k2, w_cmp_v1, w_cmp_v2, w_o_nsa, conv_w,
           conv_b, conv_ln_g, conv_ln_b, w_conv_out, b_conv_out, w_out, g_ffn, w_up, ffn_conv_w, ffn_conv_b,
           w_down, g_ple, w_ple_gate, w_ple, g_final):
    batch, seq, d = x.shape
    depth = w_in.shape[0]
    m = batch * seq
    assert seq % TM == 0 and seq % TK == 0
    sel = _sel_map_t(seq)
    row = lambda v: v.reshape(1, -1)
    h = x.reshape(m, d)
    for i in range(depth):
        w_tok, w_feat = _inproj_weights(w_in[i])
        pos = jnp.concatenate([cmp_pos_k[i]] * NSA_GROUPS + [cmp_pos_v[i]] * NSA_GROUPS, axis=1)
        w1 = jnp.stack([w_cmp_k1[i], w_cmp_v1[i]]).astype(BF16)
        zpad = jnp.zeros_like(w_cmp_k2[i])
        w2 = jnp.stack([jnp.concatenate([w_cmp_k2[i], zpad], axis=1),
                        jnp.concatenate([zpad, w_cmp_v2[i]], axis=1)]).astype(BF16)
        w2t = jnp.stack([jnp.concatenate([zpad, w_cmp_k2[i]], axis=1).T,
                         jnp.concatenate([w_cmp_v2[i], zpad], axis=1).T]).astype(BF16)

        qt, ka, kw, cv, c, sg, vst, vwt, gtt = _inproj(h, row(g_mix[i]), w_tok, w_feat, seq)
        kcv, kcvt = _compress(cv, pos, w1, w2, w2t, batch, seq)
        o = _attention(qt, kcv, kcvt, sel, ka, kw, vst, vwt, gtt, batch, seq)
        h = _merge(o, c, sg, h, w_o_nsa[i].astype(BF16), conv_w[i], row(conv_b[i]), row(conv_ln_g[i]),
                   row(conv_ln_b[i]), w_conv_out[i].astype(BF16), row(b_conv_out[i]), w_out[i].astype(BF16), seq)
        h = _ffn(h, p.reshape(depth, m, -1), i, row(g_ffn[i]), w_up[i].astype(BF16), ffn_conv_w[i], row(ffn_conv_b[i]),
                 w_down[i].astype(BF16), row(g_ple[i]), w_ple_gate[i].astype(BF16), w_ple[i].astype(BF16),
                 row(g_final), seq, final_norm=(i == depth - 1))
    return h.reshape(batch, seq, d)
```

```python
import functools

import numpy as np
import jax
import jax.numpy as jnp
from jax import lax
from jax.experimental import pallas as pl
from jax.experimental.pallas import tpu as pltpu

F32 = jnp.float32
BF16 = jnp.bfloat16

NSA_HEADS = 8
NSA_GROUPS = 2
NSA_HPG = NSA_HEADS // NSA_GROUPS
HEAD_DIM = 64
CMP_LEN = 32
CMP_STRIDE = 16
SEL_LEN = 64
SEL_SHIFT = 6
SEL_TOP = 16
WINDOW = 512
FORCE_SCORE = 1e4
CONV_WIDTH = 31
FFN_CONV_WIDTH = 3
EPS = 1e-6
NEG = -1e30
MASK_BIAS = NEG
Q_SCALE = HEAD_DIM ** -0.5 * float(np.log2(np.e))

LANES = 128
SUBLANES = 8
BF16_ROWS = 16
VMEM_LIMIT = 56 * 1024 * 1024

TM = 1024
TM_ROWS = 512
TM_FFN = 512
TQ = 256
TK = 256
VT = 256
WIN_TILES = (WINDOW + TQ) // VT + (1 if TQ % VT else 0)
CONV_HALO = 32
FFN_HALO = BF16_ROWS


def _sigmoid(x):
    return 0.5 * (jnp.tanh(0.5 * x) + 1.0)


def _gelu_tanh(x):
    return 0.5 * x * (1.0 + jnp.tanh(np.sqrt(2.0 / np.pi).astype(np.float32) * (x + 0.044715 * (x * x * x))))


def _rms(x, g):
    return x * lax.rsqrt(jnp.mean(x * x, axis=-1, keepdims=True) + EPS) * g


def _zero_after(x):
    bits = lax.bitcast_convert_type(x, jnp.uint32)
    bits = lax.shift_right_logical(lax.shift_right_logical(bits, jnp.uint32(16)), jnp.uint32(16))
    return lax.bitcast_convert_type(bits, F32)


def _dot(a, b):
    return jnp.dot(a, b, preferred_element_type=F32)


def _dot_nt(a, b):
    return lax.dot_general(a, b, (((1,), (1,)), ((), ())), preferred_element_type=F32)


def _resident(shape):
    nd = len(shape)
    return pl.BlockSpec(shape, lambda *_: (0,) * nd, pipeline_mode=pl.Buffered(1))


_KA0, _KAN = 0, 256
_KW0, _KWN = 256, 128
_CV0, _CVN = 384, 256
_ZC0, _ZCN = 640, 1024
_ZM0, _ZMN = 1664, 2048
_N_TOK = 3712
_Q0, _QN = 0, 512
_VS0, _VSN = 512, 128
_VW0, _VWN = 640, 128
_ZG0, _ZGN = 768, 32
_N_FEAT = 800


def _inproj_kernel(x_ref, g_ref, wt_ref, wf_ref, qt_ref, ka_ref, kw_ref, cv_ref, c_ref, sg_ref, vst_ref, vwt_ref,
                   gtt_ref, *, tiles_per_seq):
    u = _rms(x_ref[...], g_ref[...]).astype(BF16)

    z_tok = _dot_nt(u, wt_ref[...])
    z_feat = _dot_nt(wf_ref[...], u)

    def tok(lo, n):
        return z_tok[:, lo:lo + n]

    def feat(lo, n):
        return z_feat[lo:lo + n, :]

    ka = tok(_KA0, _KAN)
    s0 = (pl.program_id(0) % tiles_per_seq) * TM
    row = lax.broadcasted_iota(jnp.int32, ka.shape, 0) + s0
    col = lax.broadcasted_iota(jnp.int32, ka.shape, 1)
    onehot = ((col & HEAD_DIM) != 0) & ((col & (HEAD_DIM - 1)) == (row >> SEL_SHIFT))
    ka_ref[...] = jnp.where(onehot, 1.0, ka).astype(BF16)
    kw_ref[...] = tok(_KW0, _KWN).astype(BF16)
    cv_ref[...] = tok(_CV0, _CVN)
    zc = tok(_ZC0, _ZCN)
    half = _ZCN // 2
    c_ref[...] = zc[:, :half] * _sigmoid(zc[:, half:])
    sg_ref[...] = _sigmoid(tok(_ZM0, _ZMN)).astype(BF16)

    qt_ref[...] = feat(_Q0, _QN).astype(BF16)
    gtt_ref[...] = _sigmoid(feat(_ZG0, _ZGN))
    ones = jnp.ones((HEAD_DIM, VT), BF16)
    for src0, dst_ref in ((_VS0, vst_ref), (_VW0, vwt_ref)):
        v = feat(src0, NSA_GROUPS * HEAD_DIM).astype(BF16)
        for t in range(TM // VT):
            for g in range(NSA_GROUPS):
                r0 = g * 2 * HEAD_DIM
                dst_ref[t, r0:r0 + HEAD_DIM, :] = v[g * HEAD_DIM:(g + 1) * HEAD_DIM, t * VT:(t + 1) * VT]
                dst_ref[t, r0 + HEAD_DIM:r0 + 2 * HEAD_DIM, :] = ones


def _inproj(x2, g_mix, w_tok, w_feat, seq):
    m, d = x2.shape
    row = lambda n: pl.BlockSpec((TM, n), lambda i: (i, 0))
    colb = lambda n: pl.BlockSpec((n, TM), lambda i: (0, i))
    vt_spec = pl.BlockSpec((TM // VT, NSA_GROUPS * LANES, VT), lambda i: (i, 0, 0))
    vt_shape = jax.ShapeDtypeStruct((m // VT, NSA_GROUPS * LANES, VT), BF16)
    tok_outs = [(_KAN, BF16), (_KWN, BF16), (_CVN, F32), (_ZCN // 2, F32), (_ZMN, BF16)]
    return pl.pallas_call(
        functools.partial(_inproj_kernel, tiles_per_seq=seq // TM),
        grid=(m // TM,),
        in_specs=[row(d), _resident((1, d)), _resident(w_tok.shape), _resident(w_feat.shape)],
        out_specs=[colb(_QN)] + [row(n) for n, _ in tok_outs] + [vt_spec, vt_spec, colb(_ZGN)],
        out_shape=([jax.ShapeDtypeStruct((_QN, m), BF16)]
                   + [jax.ShapeDtypeStruct((m, n), dt) for n, dt in tok_outs]
                   + [vt_shape, vt_shape, jax.ShapeDtypeStruct((_ZGN, m), F32)]),
        compiler_params=pltpu.CompilerParams(dimension_semantics=("parallel",), vmem_limit_bytes=VMEM_LIMIT),
        name="inproj",
    )(x2, g_mix, w_tok, w_feat)


N_CMP_PAD = 256
_CMP_HALF = CMP_LEN // 2


def _compress_kernel(ck_ref, cv_ref, pos_ref, w1_ref, w2_ref, w2t_ref, out_ref, outt_ref, *, n_cmp):
    hid = w1_ref.shape[-1]
    first = [jnp.zeros((N_CMP_PAD, hid), F32) for _ in range(4)]
    second = [jnp.zeros((N_CMP_PAD, hid), F32) for _ in range(4)]
    for l in range(_CMP_HALF):
        for kind, src_ref in enumerate((ck_ref, cv_ref)):
            xl = src_ref[pl.ds(l, N_CMP_PAD, stride=CMP_STRIDE), :]
            pcol = slice(kind * LANES, (kind + 1) * LANES)
            xa = (xl + pos_ref[l:l + 1, pcol]).astype(BF16)
            xb = (xl + pos_ref[_CMP_HALF + l:_CMP_HALF + l + 1, pcol]).astype(BF16)
            wa = w1_ref[kind, l * HEAD_DIM:(l + 1) * HEAD_DIM, :]
            wb = w1_ref[kind, (_CMP_HALF + l) * HEAD_DIM:(_CMP_HALF + l + 1) * HEAD_DIM, :]
            for g in range(NSA_GROUPS):
                kg = kind * NSA_GROUPS + g
                cols = slice(g * HEAD_DIM, (g + 1) * HEAD_DIM)
                first[kg] = first[kg] + _dot(xa[:, cols], wa)
                second[kg] = second[kg] + _dot(xb[:, cols], wb)
    rows = lax.broadcasted_iota(jnp.int32, (N_CMP_PAD, LANES), 0)
    lanes = lax.broadcasted_iota(jnp.int32, (LANES, N_CMP_PAD), 1)
    for g in range(NSA_GROUPS):
        acts = []
        for kind in range(2):
            kg = kind * NSA_GROUPS + g
            h1 = first[kg] + pltpu.roll(second[kg], N_CMP_PAD - 1, 0)
            acts.append(_gelu_tanh(h1).astype(BF16))
        kcv = _dot(acts[0], w2_ref[0]) + _dot(acts[1], w2_ref[1])
        kcvt = _dot_nt(w2t_ref[0], acts[0]) + _dot_nt(w2t_ref[1], acts[1])
        out_ref[0, g] = jnp.where(rows < n_cmp, kcv, 0.0).astype(BF16)
        outt_ref[0, g] = jnp.where(lanes < n_cmp, kcvt, 0.0).astype(BF16)


def _compress(cv, pos, w1, w2, w2t, batch, seq):
    n_cmp = (seq - CMP_LEN) // CMP_STRIDE + 1
    assert n_cmp < N_CMP_PAD and seq == N_CMP_PAD * CMP_STRIDE
    return pl.pallas_call(
        functools.partial(_compress_kernel, n_cmp=n_cmp),
        grid=(batch,),
        in_specs=[pl.BlockSpec((seq, LANES), lambda b: (b, 0)), pl.BlockSpec((seq, LANES), lambda b: (b, 1)),
                  _resident(pos.shape), _resident(w1.shape), _resident(w2.shape), _resident(w2t.shape)],
        out_specs=[pl.BlockSpec((1, NSA_GROUPS, N_CMP_PAD, LANES), lambda b: (b, 0, 0, 0)),
                   pl.BlockSpec((1, NSA_GROUPS, LANES, N_CMP_PAD), lambda b: (b, 0, 0, 0))],
        out_shape=[jax.ShapeDtypeStruct((batch, NSA_GROUPS, N_CMP_PAD, LANES), BF16),
                   jax.ShapeDtypeStruct((batch, NSA_GROUPS, LANES, N_CMP_PAD), BF16)],
        compiler_params=pltpu.CompilerParams(dimension_semantics=("parallel",), vmem_limit_bytes=VMEM_LIMIT),
        name="compress",
    )(cv, cv, pos, w1, w2, w2t)


N_SEL = 64


class _Group:
    def __init__(self, stream, g, qb, col0, refs):
        (self.qt_ref, self.kcv_ref, self.kcvt_ref, self.sel_ref, self.ka_ref, self.kw_ref, self.vst_ref,
         self.vwt_ref, self.gtt_ref, self.o_ref, q0_ref, q1_ref, qs_ref, m_ref, acc_ref, s_ref, smax_ref,
         p_ref, alpha_ref, oc_ref, ow_ref) = refs
        self.g = g
        self.t0 = qb * TQ
        self.cols = NSA_HPG * TQ
        self.rows = slice(g * LANES, (g + 1) * LANES)
        self.qcols = slice(col0, col0 + TQ)
        k = stream
        self.q0_ref, self.q1_ref, self.qs_ref = q0_ref.at[k], q1_ref.at[k], qs_ref.at[k]
        self.m_ref, self.acc_ref = m_ref.at[k], acc_ref.at[k]
        self.s_ref, self.smax_ref, self.p_ref, self.alpha_ref = s_ref.at[k], smax_ref.at[k], p_ref.at[k], alpha_ref.at[k]
        self.oc_ref, self.ow_ref = oc_ref.at[k], ow_ref.at[k]
        self.tq = self.t0 + (lax.broadcasted_iota(jnp.int32, (1, self.cols), 1) & (TQ - 1))
        self.ksub = lax.broadcasted_iota(jnp.int32, (TK, 1), 0)

    @staticmethod
    def per_head(x):
        return jnp.concatenate([x] * NSA_HPG, axis=1)

    def pre(self):
        g, t0, cols = self.g, self.t0, self.cols
        zeros = jnp.zeros((HEAD_DIM, TQ), BF16)
        for h in range(NSA_HPG):
            r0 = (g * NSA_HPG + h) * HEAD_DIM
            qh = self.qt_ref[r0:r0 + HEAD_DIM, self.qcols]
            cs = slice(h * TQ, (h + 1) * TQ)
            self.q0_ref[0:HEAD_DIM, cs] = qh
            self.q0_ref[HEAD_DIM:, cs] = zeros
            self.q1_ref[0:HEAD_DIM, cs] = zeros
            self.q1_ref[HEAD_DIM:, cs] = qh
            self.qs_ref[0:HEAD_DIM, cs] = qh
            self.qs_ref[HEAD_DIM:, cs] = zeros
        q0 = self.q0_ref[...]
        sc = _dot(self.kcv_ref[0, g], q0)
        wt0 = jnp.maximum(t0 - WINDOW, 0) // VT
        kstart = pl.multiple_of(wt0 * VT, VT)
        span = WIN_TILES * VT
        qwin = q0 if g == 0 else self.q1_ref[...]
        sw = _dot(self.kw_ref[pl.ds(kstart, span), :], qwin)

        tq1 = t0 + lax.broadcasted_iota(jnp.int32, (1, TQ), 1)
        cend = lax.broadcasted_iota(jnp.int32, (N_CMP_PAD, 1), 0) * CMP_STRIDE + (CMP_LEN - 1)
        sc = sc + self.per_head(jnp.where(cend <= tq1, 0.0, NEG))
        ec = jnp.exp2(sc - jnp.max(sc, axis=0, keepdims=True))
        lc = jnp.sum(ec, axis=0, keepdims=True)
        visible = self.tq >= CMP_LEN - 1
        dist = tq1 - (kstart + lax.broadcasted_iota(jnp.int32, (span, 1), 0))
        sw = sw + self.per_head(jnp.where(dist.astype(jnp.uint32) < WINDOW, 0.0, NEG))
        wmax = jnp.max(sw, axis=0, keepdims=True)
        pc = ec * (jnp.where(visible, 1.0 / lc, 0.0) + _zero_after(wmax))
        self.oc_ref[...] = _dot(self.kcvt_ref[0, g], pc.astype(BF16))[0:HEAD_DIM]

        psum = pc[:, 0:TQ]
        for h in range(1, NSA_HPG):
            psum = psum + pc[:, h * TQ:(h + 1) * TQ]
        sel = self.sel_ref[...]
        p_hi = psum.astype(BF16)
        rem = psum - p_hi.astype(F32)
        p_mid = rem.astype(BF16)
        p_lo = (rem - p_mid.astype(F32)).astype(BF16)
        imp = (_dot(sel, p_hi) + _dot(sel, p_mid)) + _dot(sel, p_lo)

        pw = jnp.exp2(sw - wmax).astype(BF16)
        vw = jnp.concatenate([self.vwt_ref[wt0 + c, self.rows, :] for c in range(WIN_TILES)], axis=1)
        ow = _dot(vw, pw)
        self.ow_ref[...] = ow[0:HEAD_DIM] * (1.0 / ow[HEAD_DIM:HEAD_DIM + 1])

        j = lax.broadcasted_iota(jnp.int32, (N_SEL, TQ), 0)
        cur = (t0 + lax.broadcasted_iota(jnp.int32, (N_SEL, TQ), 1)) >> SEL_SHIFT
        forced = (j == 0) | (j == cur) | (j == cur - 1)
        return jnp.where(j <= cur, jnp.where(forced, FORCE_SCORE, imp), NEG)

    def set_bias(self, row_blocks, score):
        jloc = lax.broadcasted_iota(jnp.int32, (SUBLANES, TQ), 0)
        ranks = []
        for r in row_blocks:
            blk = score[r * SUBLANES:(r + 1) * SUBLANES]
            rank = jnp.zeros((SUBLANES, TQ), jnp.int32)
            for i in range(N_SEL):
                vi = jnp.broadcast_to(score[i:i + 1, :], (SUBLANES, TQ))
                if r * SUBLANES > i:
                    before = vi >= blk
                elif (r + 1) * SUBLANES - 1 <= i:
                    before = vi > blk
                else:
                    before = (vi > blk) | ((vi == blk) & (jloc > i - r * SUBLANES))
                rank = rank + before.astype(jnp.int32)
            ranks.append(rank)
        bias = jnp.where(jnp.concatenate(ranks, axis=0) < SEL_TOP, 0.0, MASK_BIAS).astype(BF16)
        lo = HEAD_DIM + row_blocks[0] * SUBLANES
        for h in range(NSA_HPG):
            self.qs_ref[lo:lo + bias.shape[0], h * TQ:(h + 1) * TQ] = bias

    def start(self, score):
        self.m_ref[...] = jnp.full(self.m_ref.shape, NEG, F32)
        self.acc_ref[...] = jnp.zeros(self.acc_ref.shape, F32)
        self.p_ref[1] = jnp.zeros(self.p_ref.shape[1:], BF16)
        self.alpha_ref[1] = jnp.ones(self.alpha_ref.shape[1:], F32)
        n_row_blocks = N_SEL // SUBLANES
        first_blocks = max(TK // SEL_LEN, BF16_ROWS) // SUBLANES
        self.set_bias(list(range(first_blocks)), score)
        started = self.logits(0, 0)
        held = score + _zero_after(jnp.concatenate([started] * n_row_blocks, axis=0))
        self.set_bias(list(range(first_blocks, n_row_blocks)), held)

    def logits(self, kt, slot):
        k0 = pl.multiple_of(kt * TK, TK)
        s = _dot(self.ka_ref[pl.ds(k0, TK), self.rows], self.qs_ref[...])
        self.s_ref[slot] = s
        self.smax_ref[slot] = jnp.max(s, axis=0, keepdims=True)
        return s[0:SUBLANES, 0:TQ]

    def softmax(self, kt, slot, causal, after=None):
        s = self.s_ref[slot]
        if causal:
            s = jnp.where(kt * TK + self.ksub <= self.tq, s, NEG)
            smax = jnp.max(s, axis=0, keepdims=True)
        else:
            smax = self.smax_ref[slot]
        m_prev = self.m_ref[...]
        m_new = jnp.maximum(m_prev, smax)
        self.p_ref[slot] = jnp.exp2(s - m_new).astype(BF16)
        alpha = jnp.exp2(m_prev - m_new)
        if after is not None:
            alpha = alpha + _zero_after(self.per_head(after[0:1, :]))
        self.alpha_ref[slot] = alpha
        self.m_ref[...] = m_new

    def accumulate(self, kt, slot):
        p = self.p_ref[slot]
        v0 = jnp.maximum(kt, 0) * (TK // VT)
        v = jnp.concatenate([self.vst_ref[v0 + c, self.rows, :] for c in range(TK // VT)], axis=1)
        self.acc_ref[...] = self.alpha_ref[slot] * self.acc_ref[...] + _dot(v, p)

    def finish(self):
        g = self.g
        acc = self.acc_ref[...]
        o_s = acc[0:HEAD_DIM] * (1.0 / acc[HEAD_DIM:HEAD_DIM + 1])
        o_c = self.oc_ref[...]
        o_w = self.ow_ref[...]
        heads = []
        for h in range(NSA_HPG):
            cs = slice(h * TQ, (h + 1) * TQ)

            def gate(br):
                row = br * NSA_HEADS + g * NSA_HPG + h
                return self.gtt_ref[row:row + 1, self.qcols]

            heads.append(gate(0) * o_c[:, cs] + gate(1) * o_s[:, cs] + gate(2) * o_w[:, cs])
        width = NSA_HPG * HEAD_DIM
        self.o_ref[self.qcols, g * width:(g + 1) * width] = jnp.concatenate(heads, axis=0).T.astype(BF16)


Q_BLOCKS = 2


def _attn_kernel(*refs):
    step = pl.program_id(1)
    streams = []
    for blk in range(Q_BLOCKS):
        for g in range(NSA_GROUPS):
            streams.append(_Group(len(streams), g, Q_BLOCKS * step + blk, blk * TQ, refs))
    ends_even = streams[:NSA_GROUPS]
    ends_odd = streams[NSA_GROUPS:]
    scores = [st.pre() for st in streams]
    for st, score in zip(streams, scores):
        st.start(score)
    even, odd = 0, 1

    def body(jp, carry):
        a = 2 * jp
        for st in streams:
            st.accumulate(a - 1, odd)
        for st in streams:
            st.softmax(a, even, False, after=st.logits(a + 1, odd))
        for st in streams:
            st.accumulate(a, even)
        for st in streams:
            st.softmax(a + 1, odd, False, after=st.logits(a + 2, even))
        return carry

    lax.fori_loop(0, step, body, 0)
    a = 2 * step
    for st in streams:
        st.accumulate(a - 1, odd)
    for st in ends_even:
        st.softmax(a, even, True)
    for st in ends_odd:
        st.softmax(a, even, False, after=st.logits(a + 1, odd))
    for st in streams:
        st.accumulate(a, even)
    for st in ends_odd:
        st.softmax(a + 1, odd, True)
    for st in ends_odd:
        st.accumulate(a + 1, odd)
    for st in streams:
        st.finish()


def _attention(qt, kcv, kcvt, sel, ka, kw, vst, vwt, gtt, batch, seq):
    tq = Q_BLOCKS * TQ
    nq = seq // tq
    cols = NSA_HPG * TQ
    ng = NSA_GROUPS
    ns = Q_BLOCKS * ng
    assert seq // SEL_LEN == N_SEL and WIN_TILES * VT <= seq and TK % VT == 0 and seq % tq == 0
    assert TK == TQ and Q_BLOCKS == 2, "the static pipeline drain relies on block qb ending on tile qb"
    assert WINDOW % VT == 0 and (TQ % VT == 0 or VT % TQ == 0)
    vt_spec = pl.BlockSpec((seq // VT, ng * LANES, VT), lambda b, i: (b, 0, 0))
    return pl.pallas_call(
        _attn_kernel,
        grid=(batch, nq),
        in_specs=[
            pl.BlockSpec((NSA_HEADS * HEAD_DIM, tq), lambda b, i: (0, b * nq + i)),
            pl.BlockSpec((1, ng, N_CMP_PAD, LANES), lambda b, i: (b, 0, 0, 0)),
            pl.BlockSpec((1, ng, LANES, N_CMP_PAD), lambda b, i: (b, 0, 0, 0)),
            pl.BlockSpec(sel.shape, lambda b, i: (0, 0)),
            pl.BlockSpec((seq, ng * LANES), lambda b, i: (b, 0)),
            pl.BlockSpec((seq, LANES), lambda b, i: (b, 0)),
            vt_spec, vt_spec,
            pl.BlockSpec((_ZGN, tq), lambda b, i: (0, b * nq + i)),
        ],
        out_specs=pl.BlockSpec((tq, NSA_HEADS * HEAD_DIM), lambda b, i: (b * nq + i, 0)),
        out_shape=jax.ShapeDtypeStruct((batch * seq, NSA_HEADS * HEAD_DIM), BF16),
        scratch_shapes=[
            pltpu.VMEM((ns, LANES, cols), BF16),
            pltpu.VMEM((ns, LANES, cols), BF16),
            pltpu.VMEM((ns, LANES, cols), BF16),
            pltpu.VMEM((ns, 1, cols), F32),
            pltpu.VMEM((ns, LANES, cols), F32),
            pltpu.VMEM((ns, 2, TK, cols), F32),
            pltpu.VMEM((ns, 2, 1, cols), F32),
            pltpu.VMEM((ns, 2, TK, cols), BF16),
            pltpu.VMEM((ns, 2, 1, cols), F32),
            pltpu.VMEM((ns, HEAD_DIM, cols), F32),
            pltpu.VMEM((ns, HEAD_DIM, cols), F32),
        ],
        compiler_params=pltpu.CompilerParams(dimension_semantics=("parallel", "arbitrary"),
                                             vmem_limit_bytes=VMEM_LIMIT),
        name="nsa_attention",
    )(qt, kcv, kcvt, sel, ka, kw, vst, vwt, gtt)


def _merge_kernel(o_ref, c_ref, halo_ref, sg_ref, x_ref, wo_ref, cw_ref, cb_ref, lg_ref, lb_ref, wco_ref,
                  bco_ref, wout_ref, h_ref, cext_ref, *, tiles_per_seq):
    first = (pl.program_id(0) % tiles_per_seq) == 0
    cext_ref[0:CONV_HALO, :] = jnp.where(first, 0.0, halo_ref[...])
    cext_ref[CONV_HALO:, :] = c_ref[...]
    shift = CONV_HALO - (CONV_WIDTH - 1)
    rows = c_ref.shape[0]
    acc = jnp.broadcast_to(cb_ref[...], c_ref.shape)
    for res in range(SUBLANES):
        taps = [k for k in range(CONV_WIDTH) if (shift + k) % SUBLANES == res]
        n_rows = rows + (SUBLANES if res else 0)
        part = None
        for k in taps:
            base = shift + k - res
            term = cw_ref[k:k + 1, :] * cext_ref[base:base + n_rows, :]
            part = term if part is None else part + term
        acc = acc + part[res:res + rows]
    xc = acc - jnp.mean(acc, axis=-1, keepdims=True)
    y = xc * lax.rsqrt(jnp.mean(xc * xc, axis=-1, keepdims=True) + EPS) * lg_ref[...] + lb_ref[...]
    y_b = _dot((y * _sigmoid(y)).astype(BF16), wco_ref[...]) + bco_ref[...]
    y_a = _dot(o_ref[...], wo_ref[...])
    d = y_a.shape[-1]
    mix = sg_ref[:, :d].astype(F32) * y_a + sg_ref[:, d:].astype(F32) * y_b
    h_ref[...] = x_ref[...] + _dot(mix.astype(BF16), wout_ref[...])


def _merge(o, c, sg, x2, wo, cw, cb, lg, lb, wco, bco, wout, seq):
    m, d = x2.shape
    ch = c.shape[-1]
    tm = TM_ROWS
    assert seq % tm == 0
    row = lambda n: pl.BlockSpec((tm, n), lambda i: (i, 0))
    halo = pl.BlockSpec((CONV_HALO, ch), lambda i: (jnp.maximum(i * (tm // CONV_HALO) - 1, 0), 0))
    weights = [wo, cw, cb, lg, lb, wco, bco, wout]
    return pl.pallas_call(
        functools.partial(_merge_kernel, tiles_per_seq=seq // tm),
        grid=(m // tm,),
        in_specs=[row(o.shape[-1]), row(ch), halo, row(sg.shape[-1]), row(d)] + [_resident(w.shape) for w in weights],
        out_specs=row(d),
        out_shape=jax.ShapeDtypeStruct((m, d), F32),
        scratch_shapes=[pltpu.VMEM((CONV_HALO + tm, ch), F32)],
        compiler_params=pltpu.CompilerParams(dimension_semantics=("parallel",), vmem_limit_bytes=VMEM_LIMIT),
        name="merge",
    )(o, c, c, sg, x2, *weights)


def _ffn_kernel(h_ref, halo_ref, p_ref, gf_ref, wup_ref, fw_ref, fb_ref, wdn_ref, gp_ref, wpg_ref, wple_ref,
                gfin_ref, out_ref, u_ref, v_ref, *, tiles_per_seq, final_norm):
    first = (pl.program_id(0) % tiles_per_seq) == 0
    h = h_ref[...]
    u_ref[0:FFN_HALO, :] = _rms(jnp.where(first, 0.0, halo_ref[...]), gf_ref[...]).astype(BF16)
    u_ref[FFN_HALO:, :] = _rms(h, gf_ref[...]).astype(BF16)
    v_ref[...] = _dot(u_ref[...], wup_ref[...])
    shift = FFN_HALO - (FFN_CONV_WIDTH - 1)
    rows = h.shape[0]
    v = fb_ref[...]
    for k in range(FFN_CONV_WIDTH):
        v = v + fw_ref[k:k + 1, :] * v_ref[shift + k:shift + k + rows, :]
    d_ff = wdn_ref.shape[0]
    a = _gelu_tanh(v[:, :d_ff]) * v[:, d_ff:]
    h = h + _dot(a.astype(BF16), wdn_ref[...])
    gate = _sigmoid(_dot(_rms(h, gp_ref[...]).astype(BF16), wpg_ref[...]))
    h = h + gate * _dot(p_ref[...].astype(BF16), wple_ref[...])
    out_ref[...] = _rms(h, gfin_ref[...]) if final_norm else h


def _ffn(h1, p3, layer, gf, wup, fw, fb, wdn, gp, wpg, wple, gfin, seq, final_norm):
    m, d = h1.shape
    assert wdn.shape[0] % LANES == 0 and seq % TM_FFN == 0
    tm = TM_FFN
    row = lambda n: pl.BlockSpec((tm, n), lambda i: (i, 0))
    halo = pl.BlockSpec((FFN_HALO, d), lambda i: (jnp.maximum(i * (tm // FFN_HALO) - 1, 0), 0))
    weights = [gf, wup, fw, fb, wdn, gp, wpg, wple, gfin]
    return pl.pallas_call(
        functools.partial(_ffn_kernel, tiles_per_seq=seq // tm, final_norm=final_norm),
        grid=(m // tm,),
        in_specs=[row(d), halo, pl.BlockSpec((None, tm, p3.shape[-1]), lambda i: (layer, i, 0))]
        + [_resident(w.shape) for w in weights],
        out_specs=row(d),
        out_shape=jax.ShapeDtypeStruct((m, d), F32),
        scratch_shapes=[pltpu.VMEM((FFN_HALO + tm, d), BF16), pltpu.VMEM((FFN_HALO + tm, wup.shape[1]), F32)],
        compiler_params=pltpu.CompilerParams(dimension_semantics=("parallel",), vmem_limit_bytes=VMEM_LIMIT),
        name="ffn_ple",
    )(h1, h1, p3, *weights)


def _sel_map_t(seq):
    n_cmp = (seq - CMP_LEN) // CMP_STRIDE + 1
    n_sel = seq // SEL_LEN
    c0 = np.arange(n_cmp) * CMP_STRIDE
    j0 = np.arange(n_sel) * SEL_LEN
    lo = np.maximum(c0[None, :], j0[:, None])
    hi = np.minimum(c0[None, :] + CMP_LEN, j0[:, None] + SEL_LEN)
    out = np.zeros((N_SEL, N_CMP_PAD), np.float32)
    out[:n_sel, :n_cmp] = np.maximum(hi - lo, 0) / CMP_LEN
    return jnp.asarray(out, BF16)


def _inproj_weights(w):
    d = w.shape[0]
    nq = NSA_HEADS * HEAD_DIM
    kv0 = nq
    gate0 = kv0 + 3 * 2 * NSA_GROUPS * HEAD_DIM
    conv0 = gate0 + 3 * NSA_HEADS
    merge0 = conv0 + _ZCN

    def kv(br, g):
        lo = kv0 + (br * NSA_GROUPS + g) * HEAD_DIM
        return w[:, lo:lo + HEAD_DIM]

    z = jnp.zeros((d, HEAD_DIM), w.dtype)
    tok = [kv(2, 0), z, kv(2, 1), z, kv(4, 0), kv(4, 1), w[:, kv0:kv0 + _CVN], w[:, conv0:merge0],
           w[:, merge0:merge0 + _ZMN]]
    feat = [w[:, :nq] * Q_SCALE, kv(3, 0), kv(3, 1), kv(5, 0), kv(5, 1), w[:, gate0:conv0],
            jnp.zeros((d, _ZGN - 3 * NSA_HEADS), w.dtype)]
    w_tok = jnp.concatenate(tok, axis=1).T.astype(BF16)
    w_feat = jnp.concatenate(feat, axis=1).T.astype(BF16)
    assert w_tok.shape[0] == _N_TOK and w_feat.shape[0] == _N_FEAT
    return w_tok, w_feat


def kernel(x, p, g_mix, w_in, cmp_pos_k, cmp_pos_v, w_cmp_k1, w_cmp_k2, w_cmp_v1, w_cmp_v2, w_o_nsa, conv_w,
           conv_b, conv_ln_g, conv_ln_b, w_conv_out, b_conv_out, w_out, g_ffn, w_up, ffn_conv_w, ffn_conv_b,
           w_down, g_ple, w_ple_gate, w_ple, g_final):
    batch, seq, d = x.shape
    depth = w_in.shape[0]
    m = batch * seq
    assert seq % TM == 0 and seq % TK == 0
    sel = _sel_map_t(seq)
    row = lambda v: v.reshape(1, -1)
    h = x.reshape(m, d)
    for i in range(depth):
        w_tok, w_feat = _inproj_weights(w_in[i])
        pos = jnp.concatenate([cmp_pos_k[i]] * NSA_GROUPS + [cmp_pos_v[i]] * NSA_GROUPS, axis=1)
        w1 = jnp.stack([w_cmp_k1[i], w_cmp_v1[i]]).astype(BF16)
        zpad = jnp.zeros_like(w_cmp_k2[i])
        w2 = jnp.stack([jnp.concatenate([w_cmp_k2[i], zpad], axis=1),
                        jnp.concatenate([zpad, w_cmp_v2[i]], axis=1)]).astype(BF16)
        w2t = jnp.stack([jnp.concatenate([zpad, w_cmp_k2[i]], axis=1).T,
                         jnp.concatenate([w_cmp_v2[i], zpad], axis=1).T]).astype(BF16)

        qt, ka, kw, cv, c, sg, vst, vwt, gtt = _inproj(h, row(g_mix[i]), w_tok, w_feat, seq)
        kcv, kcvt = _compress(cv, pos, w1, w2, w2t, batch, seq)
        o = _attention(qt, kcv, kcvt, sel, ka, kw, vst, vwt, gtt, batch, seq)
        h = _merge(o, c, sg, h, w_o_nsa[i].astype(BF16), conv_w[i], row(conv_b[i]), row(conv_ln_g[i]),
                   row(conv_ln_b[i]), w_conv_out[i].astype(BF16), row(b_conv_out[i]), w_out[i].astype(BF16), seq)
        h = _ffn(h, p.reshape(depth, m, -1), i, row(g_ffn[i]), w_up[i].astype(BF16), ffn_conv_w[i], row(ffn_conv_b[i]),
                 w_down[i].astype(BF16), row(g_ple[i]), w_ple_gate[i].astype(BF16), w_ple[i].astype(BF16),
                 row(g_final), seq, final_norm=(i == depth - 1))
    return h.reshape(batch, seq, d)
```
